```python
import math
import jax, jax.numpy as jnp
from jax import lax
import numpy as np

D_MODEL = 2048
BATCH = 1
SEQ = 8192
DEPTH = 2

DN_HEADS = 8
DN_HEAD_DIM = 128
DN_WIDTH = DN_HEADS * DN_HEAD_DIM
CONV_WIDTH = 4
DN_CHUNK = 64
SA_HEADS = 8
SA_HEAD_DIM = 128
SA_WIDTH = SA_HEADS * SA_HEAD_DIM
IDX_HEADS = 16
IDX_HEAD_DIM = 64
TOPK_MAX = 256
Q_BLOCK = 128
ROPE_THETA = 10000.0
D_FF = 5632
N_EXPERTS = 8
TOP_K_EXPERTS = 2
D_FF_EXPERT = 2816
RMS_EPS = 1e-6
L2_EPS = 1e-6

SPLITS = (3 * DN_WIDTH, DN_WIDTH, DN_HEADS, DN_HEADS,
          SA_WIDTH, SA_WIDTH, SA_WIDTH,
          IDX_HEADS * IDX_HEAD_DIM, IDX_HEAD_DIM, IDX_HEADS,
          D_MODEL, D_MODEL)
IN_WIDTH = sum(SPLITS)

kernel_name = "hybrid_deltanet_dsa_moe"


def rms_norm(x, gain):
    xf = x.astype(jnp.float32)
    y = xf * lax.rsqrt(jnp.mean(xf * xf, axis=-1, keepdims=True) + RMS_EPS)
    return (y * gain.astype(jnp.float32)).astype(x.dtype)


def l2_normalize(x):
    return x * lax.rsqrt(jnp.sum(x * x, axis=-1, keepdims=True) + L2_EPS)


def rope_tables(positions, dim):
    inv_freq = ROPE_THETA ** (-jnp.arange(0, dim, 2, dtype=jnp.float32) / dim)
    ang = positions.astype(jnp.float32)[..., None] * inv_freq
    return jnp.cos(ang), jnp.sin(ang)


def apply_rope(x, cos, sin):
    xf = x.astype(jnp.float32)
    x1, x2 = jnp.split(xf, 2, axis=-1)
    c = cos[:, :, None, :]
    s = sin[:, :, None, :]
    return jnp.concatenate([x1 * c - x2 * s, x2 * c + x1 * s], axis=-1).astype(x.dtype)


def causal_depthwise_conv(x, w):
    c = x.shape[-1]
    return lax.conv_general_dilated(
        x, w[:, None, :].astype(x.dtype), window_strides=(1,),
        padding=[(CONV_WIDTH - 1, 0)], dimension_numbers=("NWC", "WIO", "NWC"),
        feature_group_count=c)


def gated_delta_rule(q, k, v, g, beta):
    B, S, H, Dk = q.shape
    Dv = v.shape[-1]
    n = S // DN_CHUNK

    def to_chunks(t):
        return t.reshape(B, n, DN_CHUNK, H, t.shape[-1]).transpose(1, 0, 3, 2, 4)

    qc, kc, vc = to_chunks(q), to_chunks(k), to_chunks(v)
    gc = jnp.cumsum(g.reshape(B, n, DN_CHUNK, H).transpose(1, 0, 3, 2), axis=-1)
    bc = beta.reshape(B, n, DN_CHUNK, H).transpose(1, 0, 3, 2)

    tril = jnp.tril(jnp.ones((DN_CHUNK, DN_CHUNK), dtype=bool))
    strict = jnp.tril(jnp.ones((DN_CHUNK, DN_CHUNK), dtype=bool), -1)
    eye = jnp.eye(DN_CHUNK, dtype=jnp.float32)
    decay_mat = jnp.exp(jnp.where(tril, gc[..., :, None] - gc[..., None, :], -jnp.inf))

    kb = kc * bc[..., None]
    a_strict = jnp.where(strict, jnp.einsum("nbhcd,nbhsd->nbhcs", kb, kc) * decay_mat, 0.0)
    m = a_strict + eye
    w_c = lax.linalg.triangular_solve(m, kb * jnp.exp(gc)[..., None], left_side=True,
                                      lower=True, unit_diagonal=True)
    u_c = lax.linalg.triangular_solve(m, vc * bc[..., None], left_side=True,
                                      lower=True, unit_diagonal=True)
    attn_intra = jnp.where(tril, jnp.einsum("nbhcd,nbhsd->nbhcs", qc, kc) * decay_mat, 0.0)
    q_dec = qc * jnp.exp(gc)[..., None]
    k_dec = kc * jnp.exp(gc[..., -1:] - gc)[..., None]
    g_last = jnp.exp(gc[..., -1])

    def step(state, inp):
        qd, kd, wc, uc, att, gl = inp
        v_new = uc - jnp.einsum("bhcd,bhde->bhce", wc, state)
        o = jnp.einsum("bhcd,bhde->bhce", qd, state) + jnp.einsum("bhcs,bhse->bhce", att, v_new)
        state = state * gl[..., None, None] + jnp.einsum("bhcd,bhce->bhde", kd, v_new)
        return state, o

    s0 = jnp.zeros((B, H, Dk, Dv), jnp.float32)
    _, o = lax.scan(step, s0, (q_dec, k_dec, w_c, u_c, attn_intra, g_last))
    return o.transpose(1, 0, 3, 2, 4).reshape(B, S, H, Dv)


def dsa_sparse_attention(q, k, v, q_idx, k_idx, w_idx):
    B, S, H, Dh = q.shape
    n_keys = S
    topk = min(TOPK_MAX, n_keys // 4)
    n_blocks = S // Q_BLOCK
    key_pos = jnp.arange(n_keys)
    k_idx_f = k_idx.astype(jnp.float32)
    scale = Dh ** -0.5
    gather = jax.vmap(lambda t, i: t[i])

    def block(i):
        start = i * Q_BLOCK
        qb = lax.dynamic_slice_in_dim(q, start, Q_BLOCK, axis=1)
        qib = lax.dynamic_slice_in_dim(q_idx, start, Q_BLOCK, axis=1).astype(jnp.float32)
        wb = lax.dynamic_slice_in_dim(w_idx, start, Q_BLOCK, axis=1).astype(jnp.float32)
        q_pos = start + jnp.arange(Q_BLOCK)
        rel = jax.nn.relu(jnp.einsum("bqhd,bsd->bqhs", qib, k_idx_f))
        score = jnp.einsum("bqhs,bqh->bqs", rel, wb)
        causal = key_pos[None, :] <= q_pos[:, None]
        score = jnp.where(causal[None], score, -jnp.inf)
        _, sel = lax.top_k(score, topk)
        valid = sel <= q_pos[None, :, None]
        ks = gather(k, sel)
        vs = gather(v, sel)
        logits = jnp.einsum("bqhd,bqkhd->bhqk", qb, ks).astype(jnp.float32) * scale
        logits = jnp.where(valid[:, None], logits, -jnp.inf)
        p = jax.nn.softmax(logits, axis=-1).astype(v.dtype)
        return jnp.einsum("bhqk,bqkhd->bqhd", p, vs)

    out = lax.map(block, jnp.arange(n_blocks))
    return out.transpose(1, 0, 2, 3, 4).reshape(B, S, H, Dh)


def hybrid_mixer(h, cos_sa, sin_sa, cos_ix, sin_ix, w_in, conv_w, a_log, dt_bias, dn_norm,
                 w_dn_out, w_sa_out, w_o):
    B, S, _ = h.shape
    split_points = np.cumsum(SPLITS)[:-1].tolist()
    (qkv_dn, z, a, b, q_sa, k_sa, v_sa, q_ix, k_ix, w_ix, g_dn, g_sa) = jnp.split(
        h @ w_in, split_points, axis=-1)

    qkv = jax.nn.silu(causal_depthwise_conv(qkv_dn, conv_w)).astype(jnp.float32)
    q_dn, k_dn, v_dn = [t.reshape(B, S, DN_HEADS, DN_HEAD_DIM) for t in jnp.split(qkv, 3, axis=-1)]
    q_dn = l2_normalize(q_dn) * (DN_HEAD_DIM ** -0.5)
    k_dn = l2_normalize(k_dn)
    beta = jax.nn.sigmoid(b.astype(jnp.float32))
    g = -jnp.exp(a_log.astype(jnp.float32)) * jax.nn.softplus(
        a.astype(jnp.float32) + dt_bias.astype(jnp.float32))
    o_dn = gated_delta_rule(q_dn, k_dn, v_dn, g, beta)
    o_dn = rms_norm(o_dn, dn_norm) * jax.nn.silu(z.astype(jnp.float32).reshape(B, S, DN_HEADS, DN_HEAD_DIM))
    y_dn = o_dn.reshape(B, S, DN_WIDTH).astype(h.dtype) @ w_dn_out

    q_sa = apply_rope(q_sa.reshape(B, S, SA_HEADS, SA_HEAD_DIM), cos_sa, sin_sa)
    k_sa = apply_rope(k_sa.reshape(B, S, SA_HEADS, SA_HEAD_DIM), cos_sa, sin_sa)
    v_sa = v_sa.reshape(B, S, SA_HEADS, SA_HEAD_DIM)
    q_ix = apply_rope(q_ix.reshape(B, S, IDX_HEADS, IDX_HEAD_DIM), cos_ix, sin_ix)
    k_ix = apply_rope(k_ix[:, :, None, :], cos_ix, sin_ix)[:, :, 0, :]
    w_ix = w_ix * (IDX_HEADS ** -0.5 * IDX_HEAD_DIM ** -0.5)
    o_sa = dsa_sparse_attention(q_sa, k_sa, v_sa, q_ix, k_ix, w_ix)
    y_sa = o_sa.reshape(B, S, SA_WIDTH) @ w_sa_out

    merged = jax.nn.sigmoid(g_dn) * y_dn + jax.nn.sigmoid(g_sa) * y_sa
    return merged @ w_o


def swiglu(h, w_gate, w_up, w_down):
    return (jax.nn.silu(h @ w_gate) * (h @ w_up)) @ w_down


def moe_ffn(h, w_router, w_gate, w_up, w_down):
    B, S, D = h.shape
    t = h.reshape(B * S, D)
    logits = (t @ w_router).astype(jnp.float32)
    top_vals, top_idx = lax.top_k(logits, TOP_K_EXPERTS)
    top_w = jax.nn.softmax(top_vals, axis=-1)
    combine = jnp.sum(jax.nn.one_hot(top_idx, N_EXPERTS, dtype=jnp.float32) * top_w[..., None], axis=1)
    out = jnp.zeros_like(t)
    for e in range(N_EXPERTS):
        out = out + combine[:, e:e + 1].astype(t.dtype) * swiglu(t, w_gate[e], w_up[e], w_down[e])
    return out.reshape(B, S, D)


def setup_inputs(seed: int = 0) -> dict:
    key = jax.random.key(seed)
    ks = jax.random.split(key, 24)
    n_dense = (DEPTH + 1) // 2
    n_moe = DEPTH // 2
    f32 = jnp.float32

    def nrm(k, shape, fan_in):
        return jax.random.normal(k, shape, f32) * (fan_in ** -0.5)

    def gain(k, shape):
        return 1.0 + 0.02 * jax.random.normal(k, shape, f32)

    dt = jnp.exp(jax.random.uniform(ks[6], (DEPTH, DN_HEADS), f32, math.log(1e-3), math.log(1e-1)))
    return {
        "x": jax.random.normal(ks[0], (BATCH, SEQ, D_MODEL), f32),
        "positions": jnp.broadcast_to(jnp.arange(SEQ, dtype=jnp.int32), (BATCH, SEQ)),
        "norm_mix": gain(ks[1], (DEPTH, D_MODEL)),
        "w_in": nrm(ks[2], (DEPTH, D_MODEL, IN_WIDTH), D_MODEL),
        "conv_w": nrm(ks[3], (DEPTH, CONV_WIDTH, 3 * DN_WIDTH), CONV_WIDTH),
        "a_log": jnp.log(jax.random.uniform(ks[4], (DEPTH, DN_HEADS), f32, 1.0, 16.0)),
        "dt_bias": dt + jnp.log(-jnp.expm1(-dt)),
        "dn_norm": gain(ks[5], (DEPTH, DN_HEAD_DIM)),
        "w_dn_out": nrm(ks[7], (DEPTH, DN_WIDTH, D_MODEL), DN_WIDTH),
        "w_sa_out": nrm(ks[8], (DEPTH, SA_WIDTH, D_MODEL), SA_WIDTH),
        "w_o": nrm(ks[9], (DEPTH, D_MODEL, D_MODEL), D_MODEL),
        "norm_ffn": gain(ks[10], (DEPTH, D_MODEL)),
        "dense_w_gate": nrm(ks[11], (n_dense, D_MODEL, D_FF), D_MODEL),
        "dense_w_up": nrm(ks[12], (n_dense, D_MODEL, D_FF), D_MODEL),
        "dense_w_down": nrm(ks[13], (n_dense, D_FF, D_MODEL), D_FF),
        "moe_router": nrm(ks[14], (n_moe, D_MODEL, N_EXPERTS), D_MODEL),
        "moe_w_gate": nrm(ks[15], (n_moe, N_EXPERTS, D_MODEL, D_FF_EXPERT), D_MODEL),
        "moe_w_up": nrm(ks[16], (n_moe, N_EXPERTS, D_MODEL, D_FF_EXPERT), D_MODEL),
        "moe_w_down": nrm(ks[17], (n_moe, N_EXPERTS, D_FF_EXPERT, D_MODEL), D_FF_EXPERT),
        "final_norm": gain(ks[18], (D_MODEL,)),
    }


def reference(x, positions, norm_mix, w_in, conv_w, a_log, dt_bias, dn_norm, w_dn_out, w_sa_out,
              w_o, norm_ffn, dense_w_gate, dense_w_up, dense_w_down, moe_router, moe_w_gate,
              moe_w_up, moe_w_down, final_norm):
    cos_sa, sin_sa = rope_tables(positions, SA_HEAD_DIM)
    cos_ix, sin_ix = rope_tables(positions, IDX_HEAD_DIM)
    for layer in range(DEPTH):
        h = rms_norm(x, norm_mix[layer])
        x = x + hybrid_mixer(h, cos_sa, sin_sa, cos_ix, sin_ix, w_in[layer], conv_w[layer],
                             a_log[layer], dt_bias[layer], dn_norm[layer], w_dn_out[layer],
                             w_sa_out[layer], w_o[layer])
        h = rms_norm(x, norm_ffn[layer])
        j = layer // 2
        if layer % 2 == 0:
            x = x + swiglu(h, dense_w_gate[j], dense_w_up[j], dense_w_down[j])
        else:
            x = x + moe_ffn(h, moe_router[j], moe_w_gate[j], moe_w_up[j], moe_w_down[j])
    return rms_norm(x, final_norm)
```

```python
import functools

import jax
import jax.numpy as jnp
from jax import lax
from jax.experimental import pallas as pl
from jax.experimental.pallas import tpu as pltpu

F32 = jnp.float32
BF16 = jnp.bfloat16
I32 = jnp.int32

RMS_EPS = 1e-6
L2_EPS = 1e-6
DN_HEADS = 8
DN_HEAD_DIM = 128
DN_WIDTH = DN_HEADS * DN_HEAD_DIM
CONV_WIDTH = 4
DN_CHUNK = 64
SA_HEADS = 8
SA_HEAD_DIM = 128
SA_WIDTH = SA_HEADS * SA_HEAD_DIM
IDX_HEADS = 16
IDX_HEAD_DIM = 64
TOPK_MAX = 256
ROPE_THETA = 10000.0
N_EXPERTS = 8

LANES = 128
SUBLANES = 8
VMEM_CAP_BYTES = 56 * 2**20
NEG_BIG = -1e30
INT_MIN = -2**31

COL_QKV = 0
COL_Z = 3072
COL_QSA = 4096
COL_KSA = 5120
COL_VSA = 6144
COL_QIX = 7168
COL_GDN = 8192
COL_GSA = 10240
MAIN_WIDTH = 12288
SM_KIX = 0
SM_A = 64
SM_B = 72
SM_WIX = 80


def _params(semantics, vmem_bytes):
    return pltpu.CompilerParams(dimension_semantics=semantics,
                                vmem_limit_bytes=int(min(max(vmem_bytes, 16 * 2**20), VMEM_CAP_BYTES)))


def _nbytes(shape, dtype):
    n = 1
    for s in shape:
        n *= s
    return n * jnp.dtype(dtype).itemsize


def _sigmoid(x):
    return 1.0 / (1.0 + jnp.exp(-x))


def _dot(a, b):
    return jnp.dot(a, b, preferred_element_type=F32)


def _dot_nt(a, b):
    return lax.dot_general(a, b, (((1,), (1,)), ((), ())), preferred_element_type=F32)


def _dot_tn(a, b):
    return lax.dot_general(a, b, (((0,), (0,)), ((), ())), preferred_element_type=F32)


def _dot_hi(a, b):
    return jnp.dot(a, b, preferred_element_type=F32, precision=lax.Precision.HIGHEST)


def _rmsnorm_body(x_ref, g_ref, o_ref):
    x = x_ref[...]
    ms = jnp.mean(x * x, axis=-1, keepdims=True)
    o_ref[...] = (x * lax.rsqrt(ms + RMS_EPS) * g_ref[...]).astype(o_ref.dtype)


def rmsnorm(x, gain, out_dtype, tm=512):
    m, d = x.shape
    vm = 2 * (_nbytes((tm, d), F32) + _nbytes((tm, d), out_dtype)) + 4 * _nbytes((tm, d), F32)
    return pl.pallas_call(
        _rmsnorm_body,
        grid=(m // tm,),
        in_specs=[pl.BlockSpec((tm, d), lambda i: (i, 0)),
                  pl.BlockSpec((1, d), lambda i: (0, 0))],
        out_specs=pl.BlockSpec((tm, d), lambda i: (i, 0)),
        out_shape=jax.ShapeDtypeStruct((m, d), out_dtype),
        compiler_params=_params(("parallel",), vm),
        name="rmsnorm",
    )(x, gain.reshape(1, d))


def _mm_body(a_ref, b_ref, o_ref):
    o_ref[...] = _dot(a_ref[...], b_ref[...]).astype(o_ref.dtype)


def _mm_res_body(a_ref, b_ref, r_ref, o_ref):
    o_ref[...] = (r_ref[...] + _dot(a_ref[...], b_ref[...])).astype(o_ref.dtype)


def matmul(a, b, out_dtype, tm, tn, residual=None, name="matmul"):
    m, k = a.shape
    n = b.shape[1]
    in_specs = [pl.BlockSpec((tm, k), lambda i, j: (i, 0)),
                pl.BlockSpec((k, tn), lambda i, j: (0, j))]
    args = [a, b]
    body = _mm_body
    vm = 2 * (_nbytes((tm, k), a.dtype) + _nbytes((k, tn), b.dtype) + _nbytes((tm, tn), out_dtype))
    vm += 2 * _nbytes((tm, tn), F32)
    if residual is not None:
        in_specs.append(pl.BlockSpec((tm, tn), lambda i, j: (i, j)))
        args.append(residual)
        body = _mm_res_body
        vm += 2 * _nbytes((tm, tn), residual.dtype)
    return pl.pallas_call(
        body,
        grid=(m // tm, n // tn),
        in_specs=in_specs,
        out_specs=pl.BlockSpec((tm, tn), lambda i, j: (i, j)),
        out_shape=jax.ShapeDtypeStruct((m, n), out_dtype),
        compiler_params=_params(("parallel", "parallel"), vm),
        name=name,
    )(*args)


def _dn_chunk_body(xc_ref, xp_ref, sm_ref, cw_ref, alog_ref, dtb_ref,
                   qd_ref, kd_ref, w_ref, u_ref, att_ref, gl_ref, ext_ref):
    c = pl.program_id(0)
    C = DN_CHUNK
    halo = SUBLANES
    ext_ref[0:halo, :] = jnp.where(c > 0, xp_ref[...], 0.0)
    ext_ref[halo:halo + C, :] = xc_ref[...]
    cw = cw_ref[...]
    y = cw[0:1, :] * ext_ref[pl.ds(halo - CONV_WIDTH + 1, C), :]
    for j in range(1, CONV_WIDTH):
        y = y + cw[j:j + 1, :] * ext_ref[pl.ds(halo - CONV_WIDTH + 1 + j, C), :]
    y = y * _sigmoid(y)

    sm = sm_ref[...]
    xa = sm + dtb_ref[...]
    softplus = jnp.maximum(xa, 0.0) + jnp.log1p(jnp.exp(-jnp.abs(xa)))
    g = -jnp.exp(alog_ref[...]) * softplus
    beta = _sigmoid(sm)

    row = lax.broadcasted_iota(I32, (C, LANES), 0)
    gc = g
    d = 1
    while d < C:
        gc = gc + jnp.where(row >= d, pltpu.roll(gc, d, axis=0), 0.0)
        d *= 2
    gct = jnp.concatenate([gc, jnp.zeros_like(gc)], axis=0).T
    ex = jnp.exp(gc)
    gc_last = gc[C - 1:C, :]
    exl = jnp.exp(gc_last - gc)
    gl_ref[...] = jnp.exp(jnp.broadcast_to(gct[SM_A:SM_A + DN_HEADS, C - 1:C], (DN_HEADS, LANES)))

    ri = lax.broadcasted_iota(I32, (C, C), 0)
    ci = lax.broadcasted_iota(I32, (C, C), 1)
    tril = ri >= ci
    strict = ri > ci
    eye = jnp.where(ri == ci, 1.0, 0.0).astype(F32)
    lvl_masks = []
    lb = 0
    while (1 << lb) < C:
        lvl_masks.append(((ri >> (lb + 1)) == (ci >> (lb + 1)))
                         & (((ri >> lb) & 1) == 1) & (((ci >> lb) & 1) == 0))
        lb += 1

    for h in range(DN_HEADS):
        sl = slice(h * DN_HEAD_DIM, (h + 1) * DN_HEAD_DIM)
        qh = y[:, h * DN_HEAD_DIM:(h + 1) * DN_HEAD_DIM]
        kh = y[:, DN_WIDTH + h * DN_HEAD_DIM:DN_WIDTH + (h + 1) * DN_HEAD_DIM]
        vh = y[:, 2 * DN_WIDTH + h * DN_HEAD_DIM:2 * DN_WIDTH + (h + 1) * DN_HEAD_DIM]
        qn = qh * lax.rsqrt(jnp.sum(qh * qh, axis=-1, keepdims=True) + L2_EPS) * (DN_HEAD_DIM ** -0.5)
        kn = kh * lax.rsqrt(jnp.sum(kh * kh, axis=-1, keepdims=True) + L2_EPS)
        bcol = beta[:, SM_B + h:SM_B + h + 1]
        gcol = gc[:, SM_A + h:SM_A + h + 1]
        grow = gct[SM_A + h:SM_A + h + 1, 0:C]
        dec = jnp.exp(jnp.where(tril, gcol - grow, -jnp.inf))
        kb = kn * bcol
        knb = kn.astype(BF16)
        a_mat = jnp.where(strict, _dot_nt(kb.astype(BF16), knb) * dec, 0.0)
        x_inv = eye - jnp.where(lvl_masks[0], a_mat, 0.0)
        for lm in lvl_masks[1:]:
            x_inv = x_inv - _dot_hi(_dot_hi(x_inv, jnp.where(lm, a_mat, 0.0)), x_inv)
        excol = ex[:, SM_A + h:SM_A + h + 1]
        w_ref[:, sl] = _dot_hi(x_inv, kb * excol).astype(w_ref.dtype)
        u_ref[:, sl] = _dot_hi(x_inv, vh * bcol)
        att = jnp.where(tril, _dot_nt(qn.astype(BF16), knb) * dec, 0.0)
        att_ref[:, h * C:(h + 1) * C] = att.astype(att_ref.dtype)
        qd_ref[:, sl] = (qn * excol).astype(qd_ref.dtype)
        kd_ref[:, sl] = (kn * exl[:, SM_A + h:SM_A + h + 1]).astype(kd_ref.dtype)


def dn_chunk(proj, small, conv_w, alog_pad, dtb_pad):
    s = proj.shape[0]
    C = DN_CHUNK
    n_chunks = s // C
    w3 = 3 * DN_WIDTH
    row_spec = lambda width, dt: pl.BlockSpec((C, width), lambda c: (c, 0))
    vm = 2 * (_nbytes((C, w3), F32) + _nbytes((SUBLANES, w3), F32)) + 8 * _nbytes((C, w3), F32)
    return pl.pallas_call(
        _dn_chunk_body,
        grid=(n_chunks,),
        in_specs=[pl.BlockSpec((C, w3), lambda c: (c, COL_QKV // w3)),
                  pl.BlockSpec((SUBLANES, w3), lambda c: (jnp.maximum(c * (C // SUBLANES) - 1, 0), COL_QKV // w3)),
                  pl.BlockSpec((C, LANES), lambda c: (c, 0)),
                  pl.BlockSpec((CONV_WIDTH, w3), lambda c: (0, 0)),
                  pl.BlockSpec((1, LANES), lambda c: (0, 0)),
                  pl.BlockSpec((1, LANES), lambda c: (0, 0))],
        out_specs=[row_spec(DN_WIDTH, BF16), row_spec(DN_WIDTH, BF16), row_spec(DN_WIDTH, BF16),
                   row_spec(DN_WIDTH, F32), row_spec(DN_HEADS * C, BF16),
                   pl.BlockSpec((DN_HEADS, LANES), lambda c: (c, 0))],
        out_shape=[jax.ShapeDtypeStruct((s, DN_WIDTH), BF16),
                   jax.ShapeDtypeStruct((s, DN_WIDTH), BF16),
                   jax.ShapeDtypeStruct((s, DN_WIDTH), BF16),
                   jax.ShapeDtypeStruct((s, DN_WIDTH), F32),
                   jax.ShapeDtypeStruct((s, DN_HEADS * C), BF16),
                   jax.ShapeDtypeStruct((n_chunks * DN_HEADS, LANES), F32)],
        scratch_shapes=[pltpu.VMEM((SUBLANES + C, w3), F32)],
        compiler_params=_params(("parallel",), vm),
        name="dn_chunk",
    )(proj, proj, small, conv_w, alog_pad, dtb_pad)


def _dn_scan_body(qd_ref, kd_ref, w_ref, u_ref, att_ref, gl_ref, z_ref, nrm_ref, o_ref, st_ref):
    c = pl.program_id(0)
    C = DN_CHUNK

    @pl.when(c == 0)
    def _():
        st_ref[...] = jnp.zeros_like(st_ref)

    for h in range(DN_HEADS):
        sl = slice(h * DN_HEAD_DIM, (h + 1) * DN_HEAD_DIM)
        state = st_ref[h]
        sb = state.astype(BF16)
        v_new = u_ref[:, sl] - _dot(w_ref[:, sl], sb)
        vb = v_new.astype(BF16)
        o = _dot(qd_ref[:, sl], sb) + _dot(att_ref[:, h * C:(h + 1) * C], vb)
        st_ref[h] = state * gl_ref[h:h + 1, :] + _dot_tn(kd_ref[:, sl], vb)
        ms = jnp.mean(o * o, axis=-1, keepdims=True)
        z = z_ref[:, sl]
        o_ref[:, sl] = (o * lax.rsqrt(ms + RMS_EPS) * nrm_ref[...] * (z * _sigmoid(z))).astype(o_ref.dtype)


def dn_scan(qd, kd, w, u, att, gl, proj, dn_norm):
    s = qd.shape[0]
    C = DN_CHUNK
    row = lambda width: pl.BlockSpec((C, width), lambda c: (c, 0))
    vm = 2 * 6 * _nbytes((C, DN_WIDTH), F32) + 2 * _nbytes((DN_HEADS, DN_HEAD_DIM, DN_HEAD_DIM), F32)
    return pl.pallas_call(
        _dn_scan_body,
        grid=(s // C,),
        in_specs=[row(DN_WIDTH), row(DN_WIDTH), row(DN_WIDTH), row(DN_WIDTH), row(DN_HEADS * C),
                  pl.BlockSpec((DN_HEADS, LANES), lambda c: (c, 0)),
                  pl.BlockSpec((C, DN_WIDTH), lambda c: (c, COL_Z // DN_WIDTH)),
                  pl.BlockSpec((1, DN_HEAD_DIM), lambda c: (0, 0))],
        out_specs=row(DN_WIDTH),
        out_shape=jax.ShapeDtypeStruct((s, DN_WIDTH), BF16),
        scratch_shapes=[pltpu.VMEM((DN_HEADS, DN_HEAD_DIM, DN_HEAD_DIM), F32)],
        compiler_params=_params(("arbitrary",), vm),
        name="dn_scan",
    )(qd, kd, w, u, att, gl, proj, dn_norm.reshape(1, DN_HEAD_DIM))


def _rope_sa_body(q_ref, k_ref, v_ref, c_ref, s_ref, qo_ref, ko_ref, vo_ref):
    cs = c_ref[...]
    sn = s_ref[...]
    scale = SA_HEAD_DIM ** -0.5
    for h in range(SA_HEADS):
        sl = slice(h * SA_HEAD_DIM, (h + 1) * SA_HEAD_DIM)
        x = q_ref[:, sl]
        qo_ref[:, sl] = ((x * cs + pltpu.roll(x, SA_HEAD_DIM // 2, axis=1) * sn) * scale).astype(qo_ref.dtype)
        x = k_ref[:, sl]
        ko_ref[:, sl] = (x * cs + pltpu.roll(x, SA_HEAD_DIM // 2, axis=1) * sn).astype(ko_ref.dtype)
    vo_ref[...] = v_ref[...].astype(vo_ref.dtype)


def rope_sa(proj, cos_t, sin_t, tm=512):
    s = proj.shape[0]
    col = lambda off: pl.BlockSpec((tm, SA_WIDTH), lambda i: (i, off // SA_WIDTH))
    tab = pl.BlockSpec((tm, LANES), lambda i: (i, 0))
    out = pl.BlockSpec((tm, SA_WIDTH), lambda i: (i, 0))
    vm = 2 * 3 * (_nbytes((tm, SA_WIDTH), F32) + _nbytes((tm, SA_WIDTH), BF16)) + 4 * _nbytes((tm, SA_WIDTH), F32)
    return pl.pallas_call(
        _rope_sa_body,
        grid=(s // tm,),
        in_specs=[col(COL_QSA), col(COL_KSA), col(COL_VSA), tab, tab],
        out_specs=[out, out, out],
        out_shape=[jax.ShapeDtypeStruct((s, SA_WIDTH), BF16)] * 3,
        compiler_params=_params(("parallel",), vm),
        name="rope_sa",
    )(proj, proj, proj, cos_t, sin_t)


def _rope_ix_body(q_ref, sm_ref, c_ref, s_ref, qo_ref, klo_ref, khi_ref):
    cs = c_ref[...]
    sn = s_ref[...]
    lane = lax.broadcasted_iota(I32, cs.shape, 1)
    first = (lane & (IDX_HEAD_DIM - 1)) < IDX_HEAD_DIM // 2
    half = IDX_HEAD_DIM // 2

    def rot(x):
        swapped = jnp.where(first, pltpu.roll(x, LANES - half, axis=1), pltpu.roll(x, half, axis=1))
        return x * cs + swapped * sn

    for j in range(IDX_HEADS * IDX_HEAD_DIM // LANES):
        sl = slice(j * LANES, (j + 1) * LANES)
        qo_ref[:, sl] = rot(q_ref[:, sl]).astype(qo_ref.dtype)
    k_lo = jnp.where(lane < IDX_HEAD_DIM, rot(sm_ref[...]), 0.0)
    klo_ref[...] = k_lo.astype(klo_ref.dtype)
    khi_ref[...] = pltpu.roll(k_lo, IDX_HEAD_DIM, axis=1).astype(khi_ref.dtype)


def rope_ix(proj, small, cos_t, sin_t, tm=512):
    s = proj.shape[0]
    wq = IDX_HEADS * IDX_HEAD_DIM
    tab = pl.BlockSpec((tm, LANES), lambda i: (i, 0))
    vm = 2 * (_nbytes((tm, wq), F32) + _nbytes((tm, wq), BF16)) + 4 * _nbytes((tm, wq), F32)
    return pl.pallas_call(
        _rope_ix_body,
        grid=(s // tm,),
        in_specs=[pl.BlockSpec((tm, wq), lambda i: (i, COL_QIX // wq)), tab, tab, tab],
        out_specs=[pl.BlockSpec((tm, wq), lambda i: (i, 0)), tab, tab],
        out_shape=[jax.ShapeDtypeStruct((s, wq), BF16),
                   jax.ShapeDtypeStruct((s, LANES), BF16),
                   jax.ShapeDtypeStruct((s, LANES), BF16)],
        compiler_params=_params(("parallel",), vm),
        name="rope_ix",
    )(proj, small, cos_t, sin_t)


def _index_body(q_ref, klo_ref, khi_ref, sm_ref, mask_ref, keys_ref, *, tq, tk, nkt, topk):
    i = pl.program_id(0)
    nk = ((i + 1) * tq + tk - 1) // tk
    wv = sm_ref[...] * (IDX_HEADS ** -0.5 * IDX_HEAD_DIM ** -0.5)
    row_g = i * tq + lax.broadcasted_iota(I32, (tq, tk), 0)
    col_l = lax.broadcasted_iota(I32, (tq, tk), 1)

    def score_tile(kt, carry):
        off = pl.multiple_of(kt * tk, tk)
        k_lo = klo_ref[pl.ds(off, tk), :]
        k_hi = khi_ref[pl.ds(off, tk), :]
        acc = jnp.zeros((tq, tk), F32)
        for j in range(IDX_HEADS // 2):
            qp = q_ref[:, j * LANES:(j + 1) * LANES]
            w0 = wv[:, SM_WIX + 2 * j:SM_WIX + 2 * j + 1]
            w1 = wv[:, SM_WIX + 2 * j + 1:SM_WIX + 2 * j + 2]
            acc = acc + w0 * jnp.maximum(_dot_nt(qp, k_lo), 0.0)
            acc = acc + w1 * jnp.maximum(_dot_nt(qp, k_hi), 0.0)
        sc = jnp.where(kt * tk + col_l <= row_g, acc, -jnp.inf)
        bits = pltpu.bitcast(sc, I32)
        keys_ref[kt] = bits ^ ((bits >> 31) & 0x7FFFFFFF)
        return carry

    lax.fori_loop(0, nk, score_tile, 0)

    def count_ge(cand):
        cb = jnp.broadcast_to(cand, (tq, LANES))

        def body(kt, cnt):
            t = keys_ref[kt]
            for sidx in range(tk // LANES):
                cnt = cnt + jnp.where(t[:, sidx * LANES:(sidx + 1) * LANES] >= cb, 1.0, 0.0)
            return cnt

        cnt = lax.fori_loop(0, nk, body, jnp.zeros((tq, LANES), F32))
        return jnp.sum(cnt, axis=1, keepdims=True)

    kf = float(topk)
    ans = jnp.where(count_ge(jnp.zeros((tq, 1), I32)) >= kf, 0, INT_MIN).astype(I32)

    def bit_body(b, ans):
        cand = ans + lax.shift_left(jnp.int32(1), 30 - b)
        return jnp.where(count_ge(cand) >= kf, cand, ans)

    ans = lax.fori_loop(0, 31, bit_body, ans)

    def write(kt, carry):
        sel = (keys_ref[kt] >= ans) & (kt * tk + col_l <= row_g)
        mask_ref[kt] = jnp.where(sel, 1.0, 0.0).astype(mask_ref.dtype)
        return carry

    lax.fori_loop(0, nk, write, 0)

    def clear(kt, carry):
        mask_ref[kt] = jnp.zeros((tq, tk), mask_ref.dtype)
        return carry

    lax.fori_loop(nk, nkt, clear, 0)


def index_mask(q_ix, k_lo, k_hi, small, topk, tq=256, tk=512):
    s = q_ix.shape[0]
    nkt = s // tk
    wq = IDX_HEADS * IDX_HEAD_DIM
    vm = (2 * (_nbytes((tq, wq), BF16) + 2 * _nbytes((s, LANES), BF16) + _nbytes((tq, LANES), F32)
               + _nbytes((nkt, tq, tk), BF16)) + _nbytes((nkt, tq, tk), I32) + 8 * _nbytes((tq, tk), F32))
    return pl.pallas_call(
        functools.partial(_index_body, tq=tq, tk=tk, nkt=nkt, topk=topk),
        grid=(s // tq,),
        in_specs=[pl.BlockSpec((tq, wq), lambda i: (i, 0)),
                  pl.BlockSpec((s, LANES), lambda i: (0, 0)),
                  pl.BlockSpec((s, LANES), lambda i: (0, 0)),
                  pl.BlockSpec((tq, LANES), lambda i: (i, 0))],
        out_specs=pl.BlockSpec((nkt, tq, tk), lambda i: (0, i, 0)),
        out_shape=jax.ShapeDtypeStruct((nkt, s, tk), BF16),
        scratch_shapes=[pltpu.VMEM((nkt, tq, tk), I32)],
        compiler_params=_params(("parallel",), vm),
        name="index_mask",
    )(q_ix, k_lo, k_hi, small)


def _attn_body(qi_ref, ki_ref, q_ref, k_ref, v_ref, mk_ref, o_ref, m_ref, l_ref, acc_ref):
    p = pl.program_id(0)
    qi = qi_ref[p]
    ki = ki_ref[p]

    @pl.when(ki == 0)
    def _():
        m_ref[...] = jnp.full_like(m_ref, NEG_BIG)
        l_ref[...] = jnp.zeros_like(l_ref)
        acc_ref[...] = jnp.zeros_like(acc_ref)

    keep = mk_ref[0].astype(F32) > 0.5
    for h in range(SA_HEADS):
        sl = slice(h * SA_HEAD_DIM, (h + 1) * SA_HEAD_DIM)
        s = jnp.where(keep, _dot_nt(q_ref[:, sl], k_ref[:, sl]), NEG_BIG)
        m_old = m_ref[h][:, 0:1]
        m_new = jnp.maximum(m_old, jnp.max(s, axis=1, keepdims=True))
        alpha = jnp.exp(m_old - m_new)
        pr = jnp.where(keep, jnp.exp(s - m_new), 0.0)
        l_ref[h] = jnp.broadcast_to(alpha * l_ref[h][:, 0:1] + jnp.sum(pr, axis=1, keepdims=True), l_ref.shape[1:])
        m_ref[h] = jnp.broadcast_to(m_new, m_ref.shape[1:])
        acc_ref[:, sl] = alpha * acc_ref[:, sl] + _dot(pr.astype(BF16), v_ref[:, sl])

    @pl.when(ki == qi)
    def _():
        for h in range(SA_HEADS):
            sl = slice(h * SA_HEAD_DIM, (h + 1) * SA_HEAD_DIM)
            o_ref[:, sl] = (acc_ref[:, sl] / l_ref[h][:, 0:1]).astype(o_ref.dtype)


def masked_attention(q, k, v, mask, t=512):
    s = q.shape[0]
    nb = s // t
    pairs = [(a, b) for a in range(nb) for b in range(a + 1)]
    qi = jnp.asarray([a for a, _ in pairs], I32)
    ki = jnp.asarray([b for _, b in pairs], I32)
    vm = (2 * (4 * _nbytes((t, SA_WIDTH), BF16) + _nbytes((t, t), BF16)) + _nbytes((t, SA_WIDTH), F32)
          + 2 * _nbytes((SA_HEADS, t, LANES), F32) + 8 * _nbytes((t, t), F32))
    grid_spec = pltpu.PrefetchScalarGridSpec(
        num_scalar_prefetch=2,
        grid=(len(pairs),),
        in_specs=[pl.BlockSpec((t, SA_WIDTH), lambda p, qi, ki: (qi[p], 0)),
                  pl.BlockSpec((t, SA_WIDTH), lambda p, qi, ki: (ki[p], 0)),
                  pl.BlockSpec((t, SA_WIDTH), lambda p, qi, ki: (ki[p], 0)),
                  pl.BlockSpec((1, t, t), lambda p, qi, ki: (ki[p], qi[p], 0))],
        out_specs=pl.BlockSpec((t, SA_WIDTH), lambda p, qi, ki: (qi[p], 0)),
        scratch_shapes=[pltpu.VMEM((SA_HEADS, t, LANES), F32),
                        pltpu.VMEM((SA_HEADS, t, LANES), F32),
                        pltpu.VMEM((t, SA_WIDTH), F32)],
    )
    return pl.pallas_call(
        _attn_body,
        grid_spec=grid_spec,
        out_shape=jax.ShapeDtypeStruct((s, SA_WIDTH), BF16),
        compiler_params=_params(("arbitrary",), vm),
        name="masked_attention",
    )(qi, ki, q, k, v, mask)


def _merge_body(odn_ref, osa_ref, wdn_ref, wsa_ref, gdn_ref, gsa_ref, o_ref):
    y_dn = _dot(odn_ref[...], wdn_ref[...])
    y_sa = _dot(osa_ref[...], wsa_ref[...])
    o_ref[...] = (_sigmoid(gdn_ref[...]) * y_dn + _sigmoid(gsa_ref[...]) * y_sa).astype(o_ref.dtype)


def merge_branches(o_dn, o_sa, w_dn, w_sa, proj, tm=512, tn=512):
    s, kd = o_dn.shape
    d = w_dn.shape[1]
    vm = 2 * (2 * _nbytes((tm, kd), BF16) + 2 * _nbytes((kd, tn), BF16) + 2 * _nbytes((tm, tn), F32)
              + _nbytes((tm, tn), BF16)) + 6 * _nbytes((tm, tn), F32)
    return pl.pallas_call(
        _merge_body,
        grid=(s // tm, d // tn),
        in_specs=[pl.BlockSpec((tm, kd), lambda i, j: (i, 0)),
                  pl.BlockSpec((tm, kd), lambda i, j: (i, 0)),
                  pl.BlockSpec((kd, tn), lambda i, j: (0, j)),
                  pl.BlockSpec((kd, tn), lambda i, j: (0, j)),
                  pl.BlockSpec((tm, tn), lambda i, j: (i, COL_GDN // tn + j)),
                  pl.BlockSpec((tm, tn), lambda i, j: (i, COL_GSA // tn + j))],
        out_specs=pl.BlockSpec((tm, tn), lambda i, j: (i, j)),
        out_shape=jax.ShapeDtypeStruct((s, d), BF16),
        compiler_params=_params(("parallel", "parallel"), vm),
        name="merge_branches",
    )(o_dn, o_sa, w_dn, w_sa, proj, proj)


def _ffn_body(h_ref, x_ref, wg_ref, wu_ref, wd_ref, o_ref, acc_ref):
    f = pl.program_id(1)

    @pl.when(f == 0)
    def _():
        acc_ref[...] = jnp.zeros_like(acc_ref)

    h = h_ref[...]
    g = _dot(h, wg_ref[...])
    u = _dot(h, wu_ref[...])
    acc_ref[...] += _dot((g * _sigmoid(g) * u).astype(BF16), wd_ref[...])

    @pl.when(f == pl.num_programs(1) - 1)
    def _():
        o_ref[...] = x_ref[...] + acc_ref[...]


def ffn_dense(h, x, w_gate, w_up, w_down, tm=512, tf=512):
    s, d = h.shape
    ff = w_gate.shape[1]
    vm = (2 * (_nbytes((tm, d), BF16) + 2 * _nbytes((tm, d), F32) + 3 * _nbytes((d, tf), BF16))
          + _nbytes((tm, d), F32) + 6 * _nbytes((tm, tf), F32))
    return pl.pallas_call(
        _ffn_body,
        grid=(s // tm, ff // tf),
        in_specs=[pl.BlockSpec((tm, d), lambda i, f: (i, 0)),
                  pl.BlockSpec((tm, d), lambda i, f: (i, 0)),
                  pl.BlockSpec((d, tf), lambda i, f: (0, f)),
                  pl.BlockSpec((d, tf), lambda i, f: (0, f)),
                  pl.BlockSpec((tf, d), lambda i, f: (f, 0))],
        out_specs=pl.BlockSpec((tm, d), lambda i, f: (i, 0)),
        out_shape=jax.ShapeDtypeStruct((s, d), F32),
        scratch_shapes=[pltpu.VMEM((tm, d), F32)],
        compiler_params=_params(("parallel", "arbitrary"), vm),
        name="ffn_dense",
    )(h, x, w_gate, w_up, w_down)


def _router_body(h_ref, wr_ref, o_ref):
    logits = _dot(h_ref[...], wr_ref[...])
    lane = lax.broadcasted_iota(I32, logits.shape, 1)
    lg = jnp.where(lane < N_EXPERTS, logits, -jnp.inf)
    m1 = jnp.max(lg, axis=1, keepdims=True)
    i1 = jnp.min(jnp.where(lg == m1, lane, LANES), axis=1, keepdims=True)
    lg2 = jnp.where(lane == i1, -jnp.inf, lg)
    m2 = jnp.max(lg2, axis=1, keepdims=True)
    i2 = jnp.min(jnp.where(lg2 == m2, lane, LANES), axis=1, keepdims=True)
    e = jnp.exp(m2 - m1)
    o_ref[...] = jnp.where(lane == i1, 1.0 / (1.0 + e), 0.0) + jnp.where(lane == i2, e / (1.0 + e), 0.0)


def route_tokens(h, w_router_pad, tm=512):
    s, d = h.shape
    vm = 2 * (_nbytes((tm, d), BF16) + _nbytes((d, LANES), BF16) + _nbytes((tm, LANES), F32)) + 8 * _nbytes((tm, LANES), F32)
    return pl.pallas_call(
        _router_body,
        grid=(s // tm,),
        in_specs=[pl.BlockSpec((tm, d), lambda i: (i, 0)),
                  pl.BlockSpec((d, LANES), lambda i: (0, 0))],
        out_specs=pl.BlockSpec((tm, LANES), lambda i: (i, 0)),
        out_shape=jax.ShapeDtypeStruct((s, LANES), F32),
        compiler_params=_params(("parallel",), vm),
        name="moe_router",
    )(h, w_router_pad)


def _moe_body(h_ref, x_ref, cb_ref, wg_ref, wu_ref, wd_ref, o_ref, acc_ref):
    e = pl.program_id(1)
    f = pl.program_id(2)

    @pl.when((e == 0) & (f == 0))
    def _():
        acc_ref[...] = jnp.zeros_like(acc_ref)

    cb = cb_ref[...]
    lane = lax.broadcasted_iota(I32, cb.shape, 1)
    ce = jnp.sum(jnp.where(lane == e, cb, 0.0), axis=1, keepdims=True)
    h = h_ref[...]
    g = _dot(h, wg_ref[0])
    u = _dot(h, wu_ref[0])
    acc_ref[...] += _dot((g * _sigmoid(g) * u * ce).astype(BF16), wd_ref[0])

    @pl.when((e == pl.num_programs(1) - 1) & (f == pl.num_programs(2) - 1))
    def _():
        o_ref[...] = x_ref[...] + acc_ref[...]


def moe_experts(h, x, combine, w_gate, w_up, w_down, tm=512, tf=256):
    s, d = h.shape
    ne, _, ff = w_gate.shape
    vm = (2 * (_nbytes((tm, d), BF16) + 2 * _nbytes((tm, d), F32) + _nbytes((tm, LANES), F32)
               + 3 * _nbytes((d, tf), BF16)) + _nbytes((tm, d), F32) + 6 * _nbytes((tm, tf), F32))
    return pl.pallas_call(
        _moe_body,
        grid=(s // tm, ne, ff // tf),
        in_specs=[pl.BlockSpec((tm, d), lambda i, e, f: (i, 0)),
                  pl.BlockSpec((tm, d), lambda i, e, f: (i, 0)),
                  pl.BlockSpec((tm, LANES), lambda i, e, f: (i, 0)),
                  pl.BlockSpec((1, d, tf), lambda i, e, f: (e, 0, f)),
                  pl.BlockSpec((1, d, tf), lambda i, e, f: (e, 0, f)),
                  pl.BlockSpec((1, tf, d), lambda i, e, f: (e, f, 0))],
        out_specs=pl.BlockSpec((tm, d), lambda i, e, f: (i, 0)),
        out_shape=jax.ShapeDtypeStruct((s, d), F32),
        scratch_shapes=[pltpu.VMEM((tm, d), F32)],
        compiler_params=_params(("parallel", "arbitrary", "arbitrary"), vm),
        name="moe_experts",
    )(h, x, combine, w_gate, w_up, w_down)


def _rope_tables(positions, dim, reps):
    inv_freq = ROPE_THETA ** (-jnp.arange(0, dim, 2, dtype=F32) / dim)
    ang = positions.astype(F32)[:, None] * inv_freq
    cos, sin = jnp.cos(ang), jnp.sin(ang)
    return jnp.tile(cos, (1, 2 * reps)), jnp.tile(jnp.concatenate([-sin, sin], axis=1), (1, reps))


def _split_w_in(w):
    d = w.shape[0]
    o_a = 4 * DN_WIDTH
    o_qsa = o_a + 2 * DN_HEADS
    o_kix = o_qsa + 3 * SA_WIDTH + IDX_HEADS * IDX_HEAD_DIM
    o_wix = o_kix + IDX_HEAD_DIM
    o_g = o_wix + IDX_HEADS
    main = jnp.concatenate([w[:, :o_a], w[:, o_qsa:o_kix], w[:, o_g:]], axis=1)
    small = jnp.concatenate([w[:, o_kix:o_wix], w[:, o_a:o_qsa], w[:, o_wix:o_g],
                             jnp.zeros((d, LANES - IDX_HEAD_DIM - 2 * DN_HEADS - IDX_HEADS), w.dtype)], axis=1)
    return main.astype(BF16), small.astype(BF16)


def _pad_lanes(v, offset):
    return jnp.zeros((1, LANES), F32).at[0, offset:offset + v.shape[0]].set(v.astype(F32))


def _mixer(x, cos_sa, sin_sa, cos_ix, sin_ix, norm_gain, w_in, conv_w, a_log, dt_bias, dn_norm,
           w_dn_out, w_sa_out, w_o):
    s = x.shape[0]
    w_main, w_small = _split_w_in(w_in)
    h = rmsnorm(x, norm_gain, BF16)
    proj = matmul(h, w_main, F32, tm=min(1024, s), tn=512, name="in_proj")
    small = matmul(h, w_small, F32, tm=min(1024, s), tn=LANES, name="in_proj_small")

    qd, kd, w, u, att, gl = dn_chunk(proj, small, conv_w.astype(F32), _pad_lanes(a_log, SM_A), _pad_lanes(dt_bias, SM_A))
    o_dn = dn_scan(qd, kd, w, u, att, gl, proj, dn_norm.astype(F32))

    q_sa, k_sa, v_sa = rope_sa(proj, cos_sa, sin_sa)
    q_ix, k_lo, k_hi = rope_ix(proj, small, cos_ix, sin_ix)
    t_attn = min(512, s)
    mask = index_mask(q_ix, k_lo, k_hi, small, min(TOPK_MAX, s // 4), tq=min(256, s), tk=t_attn)
    o_sa = masked_attention(q_sa, k_sa, v_sa, mask, t=t_attn)

    merged = merge_branches(o_dn, o_sa, w_dn_out.astype(BF16), w_sa_out.astype(BF16), proj)
    return matmul(merged, w_o.astype(BF16), F32, tm=512, tn=512, residual=x, name="out_proj")


def kernel(x, positions, norm_mix, w_in, conv_w, a_log, dt_bias, dn_norm, w_dn_out, w_sa_out, w_o, norm_ffn, dense_w_gate, dense_w_up, dense_w_down, moe_router, moe_w_gate, moe_w_up, moe_w_down, final_norm):
    b, s, d = x.shape
    depth = norm_mix.shape[0]
    outs = []
    for bi in range(b):
        xb = x[bi]
        pos = positions[bi]
        cos_sa, sin_sa = _rope_tables(pos, SA_HEAD_DIM, 1)
        cos_ix, sin_ix = _rope_tables(pos, IDX_HEAD_DIM, 2)
        for layer in range(depth):
            xb = _mixer(xb, cos_sa, sin_sa, cos_ix, sin_ix, norm_mix[layer], w_in[layer], conv_w[layer],
                        a_log[layer], dt_bias[layer], dn_norm[layer], w_dn_out[layer], w_sa_out[layer], w_o[layer])
            h = rmsnorm(xb, norm_ffn[layer], BF16)
            j = layer // 2
            if layer % 2 == 0:
                xb = ffn_dense(h, xb, dense_w_gate[j].astype(BF16), dense_w_up[j].astype(BF16),
                               dense_w_down[j].astype(BF16))
            else:
                wr = jnp.zeros((d, LANES), BF16).at[:, :N_EXPERTS].set(moe_router[j].astype(BF16))
                combine = route_tokens(h, wr)
                xb = moe_experts(h, xb, combine, moe_w_gate[j].astype(BF16), moe_w_up[j].astype(BF16),
                                 moe_w_down[j].astype(BF16))
        outs.append(rmsnorm(xb, final_norm, x.dtype))
    return jnp.stack(outs, axis=0)
```

```python
import functools

import jax
import jax.numpy as jnp
from jax import lax
from jax.experimental import pallas as pl
from jax.experimental.pallas import tpu as pltpu

F32 = jnp.float32
BF16 = jnp.bfloat16
I32 = jnp.int32

RMS_EPS = 1e-6
L2_EPS = 1e-6
DN_HEADS = 8
DN_HEAD_DIM = 128
DN_WIDTH = DN_HEADS * DN_HEAD_DIM
CONV_WIDTH = 4
DN_CHUNK = 64
SA_HEADS = 8
SA_HEAD_DIM = 128
SA_WIDTH = SA_HEADS * SA_HEAD_DIM
IDX_HEADS = 16
IDX_HEAD_DIM = 64
TOPK_MAX = 256
ROPE_THETA = 10000.0
N_EXPERTS = 8

LANES = 128
SUBLANES = 8
VMEM_CAP_BYTES = 56 * 2**20
NEG_BIG = -1e30
INT_MIN = -2**31
LOG2_E = 1.4426950408889634

COL_QKV = 0
COL_Z = 3072
COL_QSA = 4096
COL_KSA = 5120
COL_VSA = 6144
COL_QIX = 7168
COL_GDN = 8192
COL_GSA = 10240
MAIN_WIDTH = 12288
SM_KIX = 0
SM_A = 64
SM_B = 72
SM_WIX = 80


def _params(semantics, vmem_bytes):
    return pltpu.CompilerParams(dimension_semantics=semantics,
                                vmem_limit_bytes=int(min(max(vmem_bytes, 16 * 2**20), VMEM_CAP_BYTES)))


def _nbytes(shape, dtype):
    n = 1
    for s in shape:
        n *= s
    return n * jnp.dtype(dtype).itemsize


def _sigmoid(x):
    return 1.0 / (1.0 + jnp.exp(-x))


def _dot(a, b):
    return jnp.dot(a, b, preferred_element_type=F32)


def _dot_nt(a, b):
    return lax.dot_general(a, b, (((1,), (1,)), ((), ())), preferred_element_type=F32)


def _dot_tn(a, b):
    return lax.dot_general(a, b, (((0,), (0,)), ((), ())), preferred_element_type=F32)


def _split_bf16(x):
    hi = x.astype(BF16)
    return hi, (x - hi.astype(F32)).astype(BF16)


def _dot3(a, b):
    return _dot(a[0], b[0]) + _dot(a[1], b[0]) + _dot(a[0], b[1])


def _rmsnorm_body(x_ref, g_ref, o_ref):
    x = x_ref[...]
    ms = jnp.mean(x * x, axis=-1, keepdims=True)
    o_ref[...] = (x * lax.rsqrt(ms + RMS_EPS) * g_ref[...]).astype(o_ref.dtype)


def rmsnorm(x, gain, out_dtype, tm=512):
    m, d = x.shape
    vm = 2 * (_nbytes((tm, d), F32) + _nbytes((tm, d), out_dtype)) + 4 * _nbytes((tm, d), F32)
    return pl.pallas_call(
        _rmsnorm_body,
        grid=(m // tm,),
        in_specs=[pl.BlockSpec((tm, d), lambda i: (i, 0)),
                  pl.BlockSpec((1, d), lambda i: (0, 0))],
        out_specs=pl.BlockSpec((tm, d), lambda i: (i, 0)),
        out_shape=jax.ShapeDtypeStruct((m, d), out_dtype),
        compiler_params=_params(("parallel",), vm),
        name="rmsnorm",
    )(x, gain.reshape(1, d))


def _mm_body(a_ref, b_ref, o_ref):
    o_ref[...] = _dot(a_ref[...], b_ref[...]).astype(o_ref.dtype)


def _mm_res_body(a_ref, b_ref, r_ref, o_ref):
    o_ref[...] = (r_ref[...] + _dot(a_ref[...], b_ref[...])).astype(o_ref.dtype)


def matmul(a, b, out_dtype, tm, tn, residual=None, name="matmul"):
    m, k = a.shape
    n = b.shape[1]
    in_specs = [pl.BlockSpec((tm, k), lambda i, j: (i, 0)),
                pl.BlockSpec((k, tn), lambda i, j: (0, j))]
    args = [a, b]
    body = _mm_body
    vm = 2 * (_nbytes((tm, k), a.dtype) + _nbytes((k, tn), b.dtype) + _nbytes((tm, tn), out_dtype))
    vm += 2 * _nbytes((tm, tn), F32)
    if residual is not None:
        in_specs.append(pl.BlockSpec((tm, tn), lambda i, j: (i, j)))
        args.append(residual)
        body = _mm_res_body
        vm += 2 * _nbytes((tm, tn), residual.dtype)
    return pl.pallas_call(
        body,
        grid=(m // tm, n // tn),
        in_specs=in_specs,
        out_specs=pl.BlockSpec((tm, tn), lambda i, j: (i, j)),
        out_shape=jax.ShapeDtypeStruct((m, n), out_dtype),
        compiler_params=_params(("parallel", "parallel"), vm),
        name=name,
    )(*args)


def _dn_chunk_body(xc_ref, xp_ref, sm_ref, cw_ref, alog_ref, dtb_ref,
                   qd_ref, kd_ref, w_ref, u_ref, att_ref, gl_ref, ext_ref):
    c = pl.program_id(0)
    C = DN_CHUNK
    halo = SUBLANES
    ext_ref[0:halo, :] = jnp.where(c > 0, xp_ref[...], 0.0)
    ext_ref[halo:halo + C, :] = xc_ref[...]
    cw = cw_ref[...]
    y = cw[0:1, :] * ext_ref[pl.ds(halo - CONV_WIDTH + 1, C), :]
    for j in range(1, CONV_WIDTH):
        y = y + cw[j:j + 1, :] * ext_ref[pl.ds(halo - CONV_WIDTH + 1 + j, C), :]
    y = y * _sigmoid(y)

    sm = sm_ref[...]
    xa = sm + dtb_ref[...]
    softplus = jnp.maximum(xa, 0.0) + jnp.log1p(jnp.exp(-jnp.abs(xa)))
    g = -jnp.exp(alog_ref[...]) * softplus
    beta = _sigmoid(sm)

    row = lax.broadcasted_iota(I32, (C, LANES), 0)
    gc = g
    d = 1
    while d < C:
        gc = gc + jnp.where(row >= d, pltpu.roll(gc, d, axis=0), 0.0)
        d *= 2
    gct = jnp.concatenate([gc, jnp.zeros_like(gc)], axis=0).T
    ex = jnp.exp(gc)
    gc_last = gc[C - 1:C, :]
    exl = jnp.exp(gc_last - gc)
    gl_ref[...] = jnp.exp(jnp.broadcast_to(gct[SM_A:SM_A + DN_HEADS, C - 1:C], (DN_HEADS, LANES)))

    ri = lax.broadcasted_iota(I32, (C, C), 0)
    ci = lax.broadcasted_iota(I32, (C, C), 1)
    tril = ri >= ci
    strict = ri > ci
    eye = jnp.where(ri == ci, 1.0, 0.0).astype(F32)
    lvl_masks = []
    lb = 0
    while (1 << lb) < C:
        lvl_masks.append(((ri >> (lb + 1)) == (ci >> (lb + 1)))
                         & (((ri >> lb) & 1) == 1) & (((ci >> lb) & 1) == 0))
        lb += 1

    heads = range(DN_HEADS)
    a_mats, kbs, vbs, excols = [], [], [], []
    for h in heads:
        sl = slice(h * DN_HEAD_DIM, (h + 1) * DN_HEAD_DIM)
        qh = y[:, h * DN_HEAD_DIM:(h + 1) * DN_HEAD_DIM]
        kh = y[:, DN_WIDTH + h * DN_HEAD_DIM:DN_WIDTH + (h + 1) * DN_HEAD_DIM]
        vh = y[:, 2 * DN_WIDTH + h * DN_HEAD_DIM:2 * DN_WIDTH + (h + 1) * DN_HEAD_DIM]
        qn = qh * lax.rsqrt(jnp.sum(qh * qh, axis=-1, keepdims=True) + L2_EPS) * (DN_HEAD_DIM ** -0.5)
        kn = kh * lax.rsqrt(jnp.sum(kh * kh, axis=-1, keepdims=True) + L2_EPS)
        bcol = beta[:, SM_B + h:SM_B + h + 1]
        gcol = gc[:, SM_A + h:SM_A + h + 1]
        grow = gct[SM_A + h:SM_A + h + 1, 0:C]
        dec = jnp.exp(jnp.where(tril, gcol - grow, -jnp.inf))
        kb = kn * bcol
        knb = kn.astype(BF16)
        excol = ex[:, SM_A + h:SM_A + h + 1]
        a_mats.append(jnp.where(strict, _dot_nt(kb.astype(BF16), knb) * dec, 0.0))
        kbs.append(kb * excol)
        vbs.append(vh * bcol)
        att = jnp.where(tril, _dot_nt(qn.astype(BF16), knb) * dec, 0.0)
        att_ref[:, h * C:(h + 1) * C] = att.astype(att_ref.dtype)
        qd_ref[:, sl] = (qn * excol).astype(qd_ref.dtype)
        kd_ref[:, sl] = (kn * exl[:, SM_A + h:SM_A + h + 1]).astype(kd_ref.dtype)

    x_inv = [eye - jnp.where(lvl_masks[0], a, 0.0) for a in a_mats]
    for lm in lvl_masks[1:]:
        xs = [_split_bf16(x) for x in x_inv]
        ts = [_dot3(xs[h], _split_bf16(jnp.where(lm, a_mats[h], 0.0))) for h in heads]
        x_inv = [x_inv[h] - _dot3(_split_bf16(ts[h]), xs[h]) for h in heads]
    xs = [_split_bf16(x) for x in x_inv]
    for h in heads:
        sl = slice(h * DN_HEAD_DIM, (h + 1) * DN_HEAD_DIM)
        w_ref[:, sl] = _dot3(xs[h], _split_bf16(kbs[h])).astype(w_ref.dtype)
        u_ref[:, sl] = _dot3(xs[h], _split_bf16(vbs[h]))


def dn_chunk(proj, small, conv_w, alog_pad, dtb_pad):
    s = proj.shape[0]
    C = DN_CHUNK
    n_chunks = s // C
    w3 = 3 * DN_WIDTH
    row_spec = lambda width, dt: pl.BlockSpec((C, width), lambda c: (c, 0))
    vm = 2 * (_nbytes((C, w3), F32) + _nbytes((SUBLANES, w3), F32)) + 8 * _nbytes((C, w3), F32)
    return pl.pallas_call(
        _dn_chunk_body,
        grid=(n_chunks,),
        in_specs=[pl.BlockSpec((C, w3), lambda c: (c, COL_QKV // w3)),
                  pl.BlockSpec((SUBLANES, w3), lambda c: (jnp.maximum(c * (C // SUBLANES) - 1, 0), COL_QKV // w3)),
                  pl.BlockSpec((C, LANES), lambda c: (c, 0)),
                  pl.BlockSpec((CONV_WIDTH, w3), lambda c: (0, 0)),
                  pl.BlockSpec((1, LANES), lambda c: (0, 0)),
                  pl.BlockSpec((1, LANES), lambda c: (0, 0))],
        out_specs=[row_spec(DN_WIDTH, BF16), row_spec(DN_WIDTH, BF16), row_spec(DN_WIDTH, BF16),
                   row_spec(DN_WIDTH, F32), row_spec(DN_HEADS * C, BF16),
                   pl.BlockSpec((DN_HEADS, LANES), lambda c: (c, 0))],
        out_shape=[jax.ShapeDtypeStruct((s, DN_WIDTH), BF16),
                   jax.ShapeDtypeStruct((s, DN_WIDTH), BF16),
                   jax.ShapeDtypeStruct((s, DN_WIDTH), BF16),
                   jax.ShapeDtypeStruct((s, DN_WIDTH), F32),
                   jax.ShapeDtypeStruct((s, DN_HEADS * C), BF16),
                   jax.ShapeDtypeStruct((n_chunks * DN_HEADS, LANES), F32)],
        scratch_shapes=[pltpu.VMEM((SUBLANES + C, w3), F32)],
        compiler_params=_params(("parallel",), vm),
        name="dn_chunk",
    )(proj, proj, small, conv_w, alog_pad, dtb_pad)


def _dn_scan_body(qd_ref, kd_ref, w_ref, u_ref, att_ref, gl_ref, z_ref, nrm_ref, o_ref, st_ref):
    c = pl.program_id(0)
    C = DN_CHUNK

    @pl.when(c == 0)
    def _():
        st_ref[...] = jnp.zeros_like(st_ref)

    for h in range(DN_HEADS):
        sl = slice(h * DN_HEAD_DIM, (h + 1) * DN_HEAD_DIM)
        state = st_ref[h]
        sb = state.astype(BF16)
        v_new = u_ref[:, sl] - _dot(w_ref[:, sl], sb)
        vb = v_new.astype(BF16)
        o = _dot(qd_ref[:, sl], sb) + _dot(att_ref[:, h * C:(h + 1) * C], vb)
        st_ref[h] = state * gl_ref[h:h + 1, :] + _dot_tn(kd_ref[:, sl], vb)
        ms = jnp.mean(o * o, axis=-1, keepdims=True)
        z = z_ref[:, sl]
        o_ref[:, sl] = (o * lax.rsqrt(ms + RMS_EPS) * nrm_ref[...] * (z * _sigmoid(z))).astype(o_ref.dtype)


def dn_scan(qd, kd, w, u, att, gl, proj, dn_norm):
    s = qd.shape[0]
    C = DN_CHUNK
    row = lambda width: pl.BlockSpec((C, width), lambda c: (c, 0))
    vm = 2 * 6 * _nbytes((C, DN_WIDTH), F32) + 2 * _nbytes((DN_HEADS, DN_HEAD_DIM, DN_HEAD_DIM), F32)
    return pl.pallas_call(
        _dn_scan_body,
        grid=(s // C,),
        in_specs=[row(DN_WIDTH), row(DN_WIDTH), row(DN_WIDTH), row(DN_WIDTH), row(DN_HEADS * C),
                  pl.BlockSpec((DN_HEADS, LANES), lambda c: (c, 0)),
                  pl.BlockSpec((C, DN_WIDTH), lambda c: (c, COL_Z // DN_WIDTH)),
                  pl.BlockSpec((1, DN_HEAD_DIM), lambda c: (0, 0))],
        out_specs=row(DN_WIDTH),
        out_shape=jax.ShapeDtypeStruct((s, DN_WIDTH), BF16),
        scratch_shapes=[pltpu.VMEM((DN_HEADS, DN_HEAD_DIM, DN_HEAD_DIM), F32)],
        compiler_params=_params(("arbitrary",), vm),
        name="dn_scan",
    )(qd, kd, w, u, att, gl, proj, dn_norm.reshape(1, DN_HEAD_DIM))


def _rope_sa_body(q_ref, k_ref, v_ref, c_ref, s_ref, qo_ref, ko_ref, vo_ref):
    cs = c_ref[...]
    sn = s_ref[...]
    scale = SA_HEAD_DIM ** -0.5 * LOG2_E
    for h in range(SA_HEADS):
        sl = slice(h * SA_HEAD_DIM, (h + 1) * SA_HEAD_DIM)
        x = q_ref[:, sl]
        qo_ref[:, sl] = ((x * cs + pltpu.roll(x, SA_HEAD_DIM // 2, axis=1) * sn) * scale).astype(qo_ref.dtype)
        x = k_ref[:, sl]
        ko_ref[:, sl] = (x * cs + pltpu.roll(x, SA_HEAD_DIM // 2, axis=1) * sn).astype(ko_ref.dtype)
    vo_ref[...] = v_ref[...].astype(vo_ref.dtype)


def rope_sa(proj, cos_t, sin_t, tm=512):
    s = proj.shape[0]
    col = lambda off: pl.BlockSpec((tm, SA_WIDTH), lambda i: (i, off // SA_WIDTH))
    tab = pl.BlockSpec((tm, LANES), lambda i: (i, 0))
    out = pl.BlockSpec((tm, SA_WIDTH), lambda i: (i, 0))
    vm = 2 * 3 * (_nbytes((tm, SA_WIDTH), F32) + _nbytes((tm, SA_WIDTH), BF16)) + 4 * _nbytes((tm, SA_WIDTH), F32)
    return pl.pallas_call(
        _rope_sa_body,
        grid=(s // tm,),
        in_specs=[col(COL_QSA), col(COL_KSA), col(COL_VSA), tab, tab],
        out_specs=[out, out, out],
        out_shape=[jax.ShapeDtypeStruct((s, SA_WIDTH), BF16)] * 3,
        compiler_params=_params(("parallel",), vm),
        name="rope_sa",
    )(proj, proj, proj, cos_t, sin_t)


def _rope_ix_body(q_ref, sm_ref, c_ref, s_ref, qo_ref, klo_ref, khi_ref):
    cs = c_ref[...]
    sn = s_ref[...]
    lane = lax.broadcasted_iota(I32, cs.shape, 1)
    first = (lane & (IDX_HEAD_DIM - 1)) < IDX_HEAD_DIM // 2
    half = IDX_HEAD_DIM // 2

    def rot(x):
        swapped = jnp.where(first, pltpu.roll(x, LANES - half, axis=1), pltpu.roll(x, half, axis=1))
        return x * cs + swapped * sn

    for j in range(IDX_HEADS * IDX_HEAD_DIM // LANES):
        sl = slice(j * LANES, (j + 1) * LANES)
        qo_ref[:, sl] = rot(q_ref[:, sl]).astype(qo_ref.dtype)
    k_lo = jnp.where(lane < IDX_HEAD_DIM, rot(sm_ref[...]), 0.0)
    klo_ref[...] = k_lo.astype(klo_ref.dtype)
    khi_ref[...] = pltpu.roll(k_lo, IDX_HEAD_DIM, axis=1).astype(khi_ref.dtype)


def rope_ix(proj, small, cos_t, sin_t, tm=512):
    s = proj.shape[0]
    wq = IDX_HEADS * IDX_HEAD_DIM
    tab = pl.BlockSpec((tm, LANES), lambda i: (i, 0))
    vm = 2 * (_nbytes((tm, wq), F32) + _nbytes((tm, wq), BF16)) + 4 * _nbytes((tm, wq), F32)
    return pl.pallas_call(
        _rope_ix_body,
        grid=(s // tm,),
        in_specs=[pl.BlockSpec((tm, wq), lambda i: (i, COL_QIX // wq)), tab, tab, tab],
        out_specs=[pl.BlockSpec((tm, wq), lambda i: (i, 0)), tab, tab],
        out_shape=[jax.ShapeDtypeStruct((s, wq), BF16),
                   jax.ShapeDtypeStruct((s, LANES), BF16),
                   jax.ShapeDtypeStruct((s, LANES), BF16)],
        compiler_params=_params(("parallel",), vm),
        name="rope_ix",
    )(proj, small, cos_t, sin_t)


def _index_body(q_ref, klo_ref, khi_ref, sm_ref, mask_ref, keys_ref, *, tq, tk, topk):
    i = pl.program_id(0)
    nkt = keys_ref.shape[0]
    nk = ((i + 1) * tq + tk - 1) // tk
    w_t = (sm_ref[...] * (IDX_HEADS ** -0.5 * IDX_HEAD_DIM ** -0.5)).T
    key_l = lax.broadcasted_iota(I32, (tk, tq), 0)
    qry_g = i * tq + lax.broadcasted_iota(I32, (tk, tq), 1)

    def score_tile(kt, carry):
        off = pl.multiple_of(kt * tk, tk)
        k_lo = klo_ref[pl.ds(off, tk), :]
        k_hi = khi_ref[pl.ds(off, tk), :]
        acc = jnp.zeros((tk, tq), F32)
        for j in range(IDX_HEADS // 2):
            qp = q_ref[:, j * LANES:(j + 1) * LANES]
            acc = acc + w_t[SM_WIX + 2 * j:SM_WIX + 2 * j + 1, :] * jnp.maximum(_dot_nt(k_lo, qp), 0.0)
            acc = acc + w_t[SM_WIX + 2 * j + 1:SM_WIX + 2 * j + 2, :] * jnp.maximum(_dot_nt(k_hi, qp), 0.0)
        sc = jnp.where(kt * tk + key_l <= qry_g, acc, -jnp.inf)
        bits = pltpu.bitcast(sc, I32)
        keys_ref[kt] = bits ^ ((bits >> 31) & 0x7FFFFFFF)
        return carry

    lax.fori_loop(0, nk, score_tile, 0)

    cnt_rows = 4 * SUBLANES

    def count_ge(cand):
        def body(kt, cnt):
            hit = jnp.where(keys_ref[kt] >= cand, 1.0, 0.0)
            return cnt + jnp.sum(hit.reshape(tk // cnt_rows, cnt_rows, tq), axis=0)

        cnt = lax.fori_loop(0, nk, body, jnp.zeros((cnt_rows, tq), F32))
        return jnp.sum(cnt, axis=0, keepdims=True)

    kf = float(topk)
    ans = jnp.where(count_ge(jnp.zeros((1, tq), I32)) >= kf, 0, INT_MIN).astype(I32)

    def bit_body(b, ans):
        cand = ans + lax.shift_left(jnp.int32(1), 30 - b)
        return jnp.where(count_ge(cand) >= kf, cand, ans)

    ans = lax.fori_loop(0, 31, bit_body, ans)

    def write(kt, carry):
        sel = (keys_ref[kt] >= ans) & (kt * tk + key_l <= qry_g)
        mask_ref[0, pl.ds(pl.multiple_of(kt * tk, tk), tk), :] = jnp.where(sel, 1.0, 0.0).astype(mask_ref.dtype)
        return carry

    lax.fori_loop(0, nk, write, 0)

    def clear(kt, carry):
        mask_ref[0, pl.ds(pl.multiple_of(kt * tk, tk), tk), :] = jnp.zeros((tk, tq), mask_ref.dtype)
        return carry

    lax.fori_loop(nk, nkt, clear, 0)


def index_mask(q_ix, k_lo, k_hi, small, topk, tq, tk, t_attn):
    s = q_ix.shape[0]
    nkt = s // tk
    per = t_attn // tq
    wq = IDX_HEADS * IDX_HEAD_DIM
    vm = (2 * (_nbytes((tq, wq), BF16) + 2 * _nbytes((s, LANES), BF16) + _nbytes((tq, LANES), F32)
               + _nbytes((s, tq), BF16)) + _nbytes((s, tq), I32) + 8 * _nbytes((tk, tq), F32))
    return pl.pallas_call(
        functools.partial(_index_body, tq=tq, tk=tk, topk=topk),
        grid=(s // tq,),
        in_specs=[pl.BlockSpec((tq, wq), lambda i: (i, 0)),
                  pl.BlockSpec((s, LANES), lambda i: (0, 0)),
                  pl.BlockSpec((s, LANES), lambda i: (0, 0)),
                  pl.BlockSpec((tq, LANES), lambda i: (i, 0))],
        out_specs=pl.BlockSpec((1, s, tq), lambda i: (i // per, 0, i % per)),
        out_shape=jax.ShapeDtypeStruct((s // t_attn, s, t_attn), BF16),
        scratch_shapes=[pltpu.VMEM((nkt, tk, tq), I32)],
        compiler_params=_params(("parallel",), vm),
        name="index_mask",
    )(q_ix, k_lo, k_hi, small)


def _attn_body(qi_ref, ki_ref, q_ref, k_ref, v_ref, mk_ref, o_ref, m_ref, l_ref, acc_ref):
    p = pl.program_id(0)
    qi = qi_ref[p]
    ki = ki_ref[p]

    @pl.when(ki == 0)
    def _():
        m_ref[...] = jnp.full_like(m_ref, NEG_BIG)
        l_ref[...] = jnp.zeros_like(l_ref)
        acc_ref[...] = jnp.zeros_like(acc_ref)

    keep = mk_ref[0].astype(F32) > 0.5
    for h in range(SA_HEADS):
        sl = slice(h * SA_HEAD_DIM, (h + 1) * SA_HEAD_DIM)
        s = jnp.where(keep, _dot_nt(k_ref[:, sl], q_ref[:, sl]), NEG_BIG)
        m_old = m_ref[h:h + 1, :]
        m_new = jnp.maximum(m_old, jnp.max(s, axis=0, keepdims=True))
        alpha = jnp.exp2(m_old - m_new)
        pr = jnp.where(keep, jnp.exp2(s - m_new), 0.0)
        l_ref[h:h + 1, :] = alpha * l_ref[h:h + 1, :] + jnp.sum(pr, axis=0, keepdims=True)
        m_ref[h:h + 1, :] = m_new
        acc_ref[h] = alpha * acc_ref[h] + _dot_tn(v_ref[:, sl], pr.astype(BF16))

    @pl.when(ki == qi)
    def _():
        for h in range(SA_HEADS):
            sl = slice(h * SA_HEAD_DIM, (h + 1) * SA_HEAD_DIM)
            o_ref[:, sl] = (acc_ref[h] / l_ref[h:h + 1, :]).T.astype(o_ref.dtype)


def masked_attention(q, k, v, mask_t, t):
    s = q.shape[0]
    nb = s // t
    pairs = [(a, b) for a in range(nb) for b in range(a + 1)]
    qi = jnp.asarray([a for a, _ in pairs], I32)
    ki = jnp.asarray([b for _, b in pairs], I32)
    vm = (2 * (4 * _nbytes((t, SA_WIDTH), BF16) + _nbytes((t, t), BF16)) + _nbytes((t, SA_WIDTH), F32)
          + 2 * _nbytes((SA_HEADS, t), F32) + 8 * _nbytes((t, t), F32))
    grid_spec = pltpu.PrefetchScalarGridSpec(
        num_scalar_prefetch=2,
        grid=(len(pairs),),
        in_specs=[pl.BlockSpec((t, SA_WIDTH), lambda p, qi, ki: (qi[p], 0)),
                  pl.BlockSpec((t, SA_WIDTH), lambda p, qi, ki: (ki[p], 0)),
                  pl.BlockSpec((t, SA_WIDTH), lambda p, qi, ki: (ki[p], 0)),
                  pl.BlockSpec((1, t, t), lambda p, qi, ki: (qi[p], ki[p], 0))],
        out_specs=pl.BlockSpec((t, SA_WIDTH), lambda p, qi, ki: (qi[p], 0)),
        scratch_shapes=[pltpu.VMEM((SA_HEADS, t), F32),
                        pltpu.VMEM((SA_HEADS, t), F32),
                        pltpu.VMEM((SA_HEADS, SA_HEAD_DIM, t), F32)],
    )
    return pl.pallas_call(
        _attn_body,
        grid_spec=grid_spec,
        out_shape=jax.ShapeDtypeStruct((s, SA_WIDTH), BF16),
        compiler_params=_params(("arbitrary",), vm),
        name="masked_attention",
    )(qi, ki, q, k, v, mask_t)


def _merge_body(odn_ref, osa_ref, wdn_ref, wsa_ref, gdn_ref, gsa_ref, o_ref):
    y_dn = _dot(odn_ref[...], wdn_ref[...])
    y_sa = _dot(osa_ref[...], wsa_ref[...])
    o_ref[...] = (_sigmoid(gdn_ref[...]) * y_dn + _sigmoid(gsa_ref[...]) * y_sa).astype(o_ref.dtype)


def merge_branches(o_dn, o_sa, w_dn, w_sa, proj, tm=512, tn=512):
    s, kd = o_dn.shape
    d = w_dn.shape[1]
    vm = 2 * (2 * _nbytes((tm, kd), BF16) + 2 * _nbytes((kd, tn), BF16) + 2 * _nbytes((tm, tn), F32)
              + _nbytes((tm, tn), BF16)) + 6 * _nbytes((tm, tn), F32)
    return pl.pallas_call(
        _merge_body,
        grid=(s // tm, d // tn),
        in_specs=[pl.BlockSpec((tm, kd), lambda i, j: (i, 0)),
                  pl.BlockSpec((tm, kd), lambda i, j: (i, 0)),
                  pl.BlockSpec((kd, tn), lambda i, j: (0, j)),
                  pl.BlockSpec((kd, tn), lambda i, j: (0, j)),
                  pl.BlockSpec((tm, tn), lambda i, j: (i, COL_GDN // tn + j)),
                  pl.BlockSpec((tm, tn), lambda i, j: (i, COL_GSA // tn + j))],
        out_specs=pl.BlockSpec((tm, tn), lambda i, j: (i, j)),
        out_shape=jax.ShapeDtypeStruct((s, d), BF16),
        compiler_params=_params(("parallel", "parallel"), vm),
        name="merge_branches",
    )(o_dn, o_sa, w_dn, w_sa, proj, proj)


def _ffn_body(h_ref, x_ref, wg_ref, wu_ref, wd_ref, o_ref, acc_ref):
    f = pl.program_id(1)

    @pl.when(f == 0)
    def _():
        acc_ref[...] = jnp.zeros_like(acc_ref)

    h = h_ref[...]
    g = _dot(h, wg_ref[...])
    u = _dot(h, wu_ref[...])
    acc_ref[...] += _dot((g * _sigmoid(g) * u).astype(BF16), wd_ref[...])

    @pl.when(f == pl.num_programs(1) - 1)
    def _():
        o_ref[...] = x_ref[...] + acc_ref[...]


def ffn_dense(h, x, w_gate, w_up, w_down, tm=512, tf=512):
    s, d = h.shape
    ff = w_gate.shape[1]
    vm = (2 * (_nbytes((tm, d), BF16) + 2 * _nbytes((tm, d), F32) + 3 * _nbytes((d, tf), BF16))
          + _nbytes((tm, d), F32) + 6 * _nbytes((tm, tf), F32))
    return pl.pallas_call(
        _ffn_body,
        grid=(s // tm, ff // tf),
        in_specs=[pl.BlockSpec((tm, d), lambda i, f: (i, 0)),
                  pl.BlockSpec((tm, d), lambda i, f: (i, 0)),
                  pl.BlockSpec((d, tf), lambda i, f: (0, f)),
                  pl.BlockSpec((d, tf), lambda i, f: (0, f)),
                  pl.BlockSpec((tf, d), lambda i, f: (f, 0))],
        out_specs=pl.BlockSpec((tm, d), lambda i, f: (i, 0)),
        out_shape=jax.ShapeDtypeStruct((s, d), F32),
        scratch_shapes=[pltpu.VMEM((tm, d), F32)],
        compiler_params=_params(("parallel", "arbitrary"), vm),
        name="ffn_dense",
    )(h, x, w_gate, w_up, w_down)


def _router_body(h_ref, wr_ref, o_ref):
    logits = _dot(h_ref[...], wr_ref[...])
    lane = lax.broadcasted_iota(I32, logits.shape, 1)
    lg = jnp.where(lane < N_EXPERTS, logits, -jnp.inf)
    m1 = jnp.max(lg, axis=1, keepdims=True)
    i1 = jnp.min(jnp.where(lg == m1, lane, LANES), axis=1, keepdims=True)
    lg2 = jnp.where(lane == i1, -jnp.inf, lg)
    m2 = jnp.max(lg2, axis=1, keepdims=True)
    i2 = jnp.min(jnp.where(lg2 == m2, lane, LANES), axis=1, keepdims=True)
    e = jnp.exp(m2 - m1)
    o_ref[...] = jnp.where(lane == i1, 1.0 / (1.0 + e), 0.0) + jnp.where(lane == i2, e / (1.0 + e), 0.0)


def route_tokens(h, w_router_pad, tm=512):
    s, d = h.shape
    vm = 2 * (_nbytes((tm, d), BF16) + _nbytes((d, LANES), BF16) + _nbytes((tm, LANES), F32)) + 8 * _nbytes((tm, LANES), F32)
    return pl.pallas_call(
        _router_body,
        grid=(s // tm,),
        in_specs=[pl.BlockSpec((tm, d), lambda i: (i, 0)),
                  pl.BlockSpec((d, LANES), lambda i: (0, 0))],
        out_specs=pl.BlockSpec((tm, LANES), lambda i: (i, 0)),
        out_shape=jax.ShapeDtypeStruct((s, LANES), F32),
        compiler_params=_params(("parallel",), vm),
        name="moe_router",
    )(h, w_router_pad)


def _moe_body(h_ref, x_ref, cb_ref, wg_ref, wu_ref, wd_ref, o_ref, acc_ref):
    e = pl.program_id(1)
    f = pl.program_id(2)

    @pl.when((e == 0) & (f == 0))
    def _():
        acc_ref[...] = jnp.zeros_like(acc_ref)

    cb = cb_ref[...]
    lane = lax.broadcasted_iota(I32, cb.shape, 1)
    ce = jnp.sum(jnp.where(lane == e, cb, 0.0), axis=1, keepdims=True)
    h = h_ref[...]
    g = _dot(h, wg_ref[0])
    u = _dot(h, wu_ref[0])
    acc_ref[...] += _dot((g * _sigmoid(g) * u * ce).astype(BF16), wd_ref[0])

    @pl.when((e == pl.num_programs(1) - 1) & (f == pl.num_programs(2) - 1))
    def _():
        o_ref[...] = x_ref[...] + acc_ref[...]


def moe_experts(h, x, combine, w_gate, w_up, w_down, tm=512, tf=256):
    s, d = h.shape
    ne, _, ff = w_gate.shape
    vm = (2 * (_nbytes((tm, d), BF16) + 2 * _nbytes((tm, d), F32) + _nbytes((tm, LANES), F32)
               + 3 * _nbytes((d, tf), BF16)) + _nbytes((tm, d), F32) + 6 * _nbytes((tm, tf), F32))
    return pl.pallas_call(
        _moe_body,
        grid=(s // tm, ne, ff // tf),
        in_specs=[pl.BlockSpec((tm, d), lambda i, e, f: (i, 0)),
                  pl.BlockSpec((tm, d), lambda i, e, f: (i, 0)),
                  pl.BlockSpec((tm, LANES), lambda i, e, f: (i, 0)),
                  pl.BlockSpec((1, d, tf), lambda i, e, f: (e, 0, f)),
                  pl.BlockSpec((1, d, tf), lambda i, e, f: (e, 0, f)),
                  pl.BlockSpec((1, tf, d), lambda i, e, f: (e, f, 0))],
        out_specs=pl.BlockSpec((tm, d), lambda i, e, f: (i, 0)),
        out_shape=jax.ShapeDtypeStruct((s, d), F32),
        scratch_shapes=[pltpu.VMEM((tm, d), F32)],
        compiler_params=_params(("parallel", "arbitrary", "arbitrary"), vm),
        name="moe_experts",
    )(h, x, combine, w_gate, w_up, w_down)


def _rope_tables(positions, dim, reps):
    inv_freq = ROPE_THETA ** (-jnp.arange(0, dim, 2, dtype=F32) / dim)
    ang = positions.astype(F32)[:, None] * inv_freq
    cos, sin = jnp.cos(ang), jnp.sin(ang)
    return jnp.tile(cos, (1, 2 * reps)), jnp.tile(jnp.concatenate([-sin, sin], axis=1), (1, reps))


def _split_w_in(w):
    d = w.shape[0]
    o_a = 4 * DN_WIDTH
    o_qsa = o_a + 2 * DN_HEADS
    o_kix = o_qsa + 3 * SA_WIDTH + IDX_HEADS * IDX_HEAD_DIM
    o_wix = o_kix + IDX_HEAD_DIM
    o_g = o_wix + IDX_HEADS
    main = jnp.concatenate([w[:, :o_a], w[:, o_qsa:o_kix], w[:, o_g:]], axis=1)
    small = jnp.concatenate([w[:, o_kix:o_wix], w[:, o_a:o_qsa], w[:, o_wix:o_g],
                             jnp.zeros((d, LANES - IDX_HEAD_DIM - 2 * DN_HEADS - IDX_HEADS), w.dtype)], axis=1)
    return main.astype(BF16), small.astype(BF16)


def _pad_lanes(v, offset):
    return jnp.zeros((1, LANES), F32).at[0, offset:offset + v.shape[0]].set(v.astype(F32))


def _mixer(x, cos_sa, sin_sa, cos_ix, sin_ix, norm_gain, w_in, conv_w, a_log, dt_bias, dn_norm,
           w_dn_out, w_sa_out, w_o):
    s = x.shape[0]
    w_main, w_small = _split_w_in(w_in)
    h = rmsnorm(x, norm_gain, BF16)
    proj = matmul(h, w_main, F32, tm=min(1024, s), tn=512, name="in_proj")
    small = matmul(h, w_small, F32, tm=min(1024, s), tn=LANES, name="in_proj_small")

    qd, kd, w, u, att, gl = dn_chunk(proj, small, conv_w.astype(F32), _pad_lanes(a_log, SM_A), _pad_lanes(dt_bias, SM_A))
    o_dn = dn_scan(qd, kd, w, u, att, gl, proj, dn_norm.astype(F32))

    q_sa, k_sa, v_sa = rope_sa(proj, cos_sa, sin_sa)
    q_ix, k_lo, k_hi = rope_ix(proj, small, cos_ix, sin_ix)
    t_attn = min(512, s)
    mask_t = index_mask(q_ix, k_lo, k_hi, small, min(TOPK_MAX, s // 4), tq=min(256, s), tk=t_attn, t_attn=t_attn)
    o_sa = masked_attention(q_sa, k_sa, v_sa, mask_t, t=t_attn)

    merged = merge_branches(o_dn, o_sa, w_dn_out.astype(BF16), w_sa_out.astype(BF16), proj)
    return matmul(merged, w_o.astype(BF16), F32, tm=512, tn=512, residual=x, name="out_proj")


def kernel(x, positions, norm_mix, w_in, conv_w, a_log, dt_bias, dn_norm, w_dn_out, w_sa_out, w_o, norm_ffn, dense_w_gate, dense_w_up, dense_w_down, moe_router, moe_w_gate, moe_w_up, moe_w_down, final_norm):
    b, s, d = x.shape
    depth = norm_mix.shape[0]
    outs = []
    for bi in range(b):
        xb = x[bi]
        pos = positions[bi]
        cos_sa, sin_sa = _rope_tables(pos, SA_HEAD_DIM, 1)
        cos_ix, sin_ix = _rope_tables(pos, IDX_HEAD_DIM, 2)
        for layer in range(depth):
            xb = _mixer(xb, cos_sa, sin_sa, cos_ix, sin_ix, norm_mix[layer], w_in[layer], conv_w[layer],
                        a_log[layer], dt_bias[layer], dn_norm[layer], w_dn_out[layer], w_sa_out[layer], w_o[layer])
            h = rmsnorm(xb, norm_ffn[layer], BF16)
            j = layer // 2
            if layer % 2 == 0:
                xb = ffn_dense(h, xb, dense_w_gate[j].astype(BF16), dense_w_up[j].astype(BF16),
                               dense_w_down[j].astype(BF16))
            else:
                wr = jnp.zeros((d, LANES), BF16).at[:, :N_EXPERTS].set(moe_router[j].astype(BF16))
                combine = route_tokens(h, wr)
                xb = moe_experts(h, xb, combine, moe_w_gate[j].astype(BF16), moe_w_up[j].astype(BF16),
                                 moe_w_down[j].astype(BF16))
        outs.append(rmsnorm(xb, final_norm, x.dtype))
    return jnp.stack(outs, axis=0)
```

```python
import functools

import jax
import jax.numpy as jnp
from jax import lax
from jax.experimental import pallas as pl
from jax.experimental.pallas import tpu as pltpu

F32 = jnp.float32
BF16 = jnp.bfloat16
I32 = jnp.int32

RMS_EPS = 1e-6
L2_EPS = 1e-6
DN_HEADS = 8
DN_HEAD_DIM = 128
DN_WIDTH = DN_HEADS * DN_HEAD_DIM
CONV_WIDTH = 4
DN_CHUNK = 64
SA_HEADS = 8
SA_HEAD_DIM = 128
SA_WIDTH = SA_HEADS * SA_HEAD_DIM
IDX_HEADS = 16
IDX_HEAD_DIM = 64
TOPK_MAX = 256
ROPE_THETA = 10000.0
N_EXPERTS = 8

LANES = 128
SUBLANES = 8
VMEM_CAP_BYTES = 56 * 2**20
NEG_BIG = -1e30
INT_MIN = -2**31
LOG2_E = 1.4426950408889634

COL_QKV = 0
COL_Z = 3072
COL_QSA = 4096
COL_KSA = 5120
COL_VSA = 6144
COL_QIX = 7168
COL_GDN = 8192
COL_GSA = 10240
MAIN_WIDTH = 12288
SM_KIX = 0
SM_A = 64
SM_B = 72
SM_WIX = 80


def _params(semantics, vmem_bytes):
    return pltpu.CompilerParams(dimension_semantics=semantics,
                                vmem_limit_bytes=int(min(max(vmem_bytes, 16 * 2**20), VMEM_CAP_BYTES)))


def _nbytes(shape, dtype):
    n = 1
    for s in shape:
        n *= s
    return n * jnp.dtype(dtype).itemsize


def _sigmoid(x):
    return 1.0 / (1.0 + jnp.exp(-x))


def _dot(a, b):
    return jnp.dot(a, b, preferred_element_type=F32)


def _dot_nt(a, b):
    return lax.dot_general(a, b, (((1,), (1,)), ((), ())), preferred_element_type=F32)


def _dot_tn(a, b):
    return lax.dot_general(a, b, (((0,), (0,)), ((), ())), preferred_element_type=F32)


def _split_bf16(x):
    hi = x.astype(BF16)
    return hi, (x - hi.astype(F32)).astype(BF16)


def _dot3(a, b):
    return _dot(a[0], b[0]) + _dot(a[1], b[0]) + _dot(a[0], b[1])


def _rmsnorm_body(x_ref, g_ref, o_ref):
    x = x_ref[...]
    ms = jnp.mean(x * x, axis=-1, keepdims=True)
    o_ref[...] = (x * lax.rsqrt(ms + RMS_EPS) * g_ref[...]).astype(o_ref.dtype)


def rmsnorm(x, gain, out_dtype, tm=512):
    m, d = x.shape
    vm = 2 * (_nbytes((tm, d), F32) + _nbytes((tm, d), out_dtype)) + 4 * _nbytes((tm, d), F32)
    return pl.pallas_call(
        _rmsnorm_body,
        grid=(m // tm,),
        in_specs=[pl.BlockSpec((tm, d), lambda i: (i, 0)),
                  pl.BlockSpec((1, d), lambda i: (0, 0))],
        out_specs=pl.BlockSpec((tm, d), lambda i: (i, 0)),
        out_shape=jax.ShapeDtypeStruct((m, d), out_dtype),
        compiler_params=_params(("parallel",), vm),
        name="rmsnorm",
    )(x, gain.reshape(1, d))


def _mm_body(a_ref, b_ref, o_ref):
    o_ref[...] = _dot(a_ref[...], b_ref[...]).astype(o_ref.dtype)


def _mm_res_body(a_ref, b_ref, r_ref, o_ref):
    o_ref[...] = (r_ref[...] + _dot(a_ref[...], b_ref[...])).astype(o_ref.dtype)


def matmul(a, b, out_dtype, tm, tn, residual=None, name="matmul"):
    m, k = a.shape
    n = b.shape[1]
    in_specs = [pl.BlockSpec((tm, k), lambda i, j: (i, 0)),
                pl.BlockSpec((k, tn), lambda i, j: (0, j))]
    args = [a, b]
    body = _mm_body
    vm = 2 * (_nbytes((tm, k), a.dtype) + _nbytes((k, tn), b.dtype) + _nbytes((tm, tn), out_dtype))
    vm += 2 * _nbytes((tm, tn), F32)
    if residual is not None:
        in_specs.append(pl.BlockSpec((tm, tn), lambda i, j: (i, j)))
        args.append(residual)
        body = _mm_res_body
        vm += 2 * _nbytes((tm, tn), residual.dtype)
    return pl.pallas_call(
        body,
        grid=(m // tm, n // tn),
        in_specs=in_specs,
        out_specs=pl.BlockSpec((tm, tn), lambda i, j: (i, j)),
        out_shape=jax.ShapeDtypeStruct((m, n), out_dtype),
        compiler_params=_params(("parallel", "parallel"), vm),
        name=name,
    )(*args)


def _dn_chunk_body(xc_ref, xp_ref, sm_ref, cw_ref, alog_ref, dtb_ref,
                   qd_ref, kd_ref, w_ref, u_ref, att_ref, gl_ref, ext_ref):
    c = pl.program_id(0)
    C = DN_CHUNK
    halo = SUBLANES
    ext_ref[0:halo, :] = jnp.where(c > 0, xp_ref[...], 0.0)
    ext_ref[halo:halo + C, :] = xc_ref[...]
    cw = cw_ref[...]
    y = cw[0:1, :] * ext_ref[pl.ds(halo - CONV_WIDTH + 1, C), :]
    for j in range(1, CONV_WIDTH):
        y = y + cw[j:j + 1, :] * ext_ref[pl.ds(halo - CONV_WIDTH + 1 + j, C), :]
    y = y * _sigmoid(y)

    sm = sm_ref[...]
    xa = sm + dtb_ref[...]
    softplus = jnp.maximum(xa, 0.0) + jnp.log1p(jnp.exp(-jnp.abs(xa)))
    g = -jnp.exp(alog_ref[...]) * softplus
    beta = _sigmoid(sm)

    row = lax.broadcasted_iota(I32, (C, LANES), 0)
    gc = g
    d = 1
    while d < C:
        gc = gc + jnp.where(row >= d, pltpu.roll(gc, d, axis=0), 0.0)
        d *= 2
    gct = jnp.concatenate([gc, jnp.zeros_like(gc)], axis=0).T
    ex = jnp.exp(gc)
    gc_last = gc[C - 1:C, :]
    exl = jnp.exp(gc_last - gc)
    gl_ref[...] = jnp.exp(jnp.broadcast_to(gct[SM_A:SM_A + DN_HEADS, C - 1:C], (DN_HEADS, LANES)))

    ri = lax.broadcasted_iota(I32, (C, C), 0)
    ci = lax.broadcasted_iota(I32, (C, C), 1)
    tril = ri >= ci
    strict = ri > ci
    eye = jnp.where(ri == ci, 1.0, 0.0).astype(F32)
    lvl_masks = []
    lb = 0
    while (1 << lb) < C:
        lvl_masks.append(((ri >> (lb + 1)) == (ci >> (lb + 1)))
                         & (((ri >> lb) & 1) == 1) & (((ci >> lb) & 1) == 0))
        lb += 1

    heads = range(DN_HEADS)
    a_mats, kbs, vbs, excols = [], [], [], []
    for h in heads:
        sl = slice(h * DN_HEAD_DIM, (h + 1) * DN_HEAD_DIM)
        qh = y[:, h * DN_HEAD_DIM:(h + 1) * DN_HEAD_DIM]
        kh = y[:, DN_WIDTH + h * DN_HEAD_DIM:DN_WIDTH + (h + 1) * DN_HEAD_DIM]
        vh = y[:, 2 * DN_WIDTH + h * DN_HEAD_DIM:2 * DN_WIDTH + (h + 1) * DN_HEAD_DIM]
        qn = qh * lax.rsqrt(jnp.sum(qh * qh, axis=-1, keepdims=True) + L2_EPS) * (DN_HEAD_DIM ** -0.5)
        kn = kh * lax.rsqrt(jnp.sum(kh * kh, axis=-1, keepdims=True) + L2_EPS)
        bcol = beta[:, SM_B + h:SM_B + h + 1]
        gcol = gc[:, SM_A + h:SM_A + h + 1]
        grow = gct[SM_A + h:SM_A + h + 1, 0:C]
        dec = jnp.exp(jnp.where(tril, gcol - grow, -jnp.inf))
        kb = kn * bcol
        knb = kn.astype(BF16)
        excol = ex[:, SM_A + h:SM_A + h + 1]
        a_mats.append(jnp.where(strict, _dot_nt(kb.astype(BF16), knb) * dec, 0.0))
        kbs.append(kb * excol)
        vbs.append(vh * bcol)
        att = jnp.where(tril, _dot_nt(qn.astype(BF16), knb) * dec, 0.0)
        att_ref[:, h * C:(h + 1) * C] = att.astype(att_ref.dtype)
        qd_ref[:, sl] = (qn * excol).astype(qd_ref.dtype)
        kd_ref[:, sl] = (kn * exl[:, SM_A + h:SM_A + h + 1]).astype(kd_ref.dtype)

    x_inv = [eye - jnp.where(lvl_masks[0], a, 0.0) for a in a_mats]
    for lm in lvl_masks[1:]:
        xs = [_split_bf16(x) for x in x_inv]
        ts = [_dot3(xs[h], _split_bf16(jnp.where(lm, a_mats[h], 0.0))) for h in heads]
        x_inv = [x_inv[h] - _dot3(_split_bf16(ts[h]), xs[h]) for h in heads]
    xs = [_split_bf16(x) for x in x_inv]
    for h in heads:
        sl = slice(h * DN_HEAD_DIM, (h + 1) * DN_HEAD_DIM)
        w_ref[:, sl] = _dot3(xs[h], _split_bf16(kbs[h])).astype(w_ref.dtype)
        u_ref[:, sl] = _dot3(xs[h], _split_bf16(vbs[h]))


def dn_chunk(proj, small, conv_w, alog_pad, dtb_pad):
    s = proj.shape[0]
    C = DN_CHUNK
    n_chunks = s // C
    w3 = 3 * DN_WIDTH
    row_spec = lambda width, dt: pl.BlockSpec((C, width), lambda c: (c, 0))
    vm = 2 * (_nbytes((C, w3), F32) + _nbytes((SUBLANES, w3), F32)) + 8 * _nbytes((C, w3), F32)
    return pl.pallas_call(
        _dn_chunk_body,
        grid=(n_chunks,),
        in_specs=[pl.BlockSpec((C, w3), lambda c: (c, COL_QKV // w3)),
                  pl.BlockSpec((SUBLANES, w3), lambda c: (jnp.maximum(c * (C // SUBLANES) - 1, 0), COL_QKV // w3)),
                  pl.BlockSpec((C, LANES), lambda c: (c, 0)),
                  pl.BlockSpec((CONV_WIDTH, w3), lambda c: (0, 0)),
                  pl.BlockSpec((1, LANES), lambda c: (0, 0)),
                  pl.BlockSpec((1, LANES), lambda c: (0, 0))],
        out_specs=[row_spec(DN_WIDTH, BF16), row_spec(DN_WIDTH, BF16), row_spec(DN_WIDTH, BF16),
                   row_spec(DN_WIDTH, F32), row_spec(DN_HEADS * C, BF16),
                   pl.BlockSpec((DN_HEADS, LANES), lambda c: (c, 0))],
        out_shape=[jax.ShapeDtypeStruct((s, DN_WIDTH), BF16),
                   jax.ShapeDtypeStruct((s, DN_WIDTH), BF16),
                   jax.ShapeDtypeStruct((s, DN_WIDTH), BF16),
                   jax.ShapeDtypeStruct((s, DN_WIDTH), F32),
                   jax.ShapeDtypeStruct((s, DN_HEADS * C), BF16),
                   jax.ShapeDtypeStruct((n_chunks * DN_HEADS, LANES), F32)],
        scratch_shapes=[pltpu.VMEM((SUBLANES + C, w3), F32)],
        compiler_params=_params(("parallel",), vm),
        name="dn_chunk",
    )(proj, proj, small, conv_w, alog_pad, dtb_pad)


def _dn_scan_body(qd_ref, kd_ref, w_ref, u_ref, att_ref, gl_ref, z_ref, nrm_ref, o_ref, st_ref):
    c = pl.program_id(0)
    C = DN_CHUNK

    @pl.when(c == 0)
    def _():
        st_ref[...] = jnp.zeros_like(st_ref)

    for h in range(DN_HEADS):
        sl = slice(h * DN_HEAD_DIM, (h + 1) * DN_HEAD_DIM)
        state = st_ref[h]
        sb = state.astype(BF16)
        v_new = u_ref[:, sl] - _dot(w_ref[:, sl], sb)
        vb = v_new.astype(BF16)
        o = _dot(qd_ref[:, sl], sb) + _dot(att_ref[:, h * C:(h + 1) * C], vb)
        st_ref[h] = state * gl_ref[h:h + 1, :] + _dot_tn(kd_ref[:, sl], vb)
        ms = jnp.mean(o * o, axis=-1, keepdims=True)
        z = z_ref[:, sl]
        o_ref[:, sl] = (o * lax.rsqrt(ms + RMS_EPS) * nrm_ref[...] * (z * _sigmoid(z))).astype(o_ref.dtype)


def dn_scan(qd, kd, w, u, att, gl, proj, dn_norm):
    s = qd.shape[0]
    C = DN_CHUNK
    row = lambda width: pl.BlockSpec((C, width), lambda c: (c, 0))
    vm = 2 * 6 * _nbytes((C, DN_WIDTH), F32) + 2 * _nbytes((DN_HEADS, DN_HEAD_DIM, DN_HEAD_DIM), F32)
    return pl.pallas_call(
        _dn_scan_body,
        grid=(s // C,),
        in_specs=[row(DN_WIDTH), row(DN_WIDTH), row(DN_WIDTH), row(DN_WIDTH), row(DN_HEADS * C),
                  pl.BlockSpec((DN_HEADS, LANES), lambda c: (c, 0)),
                  pl.BlockSpec((C, DN_WIDTH), lambda c: (c, COL_Z // DN_WIDTH)),
                  pl.BlockSpec((1, DN_HEAD_DIM), lambda c: (0, 0))],
        out_specs=row(DN_WIDTH),
        out_shape=jax.ShapeDtypeStruct((s, DN_WIDTH), BF16),
        scratch_shapes=[pltpu.VMEM((DN_HEADS, DN_HEAD_DIM, DN_HEAD_DIM), F32)],
        compiler_params=_params(("arbitrary",), vm),
        name="dn_scan",
    )(qd, kd, w, u, att, gl, proj, dn_norm.reshape(1, DN_HEAD_DIM))


def _rope_sa_body(q_ref, k_ref, v_ref, c_ref, s_ref, qo_ref, ko_ref, vo_ref):
    cs = c_ref[...]
    sn = s_ref[...]
    scale = SA_HEAD_DIM ** -0.5 * LOG2_E
    for h in range(SA_HEADS):
        sl = slice(h * SA_HEAD_DIM, (h + 1) * SA_HEAD_DIM)
        x = q_ref[:, sl]
        qo_ref[:, sl] = ((x * cs + pltpu.roll(x, SA_HEAD_DIM // 2, axis=1) * sn) * scale).astype(qo_ref.dtype)
        x = k_ref[:, sl]
        ko_ref[:, sl] = (x * cs + pltpu.roll(x, SA_HEAD_DIM // 2, axis=1) * sn).astype(ko_ref.dtype)
    vo_ref[...] = v_ref[...].astype(vo_ref.dtype)


def rope_sa(proj, cos_t, sin_t, tm=512):
    s = proj.shape[0]
    col = lambda off: pl.BlockSpec((tm, SA_WIDTH), lambda i: (i, off // SA_WIDTH))
    tab = pl.BlockSpec((tm, LANES), lambda i: (i, 0))
    out = pl.BlockSpec((tm, SA_WIDTH), lambda i: (i, 0))
    vm = 2 * 3 * (_nbytes((tm, SA_WIDTH), F32) + _nbytes((tm, SA_WIDTH), BF16)) + 4 * _nbytes((tm, SA_WIDTH), F32)
    return pl.pallas_call(
        _rope_sa_body,
        grid=(s // tm,),
        in_specs=[col(COL_QSA), col(COL_KSA), col(COL_VSA), tab, tab],
        out_specs=[out, out, out],
        out_shape=[jax.ShapeDtypeStruct((s, SA_WIDTH), BF16)] * 3,
        compiler_params=_params(("parallel",), vm),
        name="rope_sa",
    )(proj, proj, proj, cos_t, sin_t)


def _rope_ix_body(q_ref, sm_ref, c_ref, s_ref, qo_ref, klo_ref, khi_ref):
    cs = c_ref[...]
    sn = s_ref[...]
    lane = lax.broadcasted_iota(I32, cs.shape, 1)
    first = (lane & (IDX_HEAD_DIM - 1)) < IDX_HEAD_DIM // 2
    half = IDX_HEAD_DIM // 2

    def rot(x):
        swapped = jnp.where(first, pltpu.roll(x, LANES - half, axis=1), pltpu.roll(x, half, axis=1))
        return x * cs + swapped * sn

    for j in range(IDX_HEADS * IDX_HEAD_DIM // LANES):
        sl = slice(j * LANES, (j + 1) * LANES)
        qo_ref[:, sl] = rot(q_ref[:, sl]).astype(qo_ref.dtype)
    k_lo = jnp.where(lane < IDX_HEAD_DIM, rot(sm_ref[...]), 0.0)
    klo_ref[...] = k_lo.astype(klo_ref.dtype)
    khi_ref[...] = pltpu.roll(k_lo, IDX_HEAD_DIM, axis=1).astype(khi_ref.dtype)


def rope_ix(proj, small, cos_t, sin_t, tm=512):
    s = proj.shape[0]
    wq = IDX_HEADS * IDX_HEAD_DIM
    tab = pl.BlockSpec((tm, LANES), lambda i: (i, 0))
    vm = 2 * (_nbytes((tm, wq), F32) + _nbytes((tm, wq), BF16)) + 4 * _nbytes((tm, wq), F32)
    return pl.pallas_call(
        _rope_ix_body,
        grid=(s // tm,),
        in_specs=[pl.BlockSpec((tm, wq), lambda i: (i, COL_QIX // wq)), tab, tab, tab],
        out_specs=[pl.BlockSpec((tm, wq), lambda i: (i, 0)), tab, tab],
        out_shape=[jax.ShapeDtypeStruct((s, wq), BF16),
                   jax.ShapeDtypeStruct((s, LANES), BF16),
                   jax.ShapeDtypeStruct((s, LANES), BF16)],
        compiler_params=_params(("parallel",), vm),
        name="rope_ix",
    )(proj, small, cos_t, sin_t)


def _index_body(q_ref, klo_ref, khi_ref, sm_ref, mask_ref, keys_ref, *, tq, tk, topk):
    i = pl.program_id(0)
    nkt = keys_ref.shape[0]
    nk = ((i + 1) * tq + tk - 1) // tk
    w_t = (sm_ref[...] * (IDX_HEADS ** -0.5 * IDX_HEAD_DIM ** -0.5)).T
    key_l = lax.broadcasted_iota(I32, (tk, tq), 0)
    qry_g = i * tq + lax.broadcasted_iota(I32, (tk, tq), 1)

    def score_tile(kt, carry):
        off = pl.multiple_of(kt * tk, tk)
        k_lo = klo_ref[pl.ds(off, tk), :]
        k_hi = khi_ref[pl.ds(off, tk), :]
        acc = jnp.zeros((tk, tq), F32)
        for j in range(IDX_HEADS // 2):
            qp = q_ref[:, j * LANES:(j + 1) * LANES]
            acc = acc + w_t[SM_WIX + 2 * j:SM_WIX + 2 * j + 1, :] * jnp.maximum(_dot_nt(k_lo, qp), 0.0)
            acc = acc + w_t[SM_WIX + 2 * j + 1:SM_WIX + 2 * j + 2, :] * jnp.maximum(_dot_nt(k_hi, qp), 0.0)
        sc = jnp.where(kt * tk + key_l <= qry_g, acc, -jnp.inf)
        bits = pltpu.bitcast(sc, I32)
        keys_ref[kt] = bits ^ ((bits >> 31) & 0x7FFFFFFF)
        return carry

    lax.fori_loop(0, nk, score_tile, 0)

    cnt_rows = 4 * SUBLANES

    def count_ge(cand):
        def body(kt, cnt):
            hit = jnp.where(keys_ref[kt] >= cand, 1.0, 0.0)
            return cnt + jnp.sum(hit.reshape(tk // cnt_rows, cnt_rows, tq), axis=0)

        cnt = lax.fori_loop(0, nk, body, jnp.zeros((cnt_rows, tq), F32))
        return jnp.sum(cnt, axis=0, keepdims=True)

    kf = float(topk)
    ans = jnp.where(count_ge(jnp.zeros((1, tq), I32)) >= kf, 0, INT_MIN).astype(I32)

    def bit_body(b, ans):
        cand = ans + lax.shift_left(jnp.int32(1), 30 - b)
        return jnp.where(count_ge(cand) >= kf, cand, ans)

    ans = lax.fori_loop(0, 31, bit_body, ans)

    def write(kt, carry):
        sel = (keys_ref[kt] >= ans) & (kt * tk + key_l <= qry_g)
        mask_ref[0, pl.ds(pl.multiple_of(kt * tk, tk), tk), :] = jnp.where(sel, 1.0, 0.0).astype(mask_ref.dtype)
        return carry

    lax.fori_loop(0, nk, write, 0)

    def clear(kt, carry):
        mask_ref[0, pl.ds(pl.multiple_of(kt * tk, tk), tk), :] = jnp.zeros((tk, tq), mask_ref.dtype)
        return carry

    lax.fori_loop(nk, nkt, clear, 0)


def index_mask(q_ix, k_lo, k_hi, small, topk, tq, tk, t_attn):
    s = q_ix.shape[0]
    nkt = s // tk
    per = t_attn // tq
    wq = IDX_HEADS * IDX_HEAD_DIM
    vm = (2 * (_nbytes((tq, wq), BF16) + 2 * _nbytes((s, LANES), BF16) + _nbytes((tq, LANES), F32)
               + _nbytes((s, tq), BF16)) + _nbytes((s, tq), I32) + 8 * _nbytes((tk, tq), F32))
    return pl.pallas_call(
        functools.partial(_index_body, tq=tq, tk=tk, topk=topk),
        grid=(s // tq,),
        in_specs=[pl.BlockSpec((tq, wq), lambda i: (i, 0)),
                  pl.BlockSpec((s, LANES), lambda i: (0, 0)),
                  pl.BlockSpec((s, LANES), lambda i: (0, 0)),
                  pl.BlockSpec((tq, LANES), lambda i: (i, 0))],
        out_specs=pl.BlockSpec((1, s, tq), lambda i: (i // per, 0, i % per)),
        out_shape=jax.ShapeDtypeStruct((s // t_attn, s, t_attn), BF16),
        scratch_shapes=[pltpu.VMEM((nkt, tk, tq), I32)],
        compiler_params=_params(("parallel",), vm),
        name="index_mask",
    )(q_ix, k_lo, k_hi, small)


def _attn_body(qi_ref, ki_ref, q_ref, k_ref, v_ref, mk_ref, o_ref, m_ref, l_ref, acc_ref):
    p = pl.program_id(0)
    qi = qi_ref[p]
    ki = ki_ref[p]

    @pl.when(ki == 0)
    def _():
        m_ref[...] = jnp.full_like(m_ref, NEG_BIG)
        l_ref[...] = jnp.zeros_like(l_ref)
        acc_ref[...] = jnp.zeros_like(acc_ref)

    keep = mk_ref[0].astype(F32) > 0.5
    for h in range(SA_HEADS):
        sl = slice(h * SA_HEAD_DIM, (h + 1) * SA_HEAD_DIM)
        s = jnp.where(keep, _dot_nt(k_ref[:, sl], q_ref[:, sl]), NEG_BIG)
        m_old = m_ref[h:h + 1, :]
        m_new = jnp.maximum(m_old, jnp.max(s, axis=0, keepdims=True))
        alpha = jnp.exp2(m_old - m_new)
        pr = jnp.exp2(s - m_new)
        l_ref[h:h + 1, :] = alpha * l_ref[h:h + 1, :] + jnp.sum(pr, axis=0, keepdims=True)
        m_ref[h:h + 1, :] = m_new
        acc_ref[h] = alpha * acc_ref[h] + _dot_tn(v_ref[:, sl], pr.astype(BF16))

    @pl.when(ki == qi)
    def _():
        for h in range(SA_HEADS):
            sl = slice(h * SA_HEAD_DIM, (h + 1) * SA_HEAD_DIM)
            o_ref[:, sl] = (acc_ref[h] / l_ref[h:h + 1, :]).T.astype(o_ref.dtype)


def masked_attention(q, k, v, mask_t, t):
    s = q.shape[0]
    nb = s // t
    pairs = [(a, b) for a in range(nb) for b in range(a + 1)]
    qi = jnp.asarray([a for a, _ in pairs], I32)
    ki = jnp.asarray([b for _, b in pairs], I32)
    vm = (2 * (4 * _nbytes((t, SA_WIDTH), BF16) + _nbytes((t, t), BF16)) + _nbytes((t, SA_WIDTH), F32)
          + 2 * _nbytes((SA_HEADS, t), F32) + 8 * _nbytes((t, t), F32))
    grid_spec = pltpu.PrefetchScalarGridSpec(
        num_scalar_prefetch=2,
        grid=(len(pairs),),
        in_specs=[pl.BlockSpec((t, SA_WIDTH), lambda p, qi, ki: (qi[p], 0)),
                  pl.BlockSpec((t, SA_WIDTH), lambda p, qi, ki: (ki[p], 0)),
                  pl.BlockSpec((t, SA_WIDTH), lambda p, qi, ki: (ki[p], 0)),
                  pl.BlockSpec((1, t, t), lambda p, qi, ki: (qi[p], ki[p], 0))],
        out_specs=pl.BlockSpec((t, SA_WIDTH), lambda p, qi, ki: (qi[p], 0)),
        scratch_shapes=[pltpu.VMEM((SA_HEADS, t), F32),
                        pltpu.VMEM((SA_HEADS, t), F32),
                        pltpu.VMEM((SA_HEADS, SA_HEAD_DIM, t), F32)],
    )
    return pl.pallas_call(
        _attn_body,
        grid_spec=grid_spec,
        out_shape=jax.ShapeDtypeStruct((s, SA_WIDTH), BF16),
        compiler_params=_params(("arbitrary",), vm),
        name="masked_attention",
    )(qi, ki, q, k, v, mask_t)


def _merge_body(odn_ref, osa_ref, wdn_ref, wsa_ref, gdn_ref, gsa_ref, o_ref):
    y_dn = _dot(odn_ref[...], wdn_ref[...])
    y_sa = _dot(osa_ref[...], wsa_ref[...])
    o_ref[...] = (_sigmoid(gdn_ref[...]) * y_dn + _sigmoid(gsa_ref[...]) * y_sa).astype(o_ref.dtype)


def merge_branches(o_dn, o_sa, w_dn, w_sa, proj, tm=512, tn=512):
    s, kd = o_dn.shape
    d = w_dn.shape[1]
    vm = 2 * (2 * _nbytes((tm, kd), BF16) + 2 * _nbytes((kd, tn), BF16) + 2 * _nbytes((tm, tn), F32)
              + _nbytes((tm, tn), BF16)) + 6 * _nbytes((tm, tn), F32)
    return pl.pallas_call(
        _merge_body,
        grid=(s // tm, d // tn),
        in_specs=[pl.BlockSpec((tm, kd), lambda i, j: (i, 0)),
                  pl.BlockSpec((tm, kd), lambda i, j: (i, 0)),
                  pl.BlockSpec((kd, tn), lambda i, j: (0, j)),
                  pl.BlockSpec((kd, tn), lambda i, j: (0, j)),
                  pl.BlockSpec((tm, tn), lambda i, j: (i, COL_GDN // tn + j)),
                  pl.BlockSpec((tm, tn), lambda i, j: (i, COL_GSA // tn + j))],
        out_specs=pl.BlockSpec((tm, tn), lambda i, j: (i, j)),
        out_shape=jax.ShapeDtypeStruct((s, d), BF16),
        compiler_params=_params(("parallel", "parallel"), vm),
        name="merge_branches",
    )(o_dn, o_sa, w_dn, w_sa, proj, proj)


def _ffn_body(h_ref, x_ref, wg_ref, wu_ref, wd_ref, o_ref, acc_ref):
    f = pl.program_id(1)

    @pl.when(f == 0)
    def _():
        acc_ref[...] = jnp.zeros_like(acc_ref)

    h = h_ref[...]
    g = _dot(h, wg_ref[...])
    u = _dot(h, wu_ref[...])
    acc_ref[...] += _dot((g * _sigmoid(g) * u).astype(BF16), wd_ref[...])

    @pl.when(f == pl.num_programs(1) - 1)
    def _():
        o_ref[...] = x_ref[...] + acc_ref[...]


def ffn_dense(h, x, w_gate, w_up, w_down, tm=512, tf=512):
    s, d = h.shape
    ff = w_gate.shape[1]
    vm = (2 * (_nbytes((tm, d), BF16) + 2 * _nbytes((tm, d), F32) + 3 * _nbytes((d, tf), BF16))
          + _nbytes((tm, d), F32) + 6 * _nbytes((tm, tf), F32))
    return pl.pallas_call(
        _ffn_body,
        grid=(s // tm, ff // tf),
        in_specs=[pl.BlockSpec((tm, d), lambda i, f: (i, 0)),
                  pl.BlockSpec((tm, d), lambda i, f: (i, 0)),
                  pl.BlockSpec((d, tf), lambda i, f: (0, f)),
                  pl.BlockSpec((d, tf), lambda i, f: (0, f)),
                  pl.BlockSpec((tf, d), lambda i, f: (f, 0))],
        out_specs=pl.BlockSpec((tm, d), lambda i, f: (i, 0)),
        out_shape=jax.ShapeDtypeStruct((s, d), F32),
        scratch_shapes=[pltpu.VMEM((tm, d), F32)],
        compiler_params=_params(("parallel", "arbitrary"), vm),
        name="ffn_dense",
    )(h, x, w_gate, w_up, w_down)


def _router_body(h_ref, wr_ref, w_ref, m1_ref, m2_ref, rank_ref, tot_ref, cnt_ref):
    logits = _dot(h_ref[...], wr_ref[...])
    lane = lax.broadcasted_iota(I32, logits.shape, 1)
    lg = jnp.where(lane < N_EXPERTS, logits, -jnp.inf)
    m1 = jnp.max(lg, axis=1, keepdims=True)
    i1 = jnp.min(jnp.where(lg == m1, lane, LANES), axis=1, keepdims=True)
    lg2 = jnp.where(lane == i1, -jnp.inf, lg)
    m2 = jnp.max(lg2, axis=1, keepdims=True)
    i2 = jnp.min(jnp.where(lg2 == m2, lane, LANES), axis=1, keepdims=True)
    e = jnp.exp(m2 - m1)
    first = lane == i1
    second = lane == i2
    w_ref[...] = jnp.where(first, 1.0 / (1.0 + e), 0.0) + jnp.where(second, e / (1.0 + e), 0.0)
    m1_ref[...] = jnp.where(first, 1.0, 0.0)
    m2_ref[...] = jnp.where(second, 1.0, 0.0)

    @pl.when(pl.program_id(0) == 0)
    def _():
        cnt_ref[...] = jnp.zeros_like(cnt_ref)

    tm = logits.shape[0]
    sel = jnp.where(first | second, 1.0, 0.0).astype(BF16)
    ri = lax.broadcasted_iota(I32, (tm, tm), 0)
    ci = lax.broadcasted_iota(I32, (tm, tm), 1)
    before = jnp.where(ri > ci, 1.0, 0.0).astype(BF16)
    run = cnt_ref[0:1, :]
    rank_ref[...] = run + _dot(before, sel)
    run = run + jnp.sum(sel.astype(F32), axis=0, keepdims=True)
    cnt_ref[...] = jnp.broadcast_to(run, cnt_ref.shape)
    tot_ref[...] = jnp.broadcast_to(run, tot_ref.shape)


def route_tokens(h, w_router_pad, tm=512):
    s, d = h.shape
    tm = min(tm, s)
    vm = 2 * (_nbytes((tm, d), BF16) + _nbytes((d, LANES), BF16) + 4 * _nbytes((tm, LANES), F32)) + 4 * _nbytes((tm, tm), F32)
    row = pl.BlockSpec((tm, LANES), lambda i: (i, 0))
    return pl.pallas_call(
        _router_body,
        grid=(s // tm,),
        in_specs=[pl.BlockSpec((tm, d), lambda i: (i, 0)),
                  pl.BlockSpec((d, LANES), lambda i: (0, 0))],
        out_specs=[row, row, row, row, pl.BlockSpec((SUBLANES, LANES), lambda i: (0, 0))],
        out_shape=[jax.ShapeDtypeStruct((s, LANES), F32)] * 4 + [jax.ShapeDtypeStruct((SUBLANES, LANES), F32)],
        scratch_shapes=[pltpu.VMEM((SUBLANES, LANES), F32)],
        compiler_params=_params(("arbitrary",), vm),
        name="moe_router",
    )(h, w_router_pad)


def _plan_body(w_ref, m1_ref, m2_ref, rank_ref, tot_ref, pw_ref, meta_ref, *, tile):
    lane8 = lax.broadcasted_iota(I32, (SUBLANES, LANES), 1)
    n = tot_ref[...]
    padded = jnp.floor((n + (tile - 1.0)) * (1.0 / tile)) * tile
    ends = padded
    d = 1
    while d < N_EXPERTS:
        ends = ends + jnp.where(lane8 >= d, pltpu.roll(ends, d, axis=1), 0.0)
        d *= 2
    start = (ends - padded)[0:1, :]
    posf = start + rank_ref[...]
    m1 = m1_ref[...]
    m2 = m2_ref[...]
    w = w_ref[...]
    lane = lax.broadcasted_iota(I32, w.shape, 1)
    cols = [jnp.sum(m1 * posf, axis=1, keepdims=True), jnp.sum(m2 * posf, axis=1, keepdims=True),
            jnp.sum(m1 * w, axis=1, keepdims=True), jnp.sum(m2 * w, axis=1, keepdims=True)]
    out = jnp.zeros_like(w)
    for j, col in enumerate(cols):
        out = jnp.where(lane == j, col, out)
    pw_ref[...] = out
    tile_start = lane8.astype(F32) * tile
    owner = jnp.zeros((SUBLANES, LANES), F32)
    for e in range(N_EXPERTS - 1):
        owner = owner + jnp.where(tile_start >= ends[:, e:e + 1], 1.0, 0.0)
    n_used = ends[:, N_EXPERTS - 1:N_EXPERTS] * (1.0 / tile)
    row8 = lax.broadcasted_iota(I32, (SUBLANES, LANES), 0)
    meta_ref[...] = jnp.where(row8 == 0, owner, jnp.broadcast_to(n_used, owner.shape))


def dispatch_plan(w, m1, m2, rank, totals, tile):
    s = w.shape[0]
    full = pl.BlockSpec((s, LANES), lambda: (0, 0))
    small = pl.BlockSpec((SUBLANES, LANES), lambda: (0, 0))
    vm = 2 * 5 * _nbytes((s, LANES), F32) + 8 * _nbytes((s, LANES), F32)
    return pl.pallas_call(
        functools.partial(_plan_body, tile=float(tile)),
        in_specs=[full, full, full, full, small],
        out_specs=[full, small],
        out_shape=[jax.ShapeDtypeStruct((s, LANES), F32), jax.ShapeDtypeStruct((SUBLANES, LANES), F32)],
        compiler_params=pltpu.CompilerParams(vmem_limit_bytes=int(min(vm, VMEM_CAP_BYTES))),
        name="moe_plan",
    )(w, m1, m2, rank, totals)


def _row_copy(src_ref, src_row, dst_ref, dst_row, sem):
    return pltpu.make_async_copy(src_ref.at[pl.ds(src_row, 1), :], dst_ref.at[pl.ds(dst_row, 1), :], sem)


def _dispatch_body(p1_ref, p2_ref, x_ref, xg_in_ref, xg_ref, sem, *, tt):
    del xg_in_ref
    base = pl.program_id(0) * tt

    def issue(r, carry):
        _row_copy(x_ref, r, xg_ref, p1_ref[base + r], sem).start()
        _row_copy(x_ref, r, xg_ref, p2_ref[base + r], sem).start()
        return carry

    lax.fori_loop(0, tt, issue, 0)

    def drain(r, carry):
        _row_copy(x_ref, 0, xg_ref, 0, sem).wait()
        _row_copy(x_ref, 0, xg_ref, 0, sem).wait()
        return carry

    lax.fori_loop(0, tt, drain, 0)


def dispatch_rows(x, pos1, pos2, n_rows, tt=256):
    s, d = x.shape
    tt = min(tt, s)
    grid_spec = pltpu.PrefetchScalarGridSpec(
        num_scalar_prefetch=2,
        grid=(s // tt,),
        in_specs=[pl.BlockSpec((tt, d), lambda i, p1, p2: (i, 0)),
                  pl.BlockSpec(memory_space=pl.ANY)],
        out_specs=pl.BlockSpec(memory_space=pl.ANY),
        scratch_shapes=[pltpu.SemaphoreType.DMA(())],
    )
    return pl.pallas_call(
        functools.partial(_dispatch_body, tt=tt),
        grid_spec=grid_spec,
        out_shape=jax.ShapeDtypeStruct((n_rows, d), x.dtype),
        input_output_aliases={3: 0},
        compiler_params=_params(("arbitrary",), 4 * _nbytes((tt, d), F32)),
        name="moe_dispatch",
    )(pos1, pos2, x, jnp.zeros((n_rows, d), x.dtype))


def _combine_body(p1_ref, p2_ref, x_ref, pw_ref, y_ref, o_ref, b1_ref, b2_ref, sem, *, tt):
    base = pl.program_id(0) * tt

    def issue(r, carry):
        _row_copy(y_ref, p1_ref[base + r], b1_ref, r, sem).start()
        _row_copy(y_ref, p2_ref[base + r], b2_ref, r, sem).start()
        return carry

    lax.fori_loop(0, tt, issue, 0)

    def drain(r, carry):
        _row_copy(y_ref, 0, b1_ref, 0, sem).wait()
        _row_copy(y_ref, 0, b2_ref, 0, sem).wait()
        return carry

    lax.fori_loop(0, tt, drain, 0)
    pw = pw_ref[...]
    o_ref[...] = x_ref[...] + pw[:, 2:3] * b1_ref[...] + pw[:, 3:4] * b2_ref[...]


def combine_rows(x, pw, y, pos1, pos2, tt=256):
    s, d = x.shape
    tt = min(tt, s)
    grid_spec = pltpu.PrefetchScalarGridSpec(
        num_scalar_prefetch=2,
        grid=(s // tt,),
        in_specs=[pl.BlockSpec((tt, d), lambda i, p1, p2: (i, 0)),
                  pl.BlockSpec((tt, LANES), lambda i, p1, p2: (i, 0)),
                  pl.BlockSpec(memory_space=pl.ANY)],
        out_specs=pl.BlockSpec((tt, d), lambda i, p1, p2: (i, 0)),
        scratch_shapes=[pltpu.VMEM((tt, d), F32), pltpu.VMEM((tt, d), F32), pltpu.SemaphoreType.DMA(())],
    )
    return pl.pallas_call(
        functools.partial(_combine_body, tt=tt),
        grid_spec=grid_spec,
        out_shape=jax.ShapeDtypeStruct((s, d), F32),
        compiler_params=_params(("arbitrary",), 8 * _nbytes((tt, d), F32)),
        name="moe_combine",
    )(pos1, pos2, x, pw, y)


def _moe_body(te_ref, nu_ref, xg_ref, gain_ref, wg_ref, wu_ref, wd_ref, o_ref, h_ref, acc_ref):
    i = pl.program_id(0)
    f = pl.program_id(1)

    @pl.when(i < nu_ref[0])
    def _():
        @pl.when(f == 0)
        def _():
            x = xg_ref[...]
            ms = jnp.mean(x * x, axis=-1, keepdims=True)
            h_ref[...] = (x * lax.rsqrt(ms + RMS_EPS) * gain_ref[...]).astype(h_ref.dtype)
            acc_ref[...] = jnp.zeros_like(acc_ref)

        h = h_ref[...]
        g = _dot(h, wg_ref[0])
        u = _dot(h, wu_ref[0])
        acc_ref[...] += _dot((g * _sigmoid(g) * u).astype(BF16), wd_ref[0])

        @pl.when(f == pl.num_programs(1) - 1)
        def _():
            o_ref[...] = acc_ref[...]

    @pl.when((i >= nu_ref[0]) & (f == 0))
    def _():
        o_ref[...] = jnp.zeros_like(o_ref)


def moe_experts(xg, gain, tile_expert, n_used, w_gate, w_up, w_down, tm, tf=256):
    p, d = xg.shape
    ff = w_gate.shape[2]
    nf = ff // tf
    vm = (2 * (2 * _nbytes((tm, d), F32) + 3 * _nbytes((d, tf), BF16)) + _nbytes((tm, d), BF16)
          + _nbytes((tm, d), F32) + 6 * _nbytes((tm, tf), F32))

    def row(i, f, te, nu):
        return (jnp.minimum(i, nu[0] - 1), 0)

    def fcol(i, f, nu):
        return jnp.where(i < nu[0], f, nf - 1)

    grid_spec = pltpu.PrefetchScalarGridSpec(
        num_scalar_prefetch=2,
        grid=(p // tm, nf),
        in_specs=[pl.BlockSpec((tm, d), row),
                  pl.BlockSpec((1, d), lambda i, f, te, nu: (0, 0)),
                  pl.BlockSpec((1, d, tf), lambda i, f, te, nu: (te[i], 0, fcol(i, f, nu))),
                  pl.BlockSpec((1, d, tf), lambda i, f, te, nu: (te[i], 0, fcol(i, f, nu))),
                  pl.BlockSpec((1, tf, d), lambda i, f, te, nu: (te[i], fcol(i, f, nu), 0))],
        out_specs=pl.BlockSpec((tm, d), lambda i, f, te, nu: (i, 0)),
        scratch_shapes=[pltpu.VMEM((tm, d), BF16), pltpu.VMEM((tm, d), F32)],
    )
    return pl.pallas_call(
        _moe_body,
        grid_spec=grid_spec,
        out_shape=jax.ShapeDtypeStruct((p, d), F32),
        compiler_params=_params(("arbitrary", "arbitrary"), vm),
        name="moe_experts",
    )(tile_expert, n_used, xg, gain.reshape(1, d), w_gate, w_up, w_down)


def _rope_tables(positions, dim, reps):
    inv_freq = ROPE_THETA ** (-jnp.arange(0, dim, 2, dtype=F32) / dim)
    ang = positions.astype(F32)[:, None] * inv_freq
    cos, sin = jnp.cos(ang), jnp.sin(ang)
    return jnp.tile(cos, (1, 2 * reps)), jnp.tile(jnp.concatenate([-sin, sin], axis=1), (1, reps))


def _split_w_in(w):
    d = w.shape[0]
    o_a = 4 * DN_WIDTH
    o_qsa = o_a + 2 * DN_HEADS
    o_kix = o_qsa + 3 * SA_WIDTH + IDX_HEADS * IDX_HEAD_DIM
    o_wix = o_kix + IDX_HEAD_DIM
    o_g = o_wix + IDX_HEADS
    main = jnp.concatenate([w[:, :o_a], w[:, o_qsa:o_kix], w[:, o_g:]], axis=1)
    small = jnp.concatenate([w[:, o_kix:o_wix], w[:, o_a:o_qsa], w[:, o_wix:o_g],
                             jnp.zeros((d, LANES - IDX_HEAD_DIM - 2 * DN_HEADS - IDX_HEADS), w.dtype)], axis=1)
    return main.astype(BF16), small.astype(BF16)


def _pad_lanes(v, offset):
    return jnp.zeros((1, LANES), F32).at[0, offset:offset + v.shape[0]].set(v.astype(F32))


def _mixer(x, cos_sa, sin_sa, cos_ix, sin_ix, norm_gain, w_in, conv_w, a_log, dt_bias, dn_norm,
           w_dn_out, w_sa_out, w_o):
    s = x.shape[0]
    w_main, w_small = _split_w_in(w_in)
    h = rmsnorm(x, norm_gain, BF16)
    proj = matmul(h, w_main, F32, tm=min(1024, s), tn=512, name="in_proj")
    small = matmul(h, w_small, F32, tm=min(1024, s), tn=LANES, name="in_proj_small")

    qd, kd, w, u, att, gl = dn_chunk(proj, small, conv_w.astype(F32), _pad_lanes(a_log, SM_A), _pad_lanes(dt_bias, SM_A))
    o_dn = dn_scan(qd, kd, w, u, att, gl, proj, dn_norm.astype(F32))

    q_sa, k_sa, v_sa = rope_sa(proj, cos_sa, sin_sa)
    q_ix, k_lo, k_hi = rope_ix(proj, small, cos_ix, sin_ix)
    t_attn = min(512, s)
    mask_t = index_mask(q_ix, k_lo, k_hi, small, min(TOPK_MAX, s // 4), tq=min(256, s), tk=t_attn, t_attn=t_attn)
    o_sa = masked_attention(q_sa, k_sa, v_sa, mask_t, t=t_attn)

    merged = merge_branches(o_dn, o_sa, w_dn_out.astype(BF16), w_sa_out.astype(BF16), proj)
    return matmul(merged, w_o.astype(BF16), F32, tm=512, tn=512, residual=x, name="out_proj")


def _moe_layer(x, h, norm_gain, w_router, w_gate, w_up, w_down):
    s, d = x.shape
    tile = min(512, s)
    n_tiles = (2 * s) // tile + N_EXPERTS
    wr = jnp.zeros((d, LANES), BF16).at[:, :N_EXPERTS].set(w_router.astype(BF16))
    w, m1, m2, rank, totals = route_tokens(h, wr)
    pw, meta = dispatch_plan(w, m1, m2, rank, totals, tile)
    pos1 = pw[:, 0].astype(I32)
    pos2 = pw[:, 1].astype(I32)
    tile_expert = meta[0, :n_tiles].astype(I32)
    n_used = meta[1, :1].astype(I32)
    xg = dispatch_rows(x, pos1, pos2, n_tiles * tile)
    y = moe_experts(xg, norm_gain.astype(F32), tile_expert, n_used, w_gate.astype(BF16), w_up.astype(BF16),
                    w_down.astype(BF16), tm=tile)
    return combine_rows(x, pw, y, pos1, pos2)


def kernel(x, positions, norm_mix, w_in, conv_w, a_log, dt_bias, dn_norm, w_dn_out, w_sa_out, w_o, norm_ffn, dense_w_gate, dense_w_up, dense_w_down, moe_router, moe_w_gate, moe_w_up, moe_w_down, final_norm):
    b, s, d = x.shape
    depth = norm_mix.shape[0]
    outs = []
    for bi in range(b):
        xb = x[bi]
        pos = positions[bi]
        cos_sa, sin_sa = _rope_tables(pos, SA_HEAD_DIM, 1)
        cos_ix, sin_ix = _rope_tables(pos, IDX_HEAD_DIM, 2)
        for layer in range(depth):
            xb = _mixer(xb, cos_sa, sin_sa, cos_ix, sin_ix, norm_mix[layer], w_in[layer], conv_w[layer],
                        a_log[layer], dt_bias[layer], dn_norm[layer], w_dn_out[layer], w_sa_out[layer], w_o[layer])
            h = rmsnorm(xb, norm_ffn[layer], BF16)
            j = layer // 2
            if layer % 2 == 0:
                xb = ffn_dense(h, xb, dense_w_gate[j].astype(BF16), dense_w_up[j].astype(BF16),
                               dense_w_down[j].astype(BF16))
            else:
                xb = _moe_layer(xb, h, norm_ffn[layer], moe_router[j], moe_w_gate[j], moe_w_up[j], moe_w_down[j])
        outs.append(rmsnorm(xb, final_norm, x.dtype))
    return jnp.stack(outs, axis=0)
```

```python
import functools

import jax
import jax.numpy as jnp
from jax import lax
from jax.experimental import pallas as pl
from jax.experimental.pallas import tpu as pltpu

F32 = jnp.float32
BF16 = jnp.bfloat16
I32 = jnp.int32
I16 = jnp.int16

RMS_EPS = 1e-6
L2_EPS = 1e-6
DN_HEADS = 8
DN_HEAD_DIM = 128
DN_WIDTH = DN_HEADS * DN_HEAD_DIM
CONV_WIDTH = 4
DN_CHUNK = 64
SA_HEADS = 8
SA_HEAD_DIM = 128
SA_WIDTH = SA_HEADS * SA_HEAD_DIM
IDX_HEADS = 16
IDX_HEAD_DIM = 64
TOPK_MAX = 256
ROPE_THETA = 10000.0
N_EXPERTS = 8

LANES = 128
SUBLANES = 8
VMEM_CAP_BYTES = 56 * 2**20
NEG_BIG = -1e30
I16_MIN = -2**15
I16_ROWS = 16
LOG2_E = 1.4426950408889634

COL_QKV = 0
COL_Z = 3072
COL_QSA = 4096
COL_KSA = 5120
COL_VSA = 6144
COL_QIX = 7168
COL_GDN = 8192
COL_GSA = 10240
MAIN_WIDTH = 12288
SM_KIX = 0
SM_A = 64
SM_B = 72
SM_WIX = 80


def _params(semantics, vmem_bytes):
    return pltpu.CompilerParams(dimension_semantics=semantics,
                                vmem_limit_bytes=int(min(max(vmem_bytes, 16 * 2**20), VMEM_CAP_BYTES)))


def _nbytes(shape, dtype):
    n = 1
    for s in shape:
        n *= s
    return n * jnp.dtype(dtype).itemsize


def _sigmoid(x):
    return 1.0 / (1.0 + jnp.exp(-x))


def _dot(a, b):
    return jnp.dot(a, b, preferred_element_type=F32)


def _dot_nt(a, b):
    return lax.dot_general(a, b, (((1,), (1,)), ((), ())), preferred_element_type=F32)


def _dot_tn(a, b):
    return lax.dot_general(a, b, (((0,), (0,)), ((), ())), preferred_element_type=F32)


def _split_bf16(x):
    hi = x.astype(BF16)
    return hi, (x - hi.astype(F32)).astype(BF16)


def _dot3(a, b):
    return _dot(a[0], b[0]) + _dot(a[1], b[0]) + _dot(a[0], b[1])


def _rmsnorm_body(x_ref, g_ref, o_ref):
    x = x_ref[...]
    ms = jnp.mean(x * x, axis=-1, keepdims=True)
    o_ref[...] = (x * lax.rsqrt(ms + RMS_EPS) * g_ref[...]).astype(o_ref.dtype)


def rmsnorm(x, gain, out_dtype, tm=512):
    m, d = x.shape
    vm = 2 * (_nbytes((tm, d), F32) + _nbytes((tm, d), out_dtype)) + 4 * _nbytes((tm, d), F32)
    return pl.pallas_call(
        _rmsnorm_body,
        grid=(m // tm,),
        in_specs=[pl.BlockSpec((tm, d), lambda i: (i, 0)),
                  pl.BlockSpec((1, d), lambda i: (0, 0))],
        out_specs=pl.BlockSpec((tm, d), lambda i: (i, 0)),
        out_shape=jax.ShapeDtypeStruct((m, d), out_dtype),
        compiler_params=_params(("parallel",), vm),
        name="rmsnorm",
    )(x, gain.reshape(1, d))


def _mm_body(a_ref, b_ref, o_ref):
    o_ref[...] = _dot(a_ref[...], b_ref[...]).astype(o_ref.dtype)


def _mm_res_body(a_ref, b_ref, r_ref, o_ref):
    o_ref[...] = (r_ref[...] + _dot(a_ref[...], b_ref[...])).astype(o_ref.dtype)


def matmul(a, b, out_dtype, tm, tn, residual=None, name="matmul"):
    m, k = a.shape
    n = b.shape[1]
    in_specs = [pl.BlockSpec((tm, k), lambda i, j: (i, 0)),
                pl.BlockSpec((k, tn), lambda i, j: (0, j))]
    args = [a, b]
    body = _mm_body
    vm = 2 * (_nbytes((tm, k), a.dtype) + _nbytes((k, tn), b.dtype) + _nbytes((tm, tn), out_dtype))
    vm += 2 * _nbytes((tm, tn), F32)
    if residual is not None:
        in_specs.append(pl.BlockSpec((tm, tn), lambda i, j: (i, j)))
        args.append(residual)
        body = _mm_res_body
        vm += 2 * _nbytes((tm, tn), residual.dtype)
    return pl.pallas_call(
        body,
        grid=(m // tm, n // tn),
        in_specs=in_specs,
        out_specs=pl.BlockSpec((tm, tn), lambda i, j: (i, j)),
        out_shape=jax.ShapeDtypeStruct((m, n), out_dtype),
        compiler_params=_params(("parallel", "parallel"), vm),
        name=name,
    )(*args)


def _dn_chunk_body(xc_ref, xp_ref, sm_ref, cw_ref, alog_ref, dtb_ref,
                   qd_ref, kd_ref, w_ref, u_ref, att_ref, gl_ref, ext_ref):
    c = pl.program_id(0)
    C = DN_CHUNK
    halo = SUBLANES
    ext_ref[0:halo, :] = jnp.where(c > 0, xp_ref[...], 0.0)
    ext_ref[halo:halo + C, :] = xc_ref[...]
    cw = cw_ref[...]
    y = cw[0:1, :] * ext_ref[pl.ds(halo - CONV_WIDTH + 1, C), :]
    for j in range(1, CONV_WIDTH):
        y = y + cw[j:j + 1, :] * ext_ref[pl.ds(halo - CONV_WIDTH + 1 + j, C), :]
    y = y * _sigmoid(y)

    sm = sm_ref[...]
    xa = sm + dtb_ref[...]
    softplus = jnp.maximum(xa, 0.0) + jnp.log1p(jnp.exp(-jnp.abs(xa)))
    g = -jnp.exp(alog_ref[...]) * softplus
    beta = _sigmoid(sm)

    row = lax.broadcasted_iota(I32, (C, LANES), 0)
    gc = g
    d = 1
    while d < C:
        gc = gc + jnp.where(row >= d, pltpu.roll(gc, d, axis=0), 0.0)
        d *= 2
    gct = jnp.concatenate([gc, jnp.zeros_like(gc)], axis=0).T
    ex = jnp.exp(gc)
    gc_last = gc[C - 1:C, :]
    exl = jnp.exp(gc_last - gc)
    gl_ref[...] = jnp.exp(jnp.broadcast_to(gct[SM_A:SM_A + DN_HEADS, C - 1:C], (DN_HEADS, LANES)))

    ri = lax.broadcasted_iota(I32, (C, C), 0)
    ci = lax.broadcasted_iota(I32, (C, C), 1)
    tril = ri >= ci
    strict = ri > ci
    eye = jnp.where(ri == ci, 1.0, 0.0).astype(F32)
    lvl_masks = []
    lb = 0
    while (1 << lb) < C:
        lvl_masks.append(((ri >> (lb + 1)) == (ci >> (lb + 1)))
                         & (((ri >> lb) & 1) == 1) & (((ci >> lb) & 1) == 0))
        lb += 1

    heads = range(DN_HEADS)
    a_mats, kbs, vbs, excols = [], [], [], []
    for h in heads:
        sl = slice(h * DN_HEAD_DIM, (h + 1) * DN_HEAD_DIM)
        qh = y[:, h * DN_HEAD_DIM:(h + 1) * DN_HEAD_DIM]
        kh = y[:, DN_WIDTH + h * DN_HEAD_DIM:DN_WIDTH + (h + 1) * DN_HEAD_DIM]
        vh = y[:, 2 * DN_WIDTH + h * DN_HEAD_DIM:2 * DN_WIDTH + (h + 1) * DN_HEAD_DIM]
        qn = qh * lax.rsqrt(jnp.sum(qh * qh, axis=-1, keepdims=True) + L2_EPS) * (DN_HEAD_DIM ** -0.5)
        kn = kh * lax.rsqrt(jnp.sum(kh * kh, axis=-1, keepdims=True) + L2_EPS)
        bcol = beta[:, SM_B + h:SM_B + h + 1]
        gcol = gc[:, SM_A + h:SM_A + h + 1]
        grow = gct[SM_A + h:SM_A + h + 1, 0:C]
        dec = jnp.exp(jnp.where(tril, gcol - grow, -jnp.inf))
        kb = kn * bcol
        knb = kn.astype(BF16)
        excol = ex[:, SM_A + h:SM_A + h + 1]
        a_mats.append(jnp.where(strict, _dot_nt(kb.astype(BF16), knb) * dec, 0.0))
        kbs.append(kb * excol)
        vbs.append(vh * bcol)
        att = jnp.where(tril, _dot_nt(qn.astype(BF16), knb) * dec, 0.0)
        att_ref[:, h * C:(h + 1) * C] = att.astype(att_ref.dtype)
        qd_ref[:, sl] = (qn * excol).astype(qd_ref.dtype)
        kd_ref[:, sl] = (kn * exl[:, SM_A + h:SM_A + h + 1]).astype(kd_ref.dtype)

    x_inv = [eye - jnp.where(lvl_masks[0], a, 0.0) for a in a_mats]
    for lm in lvl_masks[1:]:
        xs = [_split_bf16(x) for x in x_inv]
        ts = [_dot3(xs[h], _split_bf16(jnp.where(lm, a_mats[h], 0.0))) for h in heads]
        x_inv = [x_inv[h] - _dot3(_split_bf16(ts[h]), xs[h]) for h in heads]
    xs = [_split_bf16(x) for x in x_inv]
    for h in heads:
        sl = slice(h * DN_HEAD_DIM, (h + 1) * DN_HEAD_DIM)
        w_ref[:, sl] = _dot3(xs[h], _split_bf16(kbs[h])).astype(w_ref.dtype)
        u_ref[:, sl] = _dot3(xs[h], _split_bf16(vbs[h]))


def dn_chunk(proj, small, conv_w, alog_pad, dtb_pad):
    s = proj.shape[0]
    C = DN_CHUNK
    n_chunks = s // C
    w3 = 3 * DN_WIDTH
    row_spec = lambda width, dt: pl.BlockSpec((C, width), lambda c: (c, 0))
    vm = 2 * (_nbytes((C, w3), F32) + _nbytes((SUBLANES, w3), F32)) + 8 * _nbytes((C, w3), F32)
    return pl.pallas_call(
        _dn_chunk_body,
        grid=(n_chunks,),
        in_specs=[pl.BlockSpec((C, w3), lambda c: (c, COL_QKV // w3)),
                  pl.BlockSpec((SUBLANES, w3), lambda c: (jnp.maximum(c * (C // SUBLANES) - 1, 0), COL_QKV // w3)),
                  pl.BlockSpec((C, LANES), lambda c: (c, 0)),
                  pl.BlockSpec((CONV_WIDTH, w3), lambda c: (0, 0)),
                  pl.BlockSpec((1, LANES), lambda c: (0, 0)),
                  pl.BlockSpec((1, LANES), lambda c: (0, 0))],
        out_specs=[row_spec(DN_WIDTH, BF16), row_spec(DN_WIDTH, BF16), row_spec(DN_WIDTH, BF16),
                   row_spec(DN_WIDTH, F32), row_spec(DN_HEADS * C, BF16),
                   pl.BlockSpec((DN_HEADS, LANES), lambda c: (c, 0))],
        out_shape=[jax.ShapeDtypeStruct((s, DN_WIDTH), BF16),
                   jax.ShapeDtypeStruct((s, DN_WIDTH), BF16),
                   jax.ShapeDtypeStruct((s, DN_WIDTH), BF16),
                   jax.ShapeDtypeStruct((s, DN_WIDTH), F32),
                   jax.ShapeDtypeStruct((s, DN_HEADS * C), BF16),
                   jax.ShapeDtypeStruct((n_chunks * DN_HEADS, LANES), F32)],
        scratch_shapes=[pltpu.VMEM((SUBLANES + C, w3), F32)],
        compiler_params=_params(("parallel",), vm),
        name="dn_chunk",
    )(proj, proj, small, conv_w, alog_pad, dtb_pad)


def _dn_scan_body(qd_ref, kd_ref, w_ref, u_ref, att_ref, gl_ref, z_ref, nrm_ref, o_ref, st_ref):
    c = pl.program_id(0)
    C = DN_CHUNK

    @pl.when(c == 0)
    def _():
        st_ref[...] = jnp.zeros_like(st_ref)

    heads = range(DN_HEADS)
    sls = [slice(h * DN_HEAD_DIM, (h + 1) * DN_HEAD_DIM) for h in heads]
    states = [st_ref[h] for h in heads]
    sbs = [s.astype(BF16) for s in states]
    w_s = [_dot(w_ref[:, sls[h]], sbs[h]) for h in heads]
    q_s = [_dot(qd_ref[:, sls[h]], sbs[h]) for h in heads]
    vbs = [(u_ref[:, sls[h]] - w_s[h]).astype(BF16) for h in heads]
    outs = [q_s[h] + _dot(att_ref[:, h * C:(h + 1) * C], vbs[h]) for h in heads]
    for h in heads:
        st_ref[h] = states[h] * gl_ref[h:h + 1, :] + _dot_tn(kd_ref[:, sls[h]], vbs[h])
    for h in heads:
        o = outs[h]
        ms = jnp.mean(o * o, axis=-1, keepdims=True)
        z = z_ref[:, sls[h]]
        o_ref[:, sls[h]] = (o * lax.rsqrt(ms + RMS_EPS) * nrm_ref[...] * (z * _sigmoid(z))).astype(o_ref.dtype)


def dn_scan(qd, kd, w, u, att, gl, proj, dn_norm):
    s = qd.shape[0]
    C = DN_CHUNK
    row = lambda width: pl.BlockSpec((C, width), lambda c: (c, 0))
    vm = 2 * 6 * _nbytes((C, DN_WIDTH), F32) + 2 * _nbytes((DN_HEADS, DN_HEAD_DIM, DN_HEAD_DIM), F32)
    return pl.pallas_call(
        _dn_scan_body,
        grid=(s // C,),
        in_specs=[row(DN_WIDTH), row(DN_WIDTH), row(DN_WIDTH), row(DN_WIDTH), row(DN_HEADS * C),
                  pl.BlockSpec((DN_HEADS, LANES), lambda c: (c, 0)),
                  pl.BlockSpec((C, DN_WIDTH), lambda c: (c, COL_Z // DN_WIDTH)),
                  pl.BlockSpec((1, DN_HEAD_DIM), lambda c: (0, 0))],
        out_specs=row(DN_WIDTH),
        out_shape=jax.ShapeDtypeStruct((s, DN_WIDTH), BF16),
        scratch_shapes=[pltpu.VMEM((DN_HEADS, DN_HEAD_DIM, DN_HEAD_DIM), F32)],
        compiler_params=_params(("arbitrary",), vm),
        name="dn_scan",
    )(qd, kd, w, u, att, gl, proj, dn_norm.reshape(1, DN_HEAD_DIM))


def _rope_sa_body(q_ref, k_ref, v_ref, c_ref, s_ref, qo_ref, ko_ref, vo_ref):
    cs = c_ref[...]
    sn = s_ref[...]
    scale = SA_HEAD_DIM ** -0.5 * LOG2_E
    for h in range(SA_HEADS):
        sl = slice(h * SA_HEAD_DIM, (h + 1) * SA_HEAD_DIM)
        x = q_ref[:, sl]
        qo_ref[:, sl] = ((x * cs + pltpu.roll(x, SA_HEAD_DIM // 2, axis=1) * sn) * scale).astype(qo_ref.dtype)
        x = k_ref[:, sl]
        ko_ref[:, sl] = (x * cs + pltpu.roll(x, SA_HEAD_DIM // 2, axis=1) * sn).astype(ko_ref.dtype)
    vo_ref[...] = v_ref[...].astype(vo_ref.dtype)


def rope_sa(proj, cos_t, sin_t, tm=512):
    s = proj.shape[0]
    col = lambda off: pl.BlockSpec((tm, SA_WIDTH), lambda i: (i, off // SA_WIDTH))
    tab = pl.BlockSpec((tm, LANES), lambda i: (i, 0))
    out = pl.BlockSpec((tm, SA_WIDTH), lambda i: (i, 0))
    vm = 2 * 3 * (_nbytes((tm, SA_WIDTH), F32) + _nbytes((tm, SA_WIDTH), BF16)) + 4 * _nbytes((tm, SA_WIDTH), F32)
    return pl.pallas_call(
        _rope_sa_body,
        grid=(s // tm,),
        in_specs=[col(COL_QSA), col(COL_KSA), col(COL_VSA), tab, tab],
        out_specs=[out, out, out],
        out_shape=[jax.ShapeDtypeStruct((s, SA_WIDTH), BF16)] * 3,
        compiler_params=_params(("parallel",), vm),
        name="rope_sa",
    )(proj, proj, proj, cos_t, sin_t)


def _rope_ix_body(q_ref, sm_ref, c_ref, s_ref, qo_ref, klo_ref, khi_ref):
    cs = c_ref[...]
    sn = s_ref[...]
    lane = lax.broadcasted_iota(I32, cs.shape, 1)
    first = (lane & (IDX_HEAD_DIM - 1)) < IDX_HEAD_DIM // 2
    half = IDX_HEAD_DIM // 2

    def rot(x):
        swapped = jnp.where(first, pltpu.roll(x, LANES - half, axis=1), pltpu.roll(x, half, axis=1))
        return x * cs + swapped * sn

    for j in range(IDX_HEADS * IDX_HEAD_DIM // LANES):
        sl = slice(j * LANES, (j + 1) * LANES)
        qo_ref[:, sl] = rot(q_ref[:, sl]).astype(qo_ref.dtype)
    k_lo = jnp.where(lane < IDX_HEAD_DIM, rot(sm_ref[...]), 0.0)
    klo_ref[...] = k_lo.astype(klo_ref.dtype)
    khi_ref[...] = pltpu.roll(k_lo, IDX_HEAD_DIM, axis=1).astype(khi_ref.dtype)


def rope_ix(proj, small, cos_t, sin_t, tm=512):
    s = proj.shape[0]
    wq = IDX_HEADS * IDX_HEAD_DIM
    tab = pl.BlockSpec((tm, LANES), lambda i: (i, 0))
    vm = 2 * (_nbytes((tm, wq), F32) + _nbytes((tm, wq), BF16)) + 4 * _nbytes((tm, wq), F32)
    return pl.pallas_call(
        _rope_ix_body,
        grid=(s // tm,),
        in_specs=[pl.BlockSpec((tm, wq), lambda i: (i, COL_QIX // wq)), tab, tab, tab],
        out_specs=[pl.BlockSpec((tm, wq), lambda i: (i, 0)), tab, tab],
        out_shape=[jax.ShapeDtypeStruct((s, wq), BF16),
                   jax.ShapeDtypeStruct((s, LANES), BF16),
                   jax.ShapeDtypeStruct((s, LANES), BF16)],
        compiler_params=_params(("parallel",), vm),
        name="rope_ix",
    )(proj, small, cos_t, sin_t)


def _index_body(q_ref, klo_ref, khi_ref, sm_ref, mask_ref, keys_ref, hi_ref, lo_ref, *, tq, tk, topk):
    i = pl.program_id(0)
    nkt = keys_ref.shape[0]
    nk = ((i + 1) * tq + tk - 1) // tk
    w_t = (sm_ref[...] * (IDX_HEADS ** -0.5 * IDX_HEAD_DIM ** -0.5)).T
    key_l = lax.broadcasted_iota(I32, (tk, tq), 0)
    qry_g = i * tq + lax.broadcasted_iota(I32, (tk, tq), 1)

    def score_tile(kt, carry):
        off = pl.multiple_of(kt * tk, tk)
        k_lo = klo_ref[pl.ds(off, tk), :]
        k_hi = khi_ref[pl.ds(off, tk), :]
        acc = jnp.zeros((tk, tq), F32)
        for j in range(IDX_HEADS // 2):
            qp = q_ref[:, j * LANES:(j + 1) * LANES]
            acc = acc + w_t[SM_WIX + 2 * j:SM_WIX + 2 * j + 1, :] * jnp.maximum(_dot_nt(k_lo, qp), 0.0)
            acc = acc + w_t[SM_WIX + 2 * j + 1:SM_WIX + 2 * j + 2, :] * jnp.maximum(_dot_nt(k_hi, qp), 0.0)
        sc = jnp.where(kt * tk + key_l <= qry_g, acc, -jnp.inf)
        bits = pltpu.bitcast(sc, I32)
        key = bits ^ ((bits >> 31) & 0x7FFFFFFF)
        keys_ref[kt] = key
        hi_ref[kt] = (key >> 16).astype(I16)
        return carry

    lax.fori_loop(0, nk, score_tile, 0)

    cnt_rows = 2 * I16_ROWS

    def count16(ref, cand, strict):
        def body(kt, cnt):
            t = ref[kt]
            hit = jnp.where((t > cand) if strict else (t >= cand), jnp.int16(1), jnp.int16(0))
            for j in range(tk // cnt_rows):
                cnt = cnt + hit[j * cnt_rows:(j + 1) * cnt_rows]
            return cnt

        cnt = lax.fori_loop(0, nk, body, jnp.zeros((cnt_rows, tq), I16))
        return jnp.sum(cnt.astype(F32), axis=0, keepdims=True)

    def kth_largest16(ref, kth):
        zero = jnp.zeros((1, tq), I32)
        ans = jnp.where(count16(ref, zero.astype(I16), False) >= kth, zero, I16_MIN)

        def bit_body(b, ans):
            cand = ans + lax.shift_left(jnp.int32(1), 14 - b)
            return jnp.where(count16(ref, cand.astype(I16), False) >= kth, cand, ans)

        return lax.fori_loop(0, 15, bit_body, ans)

    kf = jnp.full((1, tq), float(topk), F32)
    t_hi = kth_largest16(hi_ref, kf)
    above = count16(hi_ref, t_hi.astype(I16), True)

    def low_tile(kt, carry):
        key = keys_ref[kt]
        lo = (key & 0xFFFF) + I16_MIN
        lo_ref[kt] = jnp.where((key >> 16) == t_hi, lo, I16_MIN).astype(I16)
        return carry

    lax.fori_loop(0, nk, low_tile, 0)
    t_lo = kth_largest16(lo_ref, kf - above)
    ans = t_hi * 65536 + (t_lo - I16_MIN)

    def write(kt, carry):
        sel = (keys_ref[kt] >= ans) & (kt * tk + key_l <= qry_g)
        mask_ref[0, pl.ds(pl.multiple_of(kt * tk, tk), tk), :] = jnp.where(sel, 1.0, 0.0).astype(mask_ref.dtype)
        return carry

    lax.fori_loop(0, nk, write, 0)

    def clear(kt, carry):
        mask_ref[0, pl.ds(pl.multiple_of(kt * tk, tk), tk), :] = jnp.zeros((tk, tq), mask_ref.dtype)
        return carry

    lax.fori_loop(nk, nkt, clear, 0)


def index_mask(q_ix, k_lo, k_hi, small, topk, tq, tk, t_attn):
    s = q_ix.shape[0]
    nkt = s // tk
    per = t_attn // tq
    wq = IDX_HEADS * IDX_HEAD_DIM
    vm = (2 * (_nbytes((tq, wq), BF16) + 2 * _nbytes((s, LANES), BF16) + _nbytes((tq, LANES), F32)
               + _nbytes((s, tq), BF16)) + 2 * _nbytes((s, tq), I32) + 8 * _nbytes((tk, tq), F32))
    return pl.pallas_call(
        functools.partial(_index_body, tq=tq, tk=tk, topk=topk),
        grid=(s // tq,),
        in_specs=[pl.BlockSpec((tq, wq), lambda i: (i, 0)),
                  pl.BlockSpec((s, LANES), lambda i: (0, 0)),
                  pl.BlockSpec((s, LANES), lambda i: (0, 0)),
                  pl.BlockSpec((tq, LANES), lambda i: (i, 0))],
        out_specs=pl.BlockSpec((1, s, tq), lambda i: (i // per, 0, i % per)),
        out_shape=jax.ShapeDtypeStruct((s // t_attn, s, t_attn), BF16),
        scratch_shapes=[pltpu.VMEM((nkt, tk, tq), I32), pltpu.VMEM((nkt, tk, tq), I16), pltpu.VMEM((nkt, tk, tq), I16)],
        compiler_params=_params(("parallel",), vm),
        name="index_mask",
    )(q_ix, k_lo, k_hi, small)


def _attn_body(qi_ref, ki_ref, q_ref, k_ref, v_ref, mk_ref, o_ref, m_ref, l_ref, acc_ref):
    p = pl.program_id(0)
    qi = qi_ref[p]
    ki = ki_ref[p]

    @pl.when(ki == 0)
    def _():
        m_ref[...] = jnp.full_like(m_ref, NEG_BIG)
        l_ref[...] = jnp.zeros_like(l_ref)
        acc_ref[...] = jnp.zeros_like(acc_ref)

    keep = mk_ref[0].astype(F32) > 0.5
    for h in range(SA_HEADS):
        sl = slice(h * SA_HEAD_DIM, (h + 1) * SA_HEAD_DIM)
        s = jnp.where(keep, _dot_nt(k_ref[:, sl], q_ref[:, sl]), NEG_BIG)
        m_old = m_ref[h:h + 1, :]
        m_new = jnp.maximum(m_old, jnp.max(s, axis=0, keepdims=True))
        alpha = jnp.exp2(m_old - m_new)
        pr = jnp.exp2(s - m_new)
        l_ref[h:h + 1, :] = alpha * l_ref[h:h + 1, :] + jnp.sum(pr, axis=0, keepdims=True)
        m_ref[h:h + 1, :] = m_new
        acc_ref[h] = alpha * acc_ref[h] + _dot_tn(v_ref[:, sl], pr.astype(BF16))

    @pl.when(ki == qi)
    def _():
        for h in range(SA_HEADS):
            sl = slice(h * SA_HEAD_DIM, (h + 1) * SA_HEAD_DIM)
            o_ref[:, sl] = (acc_ref[h] / l_ref[h:h + 1, :]).T.astype(o_ref.dtype)


def masked_attention(q, k, v, mask_t, t):
    s = q.shape[0]
    nb = s // t
    pairs = [(a, b) for a in range(nb) for b in range(a + 1)]
    qi = jnp.asarray([a for a, _ in pairs], I32)
    ki = jnp.asarray([b for _, b in pairs], I32)
    vm = (2 * (4 * _nbytes((t, SA_WIDTH), BF16) + _nbytes((t, t), BF16)) + _nbytes((t, SA_WIDTH), F32)
          + 2 * _nbytes((SA_HEADS, t), F32) + 8 * _nbytes((t, t), F32))
    grid_spec = pltpu.PrefetchScalarGridSpec(
        num_scalar_prefetch=2,
        grid=(len(pairs),),
        in_specs=[pl.BlockSpec((t, SA_WIDTH), lambda p, qi, ki: (qi[p], 0)),
                  pl.BlockSpec((t, SA_WIDTH), lambda p, qi, ki: (ki[p], 0)),
                  pl.BlockSpec((t, SA_WIDTH), lambda p, qi, ki: (ki[p], 0)),
                  pl.BlockSpec((1, t, t), lambda p, qi, ki: (qi[p], ki[p], 0))],
        out_specs=pl.BlockSpec((t, SA_WIDTH), lambda p, qi, ki: (qi[p], 0)),
        scratch_shapes=[pltpu.VMEM((SA_HEADS, t), F32),
                        pltpu.VMEM((SA_HEADS, t), F32),
                        pltpu.VMEM((SA_HEADS, SA_HEAD_DIM, t), F32)],
    )
    return pl.pallas_call(
        _attn_body,
        grid_spec=grid_spec,
        out_shape=jax.ShapeDtypeStruct((s, SA_WIDTH), BF16),
        compiler_params=_params(("arbitrary",), vm),
        name="masked_attention",
    )(qi, ki, q, k, v, mask_t)


def _merge_body(odn_ref, osa_ref, wdn_ref, wsa_ref, gdn_ref, gsa_ref, o_ref):
    y_dn = _dot(odn_ref[...], wdn_ref[...])
    y_sa = _dot(osa_ref[...], wsa_ref[...])
    o_ref[...] = (_sigmoid(gdn_ref[...]) * y_dn + _sigmoid(gsa_ref[...]) * y_sa).astype(o_ref.dtype)


def merge_branches(o_dn, o_sa, w_dn, w_sa, proj, tm=512, tn=512):
    s, kd = o_dn.shape
    d = w_dn.shape[1]
    vm = 2 * (2 * _nbytes((tm, kd), BF16) + 2 * _nbytes((kd, tn), BF16) + 2 * _nbytes((tm, tn), F32)
              + _nbytes((tm, tn), BF16)) + 6 * _nbytes((tm, tn), F32)
    return pl.pallas_call(
        _merge_body,
        grid=(s // tm, d // tn),
        in_specs=[pl.BlockSpec((tm, kd), lambda i, j: (i, 0)),
                  pl.BlockSpec((tm, kd), lambda i, j: (i, 0)),
                  pl.BlockSpec((kd, tn), lambda i, j: (0, j)),
                  pl.BlockSpec((kd, tn), lambda i, j: (0, j)),
                  pl.BlockSpec((tm, tn), lambda i, j: (i, COL_GDN // tn + j)),
                  pl.BlockSpec((tm, tn), lambda i, j: (i, COL_GSA // tn + j))],
        out_specs=pl.BlockSpec((tm, tn), lambda i, j: (i, j)),
        out_shape=jax.ShapeDtypeStruct((s, d), BF16),
        compiler_params=_params(("parallel", "parallel"), vm),
        name="merge_branches",
    )(o_dn, o_sa, w_dn, w_sa, proj, proj)


def _ffn_body(h_ref, x_ref, wg_ref, wu_ref, wd_ref, o_ref, acc_ref):
    f = pl.program_id(1)

    @pl.when(f == 0)
    def _():
        acc_ref[...] = jnp.zeros_like(acc_ref)

    h = h_ref[...]
    g = _dot(h, wg_ref[...])
    u = _dot(h, wu_ref[...])
    acc_ref[...] += _dot((g * _sigmoid(g) * u).astype(BF16), wd_ref[...])

    @pl.when(f == pl.num_programs(1) - 1)
    def _():
        o_ref[...] = x_ref[...] + acc_ref[...]


def ffn_dense(h, x, w_gate, w_up, w_down, tm=512, tf=512):
    s, d = h.shape
    ff = w_gate.shape[1]
    vm = (2 * (_nbytes((tm, d), BF16) + 2 * _nbytes((tm, d), F32) + 3 * _nbytes((d, tf), BF16))
          + _nbytes((tm, d), F32) + 6 * _nbytes((tm, tf), F32))
    return pl.pallas_call(
        _ffn_body,
        grid=(s // tm, ff // tf),
        in_specs=[pl.BlockSpec((tm, d), lambda i, f: (i, 0)),
                  pl.BlockSpec((tm, d), lambda i, f: (i, 0)),
                  pl.BlockSpec((d, tf), lambda i, f: (0, f)),
                  pl.BlockSpec((d, tf), lambda i, f: (0, f)),
                  pl.BlockSpec((tf, d), lambda i, f: (f, 0))],
        out_specs=pl.BlockSpec((tm, d), lambda i, f: (i, 0)),
        out_shape=jax.ShapeDtypeStruct((s, d), F32),
        scratch_shapes=[pltpu.VMEM((tm, d), F32)],
        compiler_params=_params(("parallel", "arbitrary"), vm),
        name="ffn_dense",
    )(h, x, w_gate, w_up, w_down)


def _router_body(h_ref, wr_ref, w_ref, m1_ref, m2_ref, rank_ref, tot_ref, cnt_ref):
    logits = _dot(h_ref[...], wr_ref[...])
    lane = lax.broadcasted_iota(I32, logits.shape, 1)
    lg = jnp.where(lane < N_EXPERTS, logits, -jnp.inf)
    m1 = jnp.max(lg, axis=1, keepdims=True)
    i1 = jnp.min(jnp.where(lg == m1, lane, LANES), axis=1, keepdims=True)
    lg2 = jnp.where(lane == i1, -jnp.inf, lg)
    m2 = jnp.max(lg2, axis=1, keepdims=True)
    i2 = jnp.min(jnp.where(lg2 == m2, lane, LANES), axis=1, keepdims=True)
    e = jnp.exp(m2 - m1)
    first = lane == i1
    second = lane == i2
    w_ref[...] = jnp.where(first, 1.0 / (1.0 + e), 0.0) + jnp.where(second, e / (1.0 + e), 0.0)
    m1_ref[...] = jnp.where(first, 1.0, 0.0)
    m2_ref[...] = jnp.where(second, 1.0, 0.0)

    @pl.when(pl.program_id(0) == 0)
    def _():
        cnt_ref[...] = jnp.zeros_like(cnt_ref)

    tm = logits.shape[0]
    sel = jnp.where(first | second, 1.0, 0.0).astype(BF16)
    ri = lax.broadcasted_iota(I32, (tm, tm), 0)
    ci = lax.broadcasted_iota(I32, (tm, tm), 1)
    before = jnp.where(ri > ci, 1.0, 0.0).astype(BF16)
    run = cnt_ref[0:1, :]
    rank_ref[...] = run + _dot(before, sel)
    run = run + jnp.sum(sel.astype(F32), axis=0, keepdims=True)
    cnt_ref[...] = jnp.broadcast_to(run, cnt_ref.shape)
    tot_ref[...] = jnp.broadcast_to(run, tot_ref.shape)


def route_tokens(h, w_router_pad, tm=512):
    s, d = h.shape
    tm = min(tm, s)
    vm = 2 * (_nbytes((tm, d), BF16) + _nbytes((d, LANES), BF16) + 4 * _nbytes((tm, LANES), F32)) + 4 * _nbytes((tm, tm), F32)
    row = pl.BlockSpec((tm, LANES), lambda i: (i, 0))
    return pl.pallas_call(
        _router_body,
        grid=(s // tm,),
        in_specs=[pl.BlockSpec((tm, d), lambda i: (i, 0)),
                  pl.BlockSpec((d, LANES), lambda i: (0, 0))],
        out_specs=[row, row, row, row, pl.BlockSpec((SUBLANES, LANES), lambda i: (0, 0))],
        out_shape=[jax.ShapeDtypeStruct((s, LANES), F32)] * 4 + [jax.ShapeDtypeStruct((SUBLANES, LANES), F32)],
        scratch_shapes=[pltpu.VMEM((SUBLANES, LANES), F32)],
        compiler_params=_params(("arbitrary",), vm),
        name="moe_router",
    )(h, w_router_pad)


def _plan_body(w_ref, m1_ref, m2_ref, rank_ref, tot_ref, pw_ref, meta_ref, *, tile):
    lane8 = lax.broadcasted_iota(I32, (SUBLANES, LANES), 1)
    n = tot_ref[...]
    padded = jnp.floor((n + (tile - 1.0)) * (1.0 / tile)) * tile
    ends = padded
    d = 1
    while d < N_EXPERTS:
        ends = ends + jnp.where(lane8 >= d, pltpu.roll(ends, d, axis=1), 0.0)
        d *= 2
    start = (ends - padded)[0:1, :]
    posf = start + rank_ref[...]
    m1 = m1_ref[...]
    m2 = m2_ref[...]
    w = w_ref[...]
    lane = lax.broadcasted_iota(I32, w.shape, 1)
    cols = [jnp.sum(m1 * posf, axis=1, keepdims=True), jnp.sum(m2 * posf, axis=1, keepdims=True),
            jnp.sum(m1 * w, axis=1, keepdims=True), jnp.sum(m2 * w, axis=1, keepdims=True)]
    out = jnp.zeros_like(w)
    for j, col in enumerate(cols):
        out = jnp.where(lane == j, col, out)
    pw_ref[...] = out
    tile_start = lane8.astype(F32) * tile
    owner = jnp.zeros((SUBLANES, LANES), F32)
    for e in range(N_EXPERTS - 1):
        owner = owner + jnp.where(tile_start >= ends[:, e:e + 1], 1.0, 0.0)
    n_used = ends[:, N_EXPERTS - 1:N_EXPERTS] * (1.0 / tile)
    row8 = lax.broadcasted_iota(I32, (SUBLANES, LANES), 0)
    meta_ref[...] = jnp.where(row8 == 0, owner, jnp.broadcast_to(n_used, owner.shape))


def dispatch_plan(w, m1, m2, rank, totals, tile):
    s = w.shape[0]
    full = pl.BlockSpec((s, LANES), lambda: (0, 0))
    small = pl.BlockSpec((SUBLANES, LANES), lambda: (0, 0))
    vm = 2 * 5 * _nbytes((s, LANES), F32) + 8 * _nbytes((s, LANES), F32)
    return pl.pallas_call(
        functools.partial(_plan_body, tile=float(tile)),
        in_specs=[full, full, full, full, small],
        out_specs=[full, small],
        out_shape=[jax.ShapeDtypeStruct((s, LANES), F32), jax.ShapeDtypeStruct((SUBLANES, LANES), F32)],
        compiler_params=pltpu.CompilerParams(vmem_limit_bytes=int(min(vm, VMEM_CAP_BYTES))),
        name="moe_plan",
    )(w, m1, m2, rank, totals)


def _row_copy(src_ref, src_row, dst_ref, dst_row, sem):
    return pltpu.make_async_copy(src_ref.at[pl.ds(src_row, 1), :], dst_ref.at[pl.ds(dst_row, 1), :], sem)


def _dispatch_body(p1_ref, p2_ref, x_ref, xg_in_ref, xg_ref, sem, *, tt):
    del xg_in_ref
    base = pl.program_id(0) * tt

    def issue(r, carry):
        _row_copy(x_ref, r, xg_ref, p1_ref[base + r], sem).start()
        _row_copy(x_ref, r, xg_ref, p2_ref[base + r], sem).start()
        return carry

    lax.fori_loop(0, tt, issue, 0)

    def drain(r, carry):
        _row_copy(x_ref, 0, xg_ref, 0, sem).wait()
        _row_copy(x_ref, 0, xg_ref, 0, sem).wait()
        return carry

    lax.fori_loop(0, tt, drain, 0)


def dispatch_rows(x, pos1, pos2, n_rows, tt=256):
    s, d = x.shape
    tt = min(tt, s)
    grid_spec = pltpu.PrefetchScalarGridSpec(
        num_scalar_prefetch=2,
        grid=(s // tt,),
        in_specs=[pl.BlockSpec((tt, d), lambda i, p1, p2: (i, 0)),
                  pl.BlockSpec(memory_space=pl.ANY)],
        out_specs=pl.BlockSpec(memory_space=pl.ANY),
        scratch_shapes=[pltpu.SemaphoreType.DMA(())],
    )
    return pl.pallas_call(
        functools.partial(_dispatch_body, tt=tt),
        grid_spec=grid_spec,
        out_shape=jax.ShapeDtypeStruct((n_rows, d), x.dtype),
        input_output_aliases={3: 0},
        compiler_params=_params(("arbitrary",), 4 * _nbytes((tt, d), F32)),
        name="moe_dispatch",
    )(pos1, pos2, x, jnp.zeros((n_rows, d), x.dtype))


def _combine_body(p1_ref, p2_ref, x_ref, pw_ref, y_ref, o_ref, b1_ref, b2_ref, sem, *, tt):
    base = pl.program_id(0) * tt

    def issue(r, carry):
        _row_copy(y_ref, p1_ref[base + r], b1_ref, r, sem).start()
        _row_copy(y_ref, p2_ref[base + r], b2_ref, r, sem).start()
        return carry

    lax.fori_loop(0, tt, issue, 0)

    def drain(r, carry):
        _row_copy(y_ref, 0, b1_ref, 0, sem).wait()
        _row_copy(y_ref, 0, b2_ref, 0, sem).wait()
        return carry

    lax.fori_loop(0, tt, drain, 0)
    pw = pw_ref[...]
    o_ref[...] = x_ref[...] + pw[:, 2:3] * b1_ref[...] + pw[:, 3:4] * b2_ref[...]


def combine_rows(x, pw, y, pos1, pos2, tt=256):
    s, d = x.shape
    tt = min(tt, s)
    grid_spec = pltpu.PrefetchScalarGridSpec(
        num_scalar_prefetch=2,
        grid=(s // tt,),
        in_specs=[pl.BlockSpec((tt, d), lambda i, p1, p2: (i, 0)),
                  pl.BlockSpec((tt, LANES), lambda i, p1, p2: (i, 0)),
                  pl.BlockSpec(memory_space=pl.ANY)],
        out_specs=pl.BlockSpec((tt, d), lambda i, p1, p2: (i, 0)),
        scratch_shapes=[pltpu.VMEM((tt, d), F32), pltpu.VMEM((tt, d), F32), pltpu.SemaphoreType.DMA(())],
    )
    return pl.pallas_call(
        functools.partial(_combine_body, tt=tt),
        grid_spec=grid_spec,
        out_shape=jax.ShapeDtypeStruct((s, d), F32),
        compiler_params=_params(("arbitrary",), 8 * _nbytes((tt, d), F32)),
        name="moe_combine",
    )(pos1, pos2, x, pw, y)


def _moe_body(te_ref, nu_ref, xg_ref, gain_ref, wg_ref, wu_ref, wd_ref, o_ref, h_ref, acc_ref):
    i = pl.program_id(0)
    f = pl.program_id(1)

    @pl.when(i < nu_ref[0])
    def _():
        @pl.when(f == 0)
        def _():
            x = xg_ref[...]
            ms = jnp.mean(x * x, axis=-1, keepdims=True)
            h_ref[...] = (x * lax.rsqrt(ms + RMS_EPS) * gain_ref[...]).astype(h_ref.dtype)
            acc_ref[...] = jnp.zeros_like(acc_ref)

        h = h_ref[...]
        g = _dot(h, wg_ref[0])
        u = _dot(h, wu_ref[0])
        acc_ref[...] += _dot((g * _sigmoid(g) * u).astype(BF16), wd_ref[0])

        @pl.when(f == pl.num_programs(1) - 1)
        def _():
            o_ref[...] = acc_ref[...]

    @pl.when((i >= nu_ref[0]) & (f == 0))
    def _():
        o_ref[...] = jnp.zeros_like(o_ref)


def moe_experts(xg, gain, tile_expert, n_used, w_gate, w_up, w_down, tm, tf=256):
    p, d = xg.shape
    ff = w_gate.shape[2]
    nf = ff // tf
    vm = (2 * (2 * _nbytes((tm, d), F32) + 3 * _nbytes((d, tf), BF16)) + _nbytes((tm, d), BF16)
          + _nbytes((tm, d), F32) + 6 * _nbytes((tm, tf), F32))

    def row(i, f, te, nu):
        return (jnp.minimum(i, nu[0] - 1), 0)

    def fcol(i, f, nu):
        return jnp.where(i < nu[0], f, nf - 1)

    grid_spec = pltpu.PrefetchScalarGridSpec(
        num_scalar_prefetch=2,
        grid=(p // tm, nf),
        in_specs=[pl.BlockSpec((tm, d), row),
                  pl.BlockSpec((1, d), lambda i, f, te, nu: (0, 0)),
                  pl.BlockSpec((1, d, tf), lambda i, f, te, nu: (te[i], 0, fcol(i, f, nu))),
                  pl.BlockSpec((1, d, tf), lambda i, f, te, nu: (te[i], 0, fcol(i, f, nu))),
                  pl.BlockSpec((1, tf, d), lambda i, f, te, nu: (te[i], fcol(i, f, nu), 0))],
        out_specs=pl.BlockSpec((tm, d), lambda i, f, te, nu: (i, 0)),
        scratch_shapes=[pltpu.VMEM((tm, d), BF16), pltpu.VMEM((tm, d), F32)],
    )
    return pl.pallas_call(
        _moe_body,
        grid_spec=grid_spec,
        out_shape=jax.ShapeDtypeStruct((p, d), F32),
        compiler_params=_params(("arbitrary", "arbitrary"), vm),
        name="moe_experts",
    )(tile_expert, n_used, xg, gain.reshape(1, d), w_gate, w_up, w_down)


def _rope_tables(positions, dim, reps):
    inv_freq = ROPE_THETA ** (-jnp.arange(0, dim, 2, dtype=F32) / dim)
    ang = positions.astype(F32)[:, None] * inv_freq
    cos, sin = jnp.cos(ang), jnp.sin(ang)
    return jnp.tile(cos, (1, 2 * reps)), jnp.tile(jnp.concatenate([-sin, sin], axis=1), (1, reps))


def _split_w_in(w):
    d = w.shape[0]
    o_a = 4 * DN_WIDTH
    o_qsa = o_a + 2 * DN_HEADS
    o_kix = o_qsa + 3 * SA_WIDTH + IDX_HEADS * IDX_HEAD_DIM
    o_wix = o_kix + IDX_HEAD_DIM
    o_g = o_wix + IDX_HEADS
    main = jnp.concatenate([w[:, :o_a], w[:, o_qsa:o_kix], w[:, o_g:]], axis=1)
    small = jnp.concatenate([w[:, o_kix:o_wix], w[:, o_a:o_qsa], w[:, o_wix:o_g],
                             jnp.zeros((d, LANES - IDX_HEAD_DIM - 2 * DN_HEADS - IDX_HEADS), w.dtype)], axis=1)
    return main.astype(BF16), small.astype(BF16)


def _pad_lanes(v, offset):
    return jnp.zeros((1, LANES), F32).at[0, offset:offset + v.shape[0]].set(v.astype(F32))


def _mixer(x, cos_sa, sin_sa, cos_ix, sin_ix, norm_gain, w_in, conv_w, a_log, dt_bias, dn_norm,
           w_dn_out, w_sa_out, w_o):
    s = x.shape[0]
    w_main, w_small = _split_w_in(w_in)
    h = rmsnorm(x, norm_gain, BF16)
    proj = matmul(h, w_main, F32, tm=min(1024, s), tn=512, name="in_proj")
    small = matmul(h, w_small, F32, tm=min(1024, s), tn=LANES, name="in_proj_small")

    qd, kd, w, u, att, gl = dn_chunk(proj, small, conv_w.astype(F32), _pad_lanes(a_log, SM_A), _pad_lanes(dt_bias, SM_A))
    o_dn = dn_scan(qd, kd, w, u, att, gl, proj, dn_norm.astype(F32))

    q_sa, k_sa, v_sa = rope_sa(proj, cos_sa, sin_sa)
    q_ix, k_lo, k_hi = rope_ix(proj, small, cos_ix, sin_ix)
    t_attn = min(512, s)
    mask_t = index_mask(q_ix, k_lo, k_hi, small, min(TOPK_MAX, s // 4), tq=min(256, s), tk=t_attn, t_attn=t_attn)
    o_sa = masked_attention(q_sa, k_sa, v_sa, mask_t, t=t_attn)

    merged = merge_branches(o_dn, o_sa, w_dn_out.astype(BF16), w_sa_out.astype(BF16), proj)
    return matmul(merged, w_o.astype(BF16), F32, tm=512, tn=512, residual=x, name="out_proj")


def _moe_layer(x, h, norm_gain, w_router, w_gate, w_up, w_down):
    s, d = x.shape
    tile = min(512, s)
    n_tiles = (2 * s) // tile + N_EXPERTS
    wr = jnp.zeros((d, LANES), BF16).at[:, :N_EXPERTS].set(w_router.astype(BF16))
    w, m1, m2, rank, totals = route_tokens(h, wr)
    pw, meta = dispatch_plan(w, m1, m2, rank, totals, tile)
    pos1 = pw[:, 0].astype(I32)
    pos2 = pw[:, 1].astype(I32)
    tile_expert = meta[0, :n_tiles].astype(I32)
    n_used = meta[1, :1].astype(I32)
    xg = dispatch_rows(x, pos1, pos2, n_tiles * tile)
    y = moe_experts(xg, norm_gain.astype(F32), tile_expert, n_used, w_gate.astype(BF16), w_up.astype(BF16),
                    w_down.astype(BF16), tm=tile)
    return combine_rows(x, pw, y, pos1, pos2)


def kernel(x, positions, norm_mix, w_in, conv_w, a_log, dt_bias, dn_norm, w_dn_out, w_sa_out, w_o, norm_ffn, dense_w_gate, dense_w_up, dense_w_down, moe_router, moe_w_gate, moe_w_up, moe_w_down, final_norm):
    b, s, d = x.shape
    depth = norm_mix.shape[0]
    outs = []
    for bi in range(b):
        xb = x[bi]
        pos = positions[bi]
        cos_sa, sin_sa = _rope_tables(pos, SA_HEAD_DIM, 1)
        cos_ix, sin_ix = _rope_tables(pos, IDX_HEAD_DIM, 2)
        for layer in range(depth):
            xb = _mixer(xb, cos_sa, sin_sa, cos_ix, sin_ix, norm_mix[layer], w_in[layer], conv_w[layer],
                        a_log[layer], dt_bias[layer], dn_norm[layer], w_dn_out[layer], w_sa_out[layer], w_o[layer])
            h = rmsnorm(xb, norm_ffn[layer], BF16)
            j = layer // 2
            if layer % 2 == 0:
                xb = ffn_dense(h, xb, dense_w_gate[j].astype(BF16), dense_w_up[j].astype(BF16),
                               dense_w_down[j].astype(BF16))
            else:
                xb = _moe_layer(xb, h, norm_ffn[layer], moe_router[j], moe_w_gate[j], moe_w_up[j], moe_w_down[j])
        outs.append(rmsnorm(xb, final_norm, x.dtype))
    return jnp.stack(outs, axis=0)
```

```python
import functools

import jax
import jax.numpy as jnp
from jax import lax
from jax.experimental import pallas as pl
from jax.experimental.pallas import tpu as pltpu

F32 = jnp.float32
BF16 = jnp.bfloat16
I32 = jnp.int32
I16 = jnp.int16

RMS_EPS = 1e-6
L2_EPS = 1e-6
DN_HEADS = 8
DN_HEAD_DIM = 128
DN_WIDTH = DN_HEADS * DN_HEAD_DIM
CONV_WIDTH = 4
DN_CHUNK = 64
SA_HEADS = 8
SA_HEAD_DIM = 128
SA_WIDTH = SA_HEADS * SA_HEAD_DIM
IDX_HEADS = 16
IDX_HEAD_DIM = 64
TOPK_MAX = 256
ROPE_THETA = 10000.0
N_EXPERTS = 8

LANES = 128
SUBLANES = 8
VMEM_CAP_BYTES = 56 * 2**20
NEG_BIG = -1e30
I16_MIN = -2**15
I16_ROWS = 16
LOG2_E = 1.4426950408889634

COL_QKV = 0
COL_Z = 3072
COL_QSA = 4096
COL_KSA = 5120
COL_VSA = 6144
COL_QIX = 7168
COL_GDN = 8192
COL_GSA = 10240
MAIN_WIDTH = 12288
SM_KIX = 0
SM_A = 64
SM_B = 72
SM_WIX = 80


def _params(semantics, vmem_bytes):
    return pltpu.CompilerParams(dimension_semantics=semantics,
                                vmem_limit_bytes=int(min(max(vmem_bytes, 16 * 2**20), VMEM_CAP_BYTES)))


def _nbytes(shape, dtype):
    n = 1
    for s in shape:
        n *= s
    return n * jnp.dtype(dtype).itemsize


def _sigmoid(x):
    return 1.0 / (1.0 + jnp.exp(-x))


def _dot(a, b):
    return jnp.dot(a, b, preferred_element_type=F32)


def _dot_nt(a, b):
    return lax.dot_general(a, b, (((1,), (1,)), ((), ())), preferred_element_type=F32)


def _dot_tn(a, b):
    return lax.dot_general(a, b, (((0,), (0,)), ((), ())), preferred_element_type=F32)


def _split_bf16(x):
    hi = x.astype(BF16)
    return hi, (x - hi.astype(F32)).astype(BF16)


def _dot3(a, b):
    return _dot(a[0], b[0]) + _dot(a[1], b[0]) + _dot(a[0], b[1])


def _rms(x, gain):
    ms = jnp.mean(x * x, axis=-1, keepdims=True)
    return x * lax.rsqrt(ms + RMS_EPS) * gain


def _rmsnorm_body(x_ref, g_ref, o_ref):
    o_ref[...] = _rms(x_ref[...], g_ref[...]).astype(o_ref.dtype)


def rmsnorm(x, gain, out_dtype, tm=512):
    m, d = x.shape
    vm = 2 * (_nbytes((tm, d), F32) + _nbytes((tm, d), out_dtype)) + 4 * _nbytes((tm, d), F32)
    return pl.pallas_call(
        _rmsnorm_body,
        grid=(m // tm,),
        in_specs=[pl.BlockSpec((tm, d), lambda i: (i, 0)),
                  pl.BlockSpec((1, d), lambda i: (0, 0))],
        out_specs=pl.BlockSpec((tm, d), lambda i: (i, 0)),
        out_shape=jax.ShapeDtypeStruct((m, d), out_dtype),
        compiler_params=_params(("parallel",), vm),
        name="rmsnorm",
    )(x, gain.reshape(1, d))


def _mm_body(a_ref, b_ref, o_ref):
    o_ref[...] = _dot(a_ref[...], b_ref[...]).astype(o_ref.dtype)


def matmul(a, b, out_dtype, tm, tn, name):
    m, k = a.shape
    n = b.shape[1]
    vm = 2 * (_nbytes((tm, k), a.dtype) + _nbytes((k, tn), b.dtype) + _nbytes((tm, tn), out_dtype))
    vm += 2 * _nbytes((tm, tn), F32)
    return pl.pallas_call(
        _mm_body,
        grid=(m // tm, n // tn),
        in_specs=[pl.BlockSpec((tm, k), lambda i, j: (i, 0)),
                  pl.BlockSpec((k, tn), lambda i, j: (0, j))],
        out_specs=pl.BlockSpec((tm, tn), lambda i, j: (i, j)),
        out_shape=jax.ShapeDtypeStruct((m, n), out_dtype),
        compiler_params=_params(("parallel", "parallel"), vm),
        name=name,
    )(a, b)


def _dn_chunk_body(xc_ref, xp_ref, sm_ref, cw_ref, alog_ref, dtb_ref,
                   qd_ref, kd_ref, w_ref, u_ref, att_ref, gl_ref, ext_ref):
    c = pl.program_id(0)
    C = DN_CHUNK
    halo = SUBLANES
    ext_ref[0:halo, :] = jnp.where(c > 0, xp_ref[...], 0.0)
    ext_ref[halo:halo + C, :] = xc_ref[...]
    cw = cw_ref[...]
    y = cw[0:1, :] * ext_ref[pl.ds(halo - CONV_WIDTH + 1, C), :]
    for j in range(1, CONV_WIDTH):
        y = y + cw[j:j + 1, :] * ext_ref[pl.ds(halo - CONV_WIDTH + 1 + j, C), :]
    y = y * _sigmoid(y)

    sm = sm_ref[...]
    xa = sm + dtb_ref[...]
    softplus = jnp.maximum(xa, 0.0) + jnp.log1p(jnp.exp(-jnp.abs(xa)))
    g = -jnp.exp(alog_ref[...]) * softplus
    beta = _sigmoid(sm)

    row = lax.broadcasted_iota(I32, (C, LANES), 0)
    gc = g
    d = 1
    while d < C:
        gc = gc + jnp.where(row >= d, pltpu.roll(gc, d, axis=0), 0.0)
        d *= 2
    gct = jnp.concatenate([gc, jnp.zeros_like(gc)], axis=0).T
    ex = jnp.exp(gc)
    gc_last = gc[C - 1:C, :]
    exl = jnp.exp(gc_last - gc)
    gl_ref[...] = jnp.exp(jnp.broadcast_to(gct[SM_A:SM_A + DN_HEADS, C - 1:C], (DN_HEADS, LANES)))

    ri = lax.broadcasted_iota(I32, (C, C), 0)
    ci = lax.broadcasted_iota(I32, (C, C), 1)
    tril = ri >= ci
    strict = ri > ci
    eye = jnp.where(ri == ci, 1.0, 0.0).astype(F32)
    lvl_masks = []
    lb = 0
    while (1 << lb) < C:
        lvl_masks.append(((ri >> (lb + 1)) == (ci >> (lb + 1)))
                         & (((ri >> lb) & 1) == 1) & (((ci >> lb) & 1) == 0))
        lb += 1

    heads = range(DN_HEADS)
    a_mats, kbs, vbs, excols = [], [], [], []
    for h in heads:
        sl = slice(h * DN_HEAD_DIM, (h + 1) * DN_HEAD_DIM)
        qh = y[:, h * DN_HEAD_DIM:(h + 1) * DN_HEAD_DIM]
        kh = y[:, DN_WIDTH + h * DN_HEAD_DIM:DN_WIDTH + (h + 1) * DN_HEAD_DIM]
        vh = y[:, 2 * DN_WIDTH + h * DN_HEAD_DIM:2 * DN_WIDTH + (h + 1) * DN_HEAD_DIM]
        qn = qh * lax.rsqrt(jnp.sum(qh * qh, axis=-1, keepdims=True) + L2_EPS) * (DN_HEAD_DIM ** -0.5)
        kn = kh * lax.rsqrt(jnp.sum(kh * kh, axis=-1, keepdims=True) + L2_EPS)
        bcol = beta[:, SM_B + h:SM_B + h + 1]
        gcol = gc[:, SM_A + h:SM_A + h + 1]
        grow = gct[SM_A + h:SM_A + h + 1, 0:C]
        dec = jnp.exp(jnp.where(tril, gcol - grow, -jnp.inf))
        kb = kn * bcol
        knb = kn.astype(BF16)
        excol = ex[:, SM_A + h:SM_A + h + 1]
        a_mats.append(jnp.where(strict, _dot_nt(kb.astype(BF16), knb) * dec, 0.0))
        kbs.append(kb * excol)
        vbs.append(vh * bcol)
        att = jnp.where(tril, _dot_nt(qn.astype(BF16), knb) * dec, 0.0)
        att_ref[:, h * C:(h + 1) * C] = att.astype(att_ref.dtype)
        qd_ref[:, sl] = (qn * excol).astype(qd_ref.dtype)
        kd_ref[:, sl] = (kn * exl[:, SM_A + h:SM_A + h + 1]).astype(kd_ref.dtype)

    x_inv = [eye - jnp.where(lvl_masks[0], a, 0.0) for a in a_mats]
    for lm in lvl_masks[1:]:
        xs = [_split_bf16(x) for x in x_inv]
        ts = [_dot3(xs[h], _split_bf16(jnp.where(lm, a_mats[h], 0.0))) for h in heads]
        x_inv = [x_inv[h] - _dot3(_split_bf16(ts[h]), xs[h]) for h in heads]
    xs = [_split_bf16(x) for x in x_inv]
    for h in heads:
        sl = slice(h * DN_HEAD_DIM, (h + 1) * DN_HEAD_DIM)
        w_ref[:, sl] = _dot3(xs[h], _split_bf16(kbs[h])).astype(w_ref.dtype)
        u_ref[:, sl] = _dot3(xs[h], _split_bf16(vbs[h]))


def dn_chunk(proj, small, conv_w, alog_pad, dtb_pad):
    s = proj.shape[0]
    C = DN_CHUNK
    n_chunks = s // C
    w3 = 3 * DN_WIDTH
    row_spec = lambda width, dt: pl.BlockSpec((C, width), lambda c: (c, 0))
    vm = 2 * (_nbytes((C, w3), F32) + _nbytes((SUBLANES, w3), F32)) + 8 * _nbytes((C, w3), F32)
    return pl.pallas_call(
        _dn_chunk_body,
        grid=(n_chunks,),
        in_specs=[pl.BlockSpec((C, w3), lambda c: (c, COL_QKV // w3)),
                  pl.BlockSpec((SUBLANES, w3), lambda c: (jnp.maximum(c * (C // SUBLANES) - 1, 0), COL_QKV // w3)),
                  pl.BlockSpec((C, LANES), lambda c: (c, 0)),
                  pl.BlockSpec((CONV_WIDTH, w3), lambda c: (0, 0)),
                  pl.BlockSpec((1, LANES), lambda c: (0, 0)),
                  pl.BlockSpec((1, LANES), lambda c: (0, 0))],
        out_specs=[row_spec(DN_WIDTH, BF16), row_spec(DN_WIDTH, BF16), row_spec(DN_WIDTH, BF16),
                   row_spec(DN_WIDTH, F32), row_spec(DN_HEADS * C, BF16),
                   pl.BlockSpec((DN_HEADS, LANES), lambda c: (c, 0))],
        out_shape=[jax.ShapeDtypeStruct((s, DN_WIDTH), BF16),
                   jax.ShapeDtypeStruct((s, DN_WIDTH), BF16),
                   jax.ShapeDtypeStruct((s, DN_WIDTH), BF16),
                   jax.ShapeDtypeStruct((s, DN_WIDTH), F32),
                   jax.ShapeDtypeStruct((s, DN_HEADS * C), BF16),
                   jax.ShapeDtypeStruct((n_chunks * DN_HEADS, LANES), F32)],
        scratch_shapes=[pltpu.VMEM((SUBLANES + C, w3), F32)],
        compiler_params=_params(("parallel",), vm),
        name="dn_chunk",
    )(proj, proj, small, conv_w, alog_pad, dtb_pad)


def _dn_scan_body(qd_ref, kd_ref, w_ref, u_ref, att_ref, gl_ref, z_ref, nrm_ref, o_ref, st_ref):
    c = pl.program_id(0)
    C = DN_CHUNK

    @pl.when(c == 0)
    def _():
        st_ref[...] = jnp.zeros_like(st_ref)

    heads = range(DN_HEADS)
    sls = [slice(h * DN_HEAD_DIM, (h + 1) * DN_HEAD_DIM) for h in heads]
    states = [st_ref[h] for h in heads]
    sbs = [s.astype(BF16) for s in states]
    w_s = [_dot(w_ref[:, sls[h]], sbs[h]) for h in heads]
    q_s = [_dot(qd_ref[:, sls[h]], sbs[h]) for h in heads]
    vbs = [(u_ref[:, sls[h]] - w_s[h]).astype(BF16) for h in heads]
    outs = [q_s[h] + _dot(att_ref[:, h * C:(h + 1) * C], vbs[h]) for h in heads]
    for h in heads:
        st_ref[h] = states[h] * gl_ref[h:h + 1, :] + _dot_tn(kd_ref[:, sls[h]], vbs[h])
    for h in heads:
        o = outs[h]
        ms = jnp.mean(o * o, axis=-1, keepdims=True)
        z = z_ref[:, sls[h]]
        o_ref[:, sls[h]] = (o * lax.rsqrt(ms + RMS_EPS) * nrm_ref[...] * (z * _sigmoid(z))).astype(o_ref.dtype)


def dn_scan(qd, kd, w, u, att, gl, proj, dn_norm):
    s = qd.shape[0]
    C = DN_CHUNK
    row = lambda width: pl.BlockSpec((C, width), lambda c: (c, 0))
    vm = 2 * 6 * _nbytes((C, DN_WIDTH), F32) + 2 * _nbytes((DN_HEADS, DN_HEAD_DIM, DN_HEAD_DIM), F32)
    return pl.pallas_call(
        _dn_scan_body,
        grid=(s // C,),
        in_specs=[row(DN_WIDTH), row(DN_WIDTH), row(DN_WIDTH), row(DN_WIDTH), row(DN_HEADS * C),
                  pl.BlockSpec((DN_HEADS, LANES), lambda c: (c, 0)),
                  pl.BlockSpec((C, DN_WIDTH), lambda c: (c, COL_Z // DN_WIDTH)),
                  pl.BlockSpec((1, DN_HEAD_DIM), lambda c: (0, 0))],
        out_specs=row(DN_WIDTH),
        out_shape=jax.ShapeDtypeStruct((s, DN_WIDTH), BF16),
        scratch_shapes=[pltpu.VMEM((DN_HEADS, DN_HEAD_DIM, DN_HEAD_DIM), F32)],
        compiler_params=_params(("arbitrary",), vm),
        name="dn_scan",
    )(qd, kd, w, u, att, gl, proj, dn_norm.reshape(1, DN_HEAD_DIM))


def _rope_sa_body(q_ref, k_ref, v_ref, c_ref, s_ref, qo_ref, ko_ref, vo_ref):
    cs = c_ref[...]
    sn = s_ref[...]
    scale = SA_HEAD_DIM ** -0.5 * LOG2_E
    for h in range(SA_HEADS):
        sl = slice(h * SA_HEAD_DIM, (h + 1) * SA_HEAD_DIM)
        x = q_ref[:, sl]
        qo_ref[:, sl] = ((x * cs + pltpu.roll(x, SA_HEAD_DIM // 2, axis=1) * sn) * scale).astype(qo_ref.dtype)
        x = k_ref[:, sl]
        ko_ref[:, sl] = (x * cs + pltpu.roll(x, SA_HEAD_DIM // 2, axis=1) * sn).astype(ko_ref.dtype)
    vo_ref[...] = v_ref[...].astype(vo_ref.dtype)


def rope_sa(proj, cos_t, sin_t, tm=512):
    s = proj.shape[0]
    col = lambda off: pl.BlockSpec((tm, SA_WIDTH), lambda i: (i, off // SA_WIDTH))
    tab = pl.BlockSpec((tm, LANES), lambda i: (i, 0))
    out = pl.BlockSpec((tm, SA_WIDTH), lambda i: (i, 0))
    vm = 2 * 3 * (_nbytes((tm, SA_WIDTH), F32) + _nbytes((tm, SA_WIDTH), BF16)) + 4 * _nbytes((tm, SA_WIDTH), F32)
    return pl.pallas_call(
        _rope_sa_body,
        grid=(s // tm,),
        in_specs=[col(COL_QSA), col(COL_KSA), col(COL_VSA), tab, tab],
        out_specs=[out, out, out],
        out_shape=[jax.ShapeDtypeStruct((s, SA_WIDTH), BF16)] * 3,
        compiler_params=_params(("parallel",), vm),
        name="rope_sa",
    )(proj, proj, proj, cos_t, sin_t)


def _rope_ix_body(q_ref, sm_ref, c_ref, s_ref, qo_ref, klo_ref, khi_ref):
    cs = c_ref[...]
    sn = s_ref[...]
    lane = lax.broadcasted_iota(I32, cs.shape, 1)
    first = (lane & (IDX_HEAD_DIM - 1)) < IDX_HEAD_DIM // 2
    half = IDX_HEAD_DIM // 2

    def rot(x):
        swapped = jnp.where(first, pltpu.roll(x, LANES - half, axis=1), pltpu.roll(x, half, axis=1))
        return x * cs + swapped * sn

    for j in range(IDX_HEADS * IDX_HEAD_DIM // LANES):
        sl = slice(j * LANES, (j + 1) * LANES)
        qo_ref[:, sl] = rot(q_ref[:, sl]).astype(qo_ref.dtype)
    k_lo = jnp.where(lane < IDX_HEAD_DIM, rot(sm_ref[...]), 0.0)
    klo_ref[...] = k_lo.astype(klo_ref.dtype)
    khi_ref[...] = pltpu.roll(k_lo, IDX_HEAD_DIM, axis=1).astype(khi_ref.dtype)


def rope_ix(proj, small, cos_t, sin_t, tm=512):
    s = proj.shape[0]
    wq = IDX_HEADS * IDX_HEAD_DIM
    tab = pl.BlockSpec((tm, LANES), lambda i: (i, 0))
    vm = 2 * (_nbytes((tm, wq), F32) + _nbytes((tm, wq), BF16)) + 4 * _nbytes((tm, wq), F32)
    return pl.pallas_call(
        _rope_ix_body,
        grid=(s // tm,),
        in_specs=[pl.BlockSpec((tm, wq), lambda i: (i, COL_QIX // wq)), tab, tab, tab],
        out_specs=[pl.BlockSpec((tm, wq), lambda i: (i, 0)), tab, tab],
        out_shape=[jax.ShapeDtypeStruct((s, wq), BF16),
                   jax.ShapeDtypeStruct((s, LANES), BF16),
                   jax.ShapeDtypeStruct((s, LANES), BF16)],
        compiler_params=_params(("parallel",), vm),
        name="rope_ix",
    )(proj, small, cos_t, sin_t)


def _index_body(q_ref, klo_ref, khi_ref, sm_ref, mask_ref, keys_ref, hi_ref, lo_ref, *, tq, tk, topk):
    i = pl.program_id(0)
    nkt = keys_ref.shape[0]
    nk = ((i + 1) * tq + tk - 1) // tk
    w_t = (sm_ref[...] * (IDX_HEADS ** -0.5 * IDX_HEAD_DIM ** -0.5)).T
    key_l = lax.broadcasted_iota(I32, (tk, tq), 0)
    qry_g = i * tq + lax.broadcasted_iota(I32, (tk, tq), 1)

    def score_tile(kt, carry):
        off = pl.multiple_of(kt * tk, tk)
        k_lo = klo_ref[pl.ds(off, tk), :]
        k_hi = khi_ref[pl.ds(off, tk), :]
        acc = jnp.zeros((tk, tq), F32)
        for j in range(IDX_HEADS // 2):
            qp = q_ref[:, j * LANES:(j + 1) * LANES]
            acc = acc + w_t[SM_WIX + 2 * j:SM_WIX + 2 * j + 1, :] * jnp.maximum(_dot_nt(k_lo, qp), 0.0)
            acc = acc + w_t[SM_WIX + 2 * j + 1:SM_WIX + 2 * j + 2, :] * jnp.maximum(_dot_nt(k_hi, qp), 0.0)
        sc = jnp.where(kt * tk + key_l <= qry_g, acc, -jnp.inf)
        bits = pltpu.bitcast(sc, I32)
        key = bits ^ ((bits >> 31) & 0x7FFFFFFF)
        keys_ref[kt] = key
        hi_ref[kt] = (key >> 16).astype(I16)
        return carry

    lax.fori_loop(0, nk, score_tile, 0)

    cnt_rows = 2 * I16_ROWS

    def count16(ref, cand, strict):
        def body(kt, cnt):
            t = ref[kt]
            hit = jnp.where((t > cand) if strict else (t >= cand), jnp.int16(1), jnp.int16(0))
            for j in range(tk // cnt_rows):
                cnt = cnt + hit[j * cnt_rows:(j + 1) * cnt_rows]
            return cnt

        cnt = lax.fori_loop(0, nk, body, jnp.zeros((cnt_rows, tq), I16))
        return jnp.sum(cnt.astype(F32), axis=0, keepdims=True)

    def kth_largest16(ref, kth):
        zero = jnp.zeros((1, tq), I32)
        ans = jnp.where(count16(ref, zero.astype(I16), False) >= kth, zero, I16_MIN)

        def bit_body(b, ans):
            cand = ans + lax.shift_left(jnp.int32(1), 14 - b)
            return jnp.where(count16(ref, cand.astype(I16), False) >= kth, cand, ans)

        return lax.fori_loop(0, 15, bit_body, ans)

    kf = jnp.full((1, tq), float(topk), F32)
    t_hi = kth_largest16(hi_ref, kf)
    above = count16(hi_ref, t_hi.astype(I16), True)

    def low_tile(kt, carry):
        key = keys_ref[kt]
        lo = (key & 0xFFFF) + I16_MIN
        lo_ref[kt] = jnp.where((key >> 16) == t_hi, lo, I16_MIN).astype(I16)
        return carry

    lax.fori_loop(0, nk, low_tile, 0)
    t_lo = kth_largest16(lo_ref, kf - above)
    ans = t_hi * 65536 + (t_lo - I16_MIN)

    def write(kt, carry):
        sel = (keys_ref[kt] >= ans) & (kt * tk + key_l <= qry_g)
        mask_ref[0, pl.ds(pl.multiple_of(kt * tk, tk), tk), :] = jnp.where(sel, 0.0, NEG_BIG).astype(mask_ref.dtype)
        return carry

    lax.fori_loop(0, nk, write, 0)

    def clear(kt, carry):
        mask_ref[0, pl.ds(pl.multiple_of(kt * tk, tk), tk), :] = jnp.full((tk, tq), NEG_BIG, mask_ref.dtype)
        return carry

    lax.fori_loop(nk, nkt, clear, 0)


def index_mask(q_ix, k_lo, k_hi, small, topk, tq, tk, t_attn):
    s = q_ix.shape[0]
    nkt = s // tk
    per = t_attn // tq
    wq = IDX_HEADS * IDX_HEAD_DIM
    vm = (2 * (_nbytes((tq, wq), BF16) + 2 * _nbytes((s, LANES), BF16) + _nbytes((tq, LANES), F32)
               + _nbytes((s, tq), BF16)) + 2 * _nbytes((s, tq), I32) + 8 * _nbytes((tk, tq), F32))
    return pl.pallas_call(
        functools.partial(_index_body, tq=tq, tk=tk, topk=topk),
        grid=(s // tq,),
        in_specs=[pl.BlockSpec((tq, wq), lambda i: (i, 0)),
                  pl.BlockSpec((s, LANES), lambda i: (0, 0)),
                  pl.BlockSpec((s, LANES), lambda i: (0, 0)),
                  pl.BlockSpec((tq, LANES), lambda i: (i, 0))],
        out_specs=pl.BlockSpec((1, s, tq), lambda i: (i // per, 0, i % per)),
        out_shape=jax.ShapeDtypeStruct((s // t_attn, s, t_attn), BF16),
        scratch_shapes=[pltpu.VMEM((nkt, tk, tq), I32), pltpu.VMEM((nkt, tk, tq), I16), pltpu.VMEM((nkt, tk, tq), I16)],
        compiler_params=_params(("parallel",), vm),
        name="index_mask",
    )(q_ix, k_lo, k_hi, small)


def _attn_body(qi_ref, ki_ref, q_ref, k_ref, v_ref, mk_ref, o_ref, m_ref, l_ref, acc_ref, bias_ref):
    p = pl.program_id(0)
    qi = qi_ref[p]
    ki = ki_ref[p]

    @pl.when(ki == 0)
    def _():
        m_ref[...] = jnp.full_like(m_ref, NEG_BIG)
        l_ref[...] = jnp.zeros_like(l_ref)
        acc_ref[...] = jnp.zeros_like(acc_ref)

    bias_ref[...] = mk_ref[0].astype(F32)
    for h in range(SA_HEADS):
        sl = slice(h * SA_HEAD_DIM, (h + 1) * SA_HEAD_DIM)
        s = _dot_nt(k_ref[:, sl], q_ref[:, sl]) + bias_ref[...]
        m_old = m_ref[h:h + 1, :]
        m_new = jnp.maximum(m_old, jnp.max(s, axis=0, keepdims=True))
        alpha = jnp.exp2(m_old - m_new)
        pr = jnp.exp2(s - m_new)
        l_ref[h:h + 1, :] = alpha * l_ref[h:h + 1, :] + jnp.sum(pr, axis=0, keepdims=True)
        m_ref[h:h + 1, :] = m_new
        acc_ref[h] = alpha * acc_ref[h] + _dot_tn(v_ref[:, sl], pr.astype(BF16))

    @pl.when(ki == qi)
    def _():
        for h in range(SA_HEADS):
            sl = slice(h * SA_HEAD_DIM, (h + 1) * SA_HEAD_DIM)
            o_ref[:, sl] = (acc_ref[h] / l_ref[h:h + 1, :]).T.astype(o_ref.dtype)


def masked_attention(q, k, v, mask_t, t):
    s = q.shape[0]
    nb = s // t
    pairs = [(a, b) for a in range(nb) for b in range(a + 1)]
    qi = jnp.asarray([a for a, _ in pairs], I32)
    ki = jnp.asarray([b for _, b in pairs], I32)
    vm = (2 * (4 * _nbytes((t, SA_WIDTH), BF16) + _nbytes((t, t), BF16)) + _nbytes((t, SA_WIDTH), F32)
          + 2 * _nbytes((SA_HEADS, t), F32) + 8 * _nbytes((t, t), F32))
    grid_spec = pltpu.PrefetchScalarGridSpec(
        num_scalar_prefetch=2,
        grid=(len(pairs),),
        in_specs=[pl.BlockSpec((t, SA_WIDTH), lambda p, qi, ki: (qi[p], 0)),
                  pl.BlockSpec((t, SA_WIDTH), lambda p, qi, ki: (ki[p], 0)),
                  pl.BlockSpec((t, SA_WIDTH), lambda p, qi, ki: (ki[p], 0)),
                  pl.BlockSpec((1, t, t), lambda p, qi, ki: (qi[p], ki[p], 0))],
        out_specs=pl.BlockSpec((t, SA_WIDTH), lambda p, qi, ki: (qi[p], 0)),
        scratch_shapes=[pltpu.VMEM((SA_HEADS, t), F32),
                        pltpu.VMEM((SA_HEADS, t), F32),
                        pltpu.VMEM((SA_HEADS, SA_HEAD_DIM, t), F32),
                        pltpu.VMEM((t, t), F32)],
    )
    return pl.pallas_call(
        _attn_body,
        grid_spec=grid_spec,
        out_shape=jax.ShapeDtypeStruct((s, SA_WIDTH), BF16),
        compiler_params=_params(("arbitrary",), vm),
        name="masked_attention",
    )(qi, ki, q, k, v, mask_t)


def _merge_body(odn_ref, osa_ref, wdn_ref, wsa_ref, gdn_ref, gsa_ref, o_ref):
    y_dn = _dot(odn_ref[...], wdn_ref[...])
    y_sa = _dot(osa_ref[...], wsa_ref[...])
    o_ref[...] = (_sigmoid(gdn_ref[...]) * y_dn + _sigmoid(gsa_ref[...]) * y_sa).astype(o_ref.dtype)


def merge_branches(o_dn, o_sa, w_dn, w_sa, proj, tm=512, tn=512):
    s, kd = o_dn.shape
    d = w_dn.shape[1]
    vm = 2 * (2 * _nbytes((tm, kd), BF16) + 2 * _nbytes((kd, tn), BF16) + 2 * _nbytes((tm, tn), F32)
              + _nbytes((tm, tn), BF16)) + 6 * _nbytes((tm, tn), F32)
    return pl.pallas_call(
        _merge_body,
        grid=(s // tm, d // tn),
        in_specs=[pl.BlockSpec((tm, kd), lambda i, j: (i, 0)),
                  pl.BlockSpec((tm, kd), lambda i, j: (i, 0)),
                  pl.BlockSpec((kd, tn), lambda i, j: (0, j)),
                  pl.BlockSpec((kd, tn), lambda i, j: (0, j)),
                  pl.BlockSpec((tm, tn), lambda i, j: (i, COL_GDN // tn + j)),
                  pl.BlockSpec((tm, tn), lambda i, j: (i, COL_GSA // tn + j))],
        out_specs=pl.BlockSpec((tm, tn), lambda i, j: (i, j)),
        out_shape=jax.ShapeDtypeStruct((s, d), BF16),
        compiler_params=_params(("parallel", "parallel"), vm),
        name="merge_branches",
    )(o_dn, o_sa, w_dn, w_sa, proj, proj)


def _out_proj_body(m_ref, w_ref, x_ref, gn_ref, o_ref, hn_ref):
    y = x_ref[...] + _dot(m_ref[...], w_ref[...])
    o_ref[...] = y
    hn_ref[...] = _rms(y, gn_ref[...]).astype(hn_ref.dtype)


def out_proj_norm(merged, w_o, x, next_gain, tm=512):
    s, k = merged.shape
    d = w_o.shape[1]
    tm = min(tm, s)
    vm = 2 * (_nbytes((tm, k), BF16) + _nbytes((k, d), BF16) + 2 * _nbytes((tm, d), F32) + _nbytes((tm, d), BF16)) + 3 * _nbytes((tm, d), F32)
    row = lambda width: pl.BlockSpec((tm, width), lambda i: (i, 0))
    return pl.pallas_call(
        _out_proj_body,
        grid=(s // tm,),
        in_specs=[row(k), pl.BlockSpec((k, d), lambda i: (0, 0)), row(d), pl.BlockSpec((1, d), lambda i: (0, 0))],
        out_specs=[row(d), row(d)],
        out_shape=[jax.ShapeDtypeStruct((s, d), F32), jax.ShapeDtypeStruct((s, d), BF16)],
        compiler_params=_params(("parallel",), vm),
        name="out_proj",
    )(merged, w_o, x, next_gain.reshape(1, d))


def _ffn_body(h_ref, x_ref, wg_ref, wu_ref, wd_ref, gn_ref, o_ref, hn_ref):
    f = pl.program_id(1)

    @pl.when(f == 0)
    def _():
        o_ref[...] = jnp.zeros_like(o_ref)

    h = h_ref[...]
    g = _dot(h, wg_ref[...])
    u = _dot(h, wu_ref[...])
    o_ref[...] += _dot((g * _sigmoid(g) * u).astype(BF16), wd_ref[...])

    @pl.when(f == pl.num_programs(1) - 1)
    def _():
        y = x_ref[...] + o_ref[...]
        o_ref[...] = y
        hn_ref[...] = _rms(y, gn_ref[...]).astype(hn_ref.dtype)


def ffn_dense(h, x, w_gate, w_up, w_down, next_gain, hn_dtype, tm=512, tf=512):
    s, d = h.shape
    ff = w_gate.shape[1]
    vm = (2 * (_nbytes((tm, d), BF16) + _nbytes((tm, d), hn_dtype) + 2 * _nbytes((tm, d), F32)
               + 3 * _nbytes((d, tf), BF16)) + 3 * _nbytes((tm, d), F32) + 6 * _nbytes((tm, tf), F32))
    row = pl.BlockSpec((tm, d), lambda i, f: (i, 0))
    return pl.pallas_call(
        _ffn_body,
        grid=(s // tm, ff // tf),
        in_specs=[row, row,
                  pl.BlockSpec((d, tf), lambda i, f: (0, f)),
                  pl.BlockSpec((d, tf), lambda i, f: (0, f)),
                  pl.BlockSpec((tf, d), lambda i, f: (f, 0)),
                  pl.BlockSpec((1, d), lambda i, f: (0, 0))],
        out_specs=[row, row],
        out_shape=[jax.ShapeDtypeStruct((s, d), F32), jax.ShapeDtypeStruct((s, d), hn_dtype)],
        compiler_params=_params(("parallel", "arbitrary"), vm),
        name="ffn_dense",
    )(h, x, w_gate, w_up, w_down, next_gain.reshape(1, d))


def _router_body(h_ref, wr_ref, w_ref, m1_ref, m2_ref, rank_ref, tot_ref, cnt_ref):
    logits = _dot(h_ref[...], wr_ref[...])
    lane = lax.broadcasted_iota(I32, logits.shape, 1)
    lg = jnp.where(lane < N_EXPERTS, logits, -jnp.inf)
    m1 = jnp.max(lg, axis=1, keepdims=True)
    i1 = jnp.min(jnp.where(lg == m1, lane, LANES), axis=1, keepdims=True)
    lg2 = jnp.where(lane == i1, -jnp.inf, lg)
    m2 = jnp.max(lg2, axis=1, keepdims=True)
    i2 = jnp.min(jnp.where(lg2 == m2, lane, LANES), axis=1, keepdims=True)
    e = jnp.exp(m2 - m1)
    first = lane == i1
    second = lane == i2
    w_ref[...] = jnp.where(first, 1.0 / (1.0 + e), 0.0) + jnp.where(second, e / (1.0 + e), 0.0)
    m1_ref[...] = jnp.where(first, 1.0, 0.0)
    m2_ref[...] = jnp.where(second, 1.0, 0.0)

    @pl.when(pl.program_id(0) == 0)
    def _():
        cnt_ref[...] = jnp.zeros_like(cnt_ref)

    tm = logits.shape[0]
    sel = jnp.where(first | second, 1.0, 0.0).astype(BF16)
    ri = lax.broadcasted_iota(I32, (tm, tm), 0)
    ci = lax.broadcasted_iota(I32, (tm, tm), 1)
    before = jnp.where(ri > ci, 1.0, 0.0).astype(BF16)
    run = cnt_ref[0:1, :]
    rank_ref[...] = run + _dot(before, sel)
    run = run + jnp.sum(sel.astype(F32), axis=0, keepdims=True)
    cnt_ref[...] = jnp.broadcast_to(run, cnt_ref.shape)
    tot_ref[...] = jnp.broadcast_to(run, tot_ref.shape)


def route_tokens(h, w_router_pad, tm=512):
    s, d = h.shape
    tm = min(tm, s)
    vm = 2 * (_nbytes((tm, d), BF16) + _nbytes((d, LANES), BF16) + 4 * _nbytes((tm, LANES), F32)) + 4 * _nbytes((tm, tm), F32)
    row = pl.BlockSpec((tm, LANES), lambda i: (i, 0))
    return pl.pallas_call(
        _router_body,
        grid=(s // tm,),
        in_specs=[pl.BlockSpec((tm, d), lambda i: (i, 0)),
                  pl.BlockSpec((d, LANES), lambda i: (0, 0))],
        out_specs=[row, row, row, row, pl.BlockSpec((SUBLANES, LANES), lambda i: (0, 0))],
        out_shape=[jax.ShapeDtypeStruct((s, LANES), F32)] * 4 + [jax.ShapeDtypeStruct((SUBLANES, LANES), F32)],
        scratch_shapes=[pltpu.VMEM((SUBLANES, LANES), F32)],
        compiler_params=_params(("arbitrary",), vm),
        name="moe_router",
    )(h, w_router_pad)


def _plan_body(w_ref, m1_ref, m2_ref, rank_ref, tot_ref, pw_ref, meta_ref, *, tile):
    lane8 = lax.broadcasted_iota(I32, (SUBLANES, LANES), 1)
    n = tot_ref[...]
    padded = jnp.floor((n + (tile - 1.0)) * (1.0 / tile)) * tile
    ends = padded
    d = 1
    while d < N_EXPERTS:
        ends = ends + jnp.where(lane8 >= d, pltpu.roll(ends, d, axis=1), 0.0)
        d *= 2
    start = (ends - padded)[0:1, :]
    posf = start + rank_ref[...]
    m1 = m1_ref[...]
    m2 = m2_ref[...]
    w = w_ref[...]
    lane = lax.broadcasted_iota(I32, w.shape, 1)
    cols = [jnp.sum(m1 * posf, axis=1, keepdims=True), jnp.sum(m2 * posf, axis=1, keepdims=True),
            jnp.sum(m1 * w, axis=1, keepdims=True), jnp.sum(m2 * w, axis=1, keepdims=True)]
    out = jnp.zeros_like(w)
    for j, col in enumerate(cols):
        out = jnp.where(lane == j, col, out)
    pw_ref[...] = out
    tile_start = lane8.astype(F32) * tile
    owner = jnp.zeros((SUBLANES, LANES), F32)
    for e in range(N_EXPERTS - 1):
        owner = owner + jnp.where(tile_start >= ends[:, e:e + 1], 1.0, 0.0)
    n_used = ends[:, N_EXPERTS - 1:N_EXPERTS] * (1.0 / tile)
    row8 = lax.broadcasted_iota(I32, (SUBLANES, LANES), 0)
    meta_ref[...] = jnp.where(row8 == 0, owner, jnp.broadcast_to(n_used, owner.shape))


def dispatch_plan(w, m1, m2, rank, totals, tile):
    s = w.shape[0]
    full = pl.BlockSpec((s, LANES), lambda: (0, 0))
    small = pl.BlockSpec((SUBLANES, LANES), lambda: (0, 0))
    vm = 2 * 5 * _nbytes((s, LANES), F32) + 8 * _nbytes((s, LANES), F32)
    return pl.pallas_call(
        functools.partial(_plan_body, tile=float(tile)),
        in_specs=[full, full, full, full, small],
        out_specs=[full, small],
        out_shape=[jax.ShapeDtypeStruct((s, LANES), F32), jax.ShapeDtypeStruct((SUBLANES, LANES), F32)],
        compiler_params=pltpu.CompilerParams(vmem_limit_bytes=int(min(vm, VMEM_CAP_BYTES))),
        name="moe_plan",
    )(w, m1, m2, rank, totals)


def _row_copy(src_ref, src_row, dst_ref, dst_row, sem):
    return pltpu.make_async_copy(src_ref.at[pl.ds(src_row, 1), :], dst_ref.at[pl.ds(dst_row, 1), :], sem)


def _dispatch_body(p1_ref, p2_ref, x_ref, xg_in_ref, xg_ref, sem, *, tt):
    del xg_in_ref
    base = pl.program_id(0) * tt

    def issue(r, carry):
        _row_copy(x_ref, r, xg_ref, p1_ref[base + r], sem).start()
        _row_copy(x_ref, r, xg_ref, p2_ref[base + r], sem).start()
        return carry

    lax.fori_loop(0, tt, issue, 0)

    def drain(r, carry):
        _row_copy(x_ref, 0, xg_ref, 0, sem).wait()
        _row_copy(x_ref, 0, xg_ref, 0, sem).wait()
        return carry

    lax.fori_loop(0, tt, drain, 0)


def dispatch_rows(x, pos1, pos2, n_rows, tt=256):
    s, d = x.shape
    tt = min(tt, s)
    grid_spec = pltpu.PrefetchScalarGridSpec(
        num_scalar_prefetch=2,
        grid=(s // tt,),
        in_specs=[pl.BlockSpec((tt, d), lambda i, p1, p2: (i, 0)),
                  pl.BlockSpec(memory_space=pl.ANY)],
        out_specs=pl.BlockSpec(memory_space=pl.ANY),
        scratch_shapes=[pltpu.SemaphoreType.DMA(())],
    )
    return pl.pallas_call(
        functools.partial(_dispatch_body, tt=tt),
        grid_spec=grid_spec,
        out_shape=jax.ShapeDtypeStruct((n_rows, d), x.dtype),
        input_output_aliases={3: 0},
        compiler_params=_params(("arbitrary",), 4 * _nbytes((tt, d), F32)),
        name="moe_dispatch",
    )(pos1, pos2, x, jnp.zeros((n_rows, d), x.dtype))


def _combine_body(p1_ref, p2_ref, x_ref, pw_ref, gn_ref, y_ref, o_ref, hn_ref, b1_ref, b2_ref, sem, *, tt):
    base = pl.program_id(0) * tt

    def issue(r, carry):
        _row_copy(y_ref, p1_ref[base + r], b1_ref, r, sem).start()
        _row_copy(y_ref, p2_ref[base + r], b2_ref, r, sem).start()
        return carry

    lax.fori_loop(0, tt, issue, 0)

    def drain(r, carry):
        _row_copy(y_ref, 0, b1_ref, 0, sem).wait()
        _row_copy(y_ref, 0, b2_ref, 0, sem).wait()
        return carry

    lax.fori_loop(0, tt, drain, 0)
    pw = pw_ref[...]
    y = x_ref[...] + pw[:, 2:3] * b1_ref[...] + pw[:, 3:4] * b2_ref[...]
    o_ref[...] = y
    hn_ref[...] = _rms(y, gn_ref[...]).astype(hn_ref.dtype)


def combine_rows(x, pw, y, pos1, pos2, next_gain, hn_dtype, tt=256):
    s, d = x.shape
    tt = min(tt, s)
    grid_spec = pltpu.PrefetchScalarGridSpec(
        num_scalar_prefetch=2,
        grid=(s // tt,),
        in_specs=[pl.BlockSpec((tt, d), lambda i, p1, p2: (i, 0)),
                  pl.BlockSpec((tt, LANES), lambda i, p1, p2: (i, 0)),
                  pl.BlockSpec((1, d), lambda i, p1, p2: (0, 0)),
                  pl.BlockSpec(memory_space=pl.ANY)],
        out_specs=[pl.BlockSpec((tt, d), lambda i, p1, p2: (i, 0)), pl.BlockSpec((tt, d), lambda i, p1, p2: (i, 0))],
        scratch_shapes=[pltpu.VMEM((tt, d), F32), pltpu.VMEM((tt, d), F32), pltpu.SemaphoreType.DMA(())],
    )
    return pl.pallas_call(
        functools.partial(_combine_body, tt=tt),
        grid_spec=grid_spec,
        out_shape=[jax.ShapeDtypeStruct((s, d), F32), jax.ShapeDtypeStruct((s, d), hn_dtype)],
        compiler_params=_params(("arbitrary",), 12 * _nbytes((tt, d), F32)),
        name="moe_combine",
    )(pos1, pos2, x, pw, next_gain.reshape(1, d), y)


def _moe_body(te_ref, nu_ref, xg_ref, gain_ref, wg_ref, wu_ref, wd_ref, o_ref, h_ref, acc_ref):
    i = pl.program_id(0)
    f = pl.program_id(1)

    @pl.when(i < nu_ref[0])
    def _():
        @pl.when(f == 0)
        def _():
            h_ref[...] = _rms(xg_ref[...], gain_ref[...]).astype(h_ref.dtype)
            acc_ref[...] = jnp.zeros_like(acc_ref)

        h = h_ref[...]
        g = _dot(h, wg_ref[0])
        u = _dot(h, wu_ref[0])
        acc_ref[...] += _dot((g * _sigmoid(g) * u).astype(BF16), wd_ref[0])

        @pl.when(f == pl.num_programs(1) - 1)
        def _():
            o_ref[...] = acc_ref[...]

    @pl.when((i >= nu_ref[0]) & (f == 0))
    def _():
        o_ref[...] = jnp.zeros_like(o_ref)


def moe_experts(xg, gain, tile_expert, n_used, w_gate, w_up, w_down, tm, tf=256):
    p, d = xg.shape
    ff = w_gate.shape[2]
    nf = ff // tf
    vm = (2 * (2 * _nbytes((tm, d), F32) + 3 * _nbytes((d, tf), BF16)) + _nbytes((tm, d), BF16)
          + _nbytes((tm, d), F32) + 6 * _nbytes((tm, tf), F32))

    def row(i, f, te, nu):
        return (jnp.minimum(i, nu[0] - 1), 0)

    def fcol(i, f, nu):
        return jnp.where(i < nu[0], f, nf - 1)

    grid_spec = pltpu.PrefetchScalarGridSpec(
        num_scalar_prefetch=2,
        grid=(p // tm, nf),
        in_specs=[pl.BlockSpec((tm, d), row),
                  pl.BlockSpec((1, d), lambda i, f, te, nu: (0, 0)),
                  pl.BlockSpec((1, d, tf), lambda i, f, te, nu: (te[i], 0, fcol(i, f, nu))),
                  pl.BlockSpec((1, d, tf), lambda i, f, te, nu: (te[i], 0, fcol(i, f, nu))),
                  pl.BlockSpec((1, tf, d), lambda i, f, te, nu: (te[i], fcol(i, f, nu), 0))],
        out_specs=pl.BlockSpec((tm, d), lambda i, f, te, nu: (i, 0)),
        scratch_shapes=[pltpu.VMEM((tm, d), BF16), pltpu.VMEM((tm, d), F32)],
    )
    return pl.pallas_call(
        _moe_body,
        grid_spec=grid_spec,
        out_shape=jax.ShapeDtypeStruct((p, d), F32),
        compiler_params=_params(("arbitrary", "arbitrary"), vm),
        name="moe_experts",
    )(tile_expert, n_used, xg, gain.reshape(1, d), w_gate, w_up, w_down)


def _rope_tables(positions, dim, reps):
    inv_freq = ROPE_THETA ** (-jnp.arange(0, dim, 2, dtype=F32) / dim)
    ang = positions.astype(F32)[:, None] * inv_freq
    cos, sin = jnp.cos(ang), jnp.sin(ang)
    return jnp.tile(cos, (1, 2 * reps)), jnp.tile(jnp.concatenate([-sin, sin], axis=1), (1, reps))


def _split_w_in(w):
    d = w.shape[0]
    o_a = 4 * DN_WIDTH
    o_qsa = o_a + 2 * DN_HEADS
    o_kix = o_qsa + 3 * SA_WIDTH + IDX_HEADS * IDX_HEAD_DIM
    o_wix = o_kix + IDX_HEAD_DIM
    o_g = o_wix + IDX_HEADS
    main = jnp.concatenate([w[:, :o_a], w[:, o_qsa:o_kix], w[:, o_g:]], axis=1)
    small = jnp.concatenate([w[:, o_kix:o_wix], w[:, o_a:o_qsa], w[:, o_wix:o_g],
                             jnp.zeros((d, LANES - IDX_HEAD_DIM - 2 * DN_HEADS - IDX_HEADS), w.dtype)], axis=1)
    return main.astype(BF16), small.astype(BF16)


def _pad_lanes(v, offset):
    return jnp.zeros((1, LANES), F32).at[0, offset:offset + v.shape[0]].set(v.astype(F32))


def _mixer(x, h, cos_sa, sin_sa, cos_ix, sin_ix, ffn_gain, w_in, conv_w, a_log, dt_bias, dn_norm,
           w_dn_out, w_sa_out, w_o):
    s = x.shape[0]
    w_main, w_small = _split_w_in(w_in)
    proj = matmul(h, w_main, F32, tm=min(1024, s), tn=512, name="in_proj")
    small = matmul(h, w_small, F32, tm=min(1024, s), tn=LANES, name="in_proj_small")

    qd, kd, w, u, att, gl = dn_chunk(proj, small, conv_w.astype(F32), _pad_lanes(a_log, SM_A), _pad_lanes(dt_bias, SM_A))
    o_dn = dn_scan(qd, kd, w, u, att, gl, proj, dn_norm.astype(F32))

    q_sa, k_sa, v_sa = rope_sa(proj, cos_sa, sin_sa)
    q_ix, k_lo, k_hi = rope_ix(proj, small, cos_ix, sin_ix)
    t_attn = min(512, s)
    mask_t = index_mask(q_ix, k_lo, k_hi, small, min(TOPK_MAX, s // 4), tq=min(256, s), tk=t_attn, t_attn=t_attn)
    o_sa = masked_attention(q_sa, k_sa, v_sa, mask_t, t=t_attn)

    merged = merge_branches(o_dn, o_sa, w_dn_out.astype(BF16), w_sa_out.astype(BF16), proj)
    return out_proj_norm(merged, w_o.astype(BF16), x, ffn_gain.astype(F32))


def _moe_layer(x, h, norm_gain, w_router, w_gate, w_up, w_down, next_gain, hn_dtype):
    s, d = x.shape
    tile = min(512, s)
    n_tiles = (2 * s) // tile + N_EXPERTS
    wr = jnp.zeros((d, LANES), BF16).at[:, :N_EXPERTS].set(w_router.astype(BF16))
    w, m1, m2, rank, totals = route_tokens(h, wr)
    pw, meta = dispatch_plan(w, m1, m2, rank, totals, tile)
    pos1 = pw[:, 0].astype(I32)
    pos2 = pw[:, 1].astype(I32)
    tile_expert = meta[0, :n_tiles].astype(I32)
    n_used = meta[1, :1].astype(I32)
    xg = dispatch_rows(x, pos1, pos2, n_tiles * tile)
    y = moe_experts(xg, norm_gain.astype(F32), tile_expert, n_used, w_gate.astype(BF16), w_up.astype(BF16),
                    w_down.astype(BF16), tm=tile)
    return combine_rows(x, pw, y, pos1, pos2, next_gain.astype(F32), hn_dtype)


def kernel(x, positions, norm_mix, w_in, conv_w, a_log, dt_bias, dn_norm, w_dn_out, w_sa_out, w_o, norm_ffn, dense_w_gate, dense_w_up, dense_w_down, moe_router, moe_w_gate, moe_w_up, moe_w_down, final_norm):
    b, s, d = x.shape
    depth = norm_mix.shape[0]
    outs = []
    for bi in range(b):
        xb = x[bi]
        pos = positions[bi]
        cos_sa, sin_sa = _rope_tables(pos, SA_HEAD_DIM, 1)
        cos_ix, sin_ix = _rope_tables(pos, IDX_HEAD_DIM, 2)
        h = rmsnorm(xb, norm_mix[0], BF16)
        for layer in range(depth):
            xb, h = _mixer(xb, h, cos_sa, sin_sa, cos_ix, sin_ix, norm_ffn[layer], w_in[layer], conv_w[layer],
                           a_log[layer], dt_bias[layer], dn_norm[layer], w_dn_out[layer], w_sa_out[layer], w_o[layer])
            last = layer == depth - 1
            next_gain = (final_norm if last else norm_mix[layer + 1]).astype(F32)
            hn_dtype = x.dtype if last else BF16
            j = layer // 2
            if layer % 2 == 0:
                xb, h = ffn_dense(h, xb, dense_w_gate[j].astype(BF16), dense_w_up[j].astype(BF16),
                                  dense_w_down[j].astype(BF16), next_gain, hn_dtype)
            else:
                xb, h = _moe_layer(xb, h, norm_ffn[layer], moe_router[j], moe_w_gate[j], moe_w_up[j], moe_w_down[j],
                                   next_gain, hn_dtype)
        outs.append(h)
    return jnp.stack(outs, axis=0)
```

```python
import functools

import jax
import jax.numpy as jnp
from jax import lax
from jax.experimental import pallas as pl
from jax.experimental.pallas import tpu as pltpu

F32 = jnp.float32
BF16 = jnp.bfloat16
I32 = jnp.int32
I16 = jnp.int16

RMS_EPS = 1e-6
L2_EPS = 1e-6
DN_HEADS = 8
DN_HEAD_DIM = 128
DN_WIDTH = DN_HEADS * DN_HEAD_DIM
CONV_WIDTH = 4
DN_CHUNK = 64
SA_HEADS = 8
SA_HEAD_DIM = 128
SA_WIDTH = SA_HEADS * SA_HEAD_DIM
IDX_HEADS = 16
IDX_HEAD_DIM = 64
TOPK_MAX = 256
ROPE_THETA = 10000.0
N_EXPERTS = 8

LANES = 128
SUBLANES = 8
VMEM_CAP_BYTES = 56 * 2**20
NEG_BIG = -1e30
I16_MIN = -2**15
I16_ROWS = 16
LOG2_E = 1.4426950408889634

COL_QKV = 0
COL_Z = 3072
COL_QSA = 4096
COL_KSA = 5120
COL_VSA = 6144
COL_QIX = 7168
COL_GDN = 8192
COL_GSA = 10240
MAIN_WIDTH = 12288
SM_KIX = 0
SM_A = 64
SM_B = 72
SM_WIX = 80


def _params(semantics, vmem_bytes):
    return pltpu.CompilerParams(dimension_semantics=semantics,
                                vmem_limit_bytes=int(min(max(vmem_bytes, 16 * 2**20), VMEM_CAP_BYTES)))


def _nbytes(shape, dtype):
    n = 1
    for s in shape:
        n *= s
    return n * jnp.dtype(dtype).itemsize


def _sigmoid(x):
    return 1.0 / (1.0 + jnp.exp(-x))


def _dot(a, b):
    return jnp.dot(a, b, preferred_element_type=F32)


def _dot_nt(a, b):
    return lax.dot_general(a, b, (((1,), (1,)), ((), ())), preferred_element_type=F32)


def _dot_tn(a, b):
    return lax.dot_general(a, b, (((0,), (0,)), ((), ())), preferred_element_type=F32)


def _split_bf16(x):
    hi = x.astype(BF16)
    return hi, (x - hi.astype(F32)).astype(BF16)


def _dot3(a, b):
    return _dot(a[0], b[0]) + _dot(a[1], b[0]) + _dot(a[0], b[1])


def _rms(x, gain):
    ms = jnp.mean(x * x, axis=-1, keepdims=True)
    return x * lax.rsqrt(ms + RMS_EPS) * gain


def _rmsnorm_body(x_ref, g_ref, o_ref):
    o_ref[...] = _rms(x_ref[...], g_ref[...]).astype(o_ref.dtype)


def rmsnorm(x, gain, out_dtype, tm=512):
    m, d = x.shape
    vm = 2 * (_nbytes((tm, d), F32) + _nbytes((tm, d), out_dtype)) + 4 * _nbytes((tm, d), F32)
    return pl.pallas_call(
        _rmsnorm_body,
        grid=(m // tm,),
        in_specs=[pl.BlockSpec((tm, d), lambda i: (i, 0)),
                  pl.BlockSpec((1, d), lambda i: (0, 0))],
        out_specs=pl.BlockSpec((tm, d), lambda i: (i, 0)),
        out_shape=jax.ShapeDtypeStruct((m, d), out_dtype),
        compiler_params=_params(("parallel",), vm),
        name="rmsnorm",
    )(x, gain.reshape(1, d))


def _mm_body(a_ref, b_ref, o_ref):
    o_ref[...] = _dot(a_ref[...], b_ref[...]).astype(o_ref.dtype)


def matmul(a, b, out_dtype, tm, tn, name):
    m, k = a.shape
    n = b.shape[1]
    vm = 2 * (_nbytes((tm, k), a.dtype) + _nbytes((k, tn), b.dtype) + _nbytes((tm, tn), out_dtype))
    vm += 2 * _nbytes((tm, tn), F32)
    return pl.pallas_call(
        _mm_body,
        grid=(m // tm, n // tn),
        in_specs=[pl.BlockSpec((tm, k), lambda i, j: (i, 0)),
                  pl.BlockSpec((k, tn), lambda i, j: (0, j))],
        out_specs=pl.BlockSpec((tm, tn), lambda i, j: (i, j)),
        out_shape=jax.ShapeDtypeStruct((m, n), out_dtype),
        compiler_params=_params(("parallel", "parallel"), vm),
        name=name,
    )(a, b)


def _dn_chunk_body(xc_ref, xp_ref, sm_ref, cw_ref, alog_ref, dtb_ref,
                   qd_ref, kd_ref, w_ref, u_ref, att_ref, gl_ref, ext_ref):
    c = pl.program_id(0)
    C = DN_CHUNK
    halo = SUBLANES
    ext_ref[0:halo, :] = jnp.where(c > 0, xp_ref[...], 0.0)
    ext_ref[halo:halo + C, :] = xc_ref[...]
    cw = cw_ref[...]
    y = cw[0:1, :] * ext_ref[pl.ds(halo - CONV_WIDTH + 1, C), :]
    for j in range(1, CONV_WIDTH):
        y = y + cw[j:j + 1, :] * ext_ref[pl.ds(halo - CONV_WIDTH + 1 + j, C), :]
    y = y * _sigmoid(y)

    sm = sm_ref[...]
    xa = sm + dtb_ref[...]
    softplus = jnp.maximum(xa, 0.0) + jnp.log1p(jnp.exp(-jnp.abs(xa)))
    g = -jnp.exp(alog_ref[...]) * softplus
    beta = _sigmoid(sm)

    row = lax.broadcasted_iota(I32, (C, LANES), 0)
    gc = g
    d = 1
    while d < C:
        gc = gc + jnp.where(row >= d, pltpu.roll(gc, d, axis=0), 0.0)
        d *= 2
    gct = jnp.concatenate([gc, jnp.zeros_like(gc)], axis=0).T
    ex = jnp.exp(gc)
    gc_last = gc[C - 1:C, :]
    exl = jnp.exp(gc_last - gc)
    gl_ref[...] = jnp.exp(jnp.broadcast_to(gct[SM_A:SM_A + DN_HEADS, C - 1:C], (DN_HEADS, LANES)))

    ri = lax.broadcasted_iota(I32, (C, C), 0)
    ci = lax.broadcasted_iota(I32, (C, C), 1)
    tril = ri >= ci
    strict = ri > ci
    eye = jnp.where(ri == ci, 1.0, 0.0).astype(F32)
    lvl_masks = []
    lb = 0
    while (1 << lb) < C:
        lvl_masks.append(((ri >> (lb + 1)) == (ci >> (lb + 1)))
                         & (((ri >> lb) & 1) == 1) & (((ci >> lb) & 1) == 0))
        lb += 1

    heads = range(DN_HEADS)
    a_mats, kbs, vbs, excols = [], [], [], []
    for h in heads:
        sl = slice(h * DN_HEAD_DIM, (h + 1) * DN_HEAD_DIM)
        qh = y[:, h * DN_HEAD_DIM:(h + 1) * DN_HEAD_DIM]
        kh = y[:, DN_WIDTH + h * DN_HEAD_DIM:DN_WIDTH + (h + 1) * DN_HEAD_DIM]
        vh = y[:, 2 * DN_WIDTH + h * DN_HEAD_DIM:2 * DN_WIDTH + (h + 1) * DN_HEAD_DIM]
        qn = qh * lax.rsqrt(jnp.sum(qh * qh, axis=-1, keepdims=True) + L2_EPS) * (DN_HEAD_DIM ** -0.5)
        kn = kh * lax.rsqrt(jnp.sum(kh * kh, axis=-1, keepdims=True) + L2_EPS)
        bcol = beta[:, SM_B + h:SM_B + h + 1]
        gcol = gc[:, SM_A + h:SM_A + h + 1]
        grow = gct[SM_A + h:SM_A + h + 1, 0:C]
        dec = jnp.exp(jnp.where(tril, gcol - grow, -jnp.inf))
        kb = kn * bcol
        knb = kn.astype(BF16)
        excol = ex[:, SM_A + h:SM_A + h + 1]
        a_mats.append(jnp.where(strict, _dot_nt(kb.astype(BF16), knb) * dec, 0.0))
        kbs.append(kb * excol)
        vbs.append(vh * bcol)
        att = jnp.where(tril, _dot_nt(qn.astype(BF16), knb) * dec, 0.0)
        att_ref[:, h * C:(h + 1) * C] = att.astype(att_ref.dtype)
        qd_ref[:, sl] = (qn * excol).astype(qd_ref.dtype)
        kd_ref[:, sl] = (kn * exl[:, SM_A + h:SM_A + h + 1]).astype(kd_ref.dtype)

    x_inv = [eye - jnp.where(lvl_masks[0], a, 0.0) for a in a_mats]
    for lm in lvl_masks[1:]:
        xs = [_split_bf16(x) for x in x_inv]
        ts = [_dot3(xs[h], _split_bf16(jnp.where(lm, a_mats[h], 0.0))) for h in heads]
        x_inv = [x_inv[h] - _dot3(_split_bf16(ts[h]), xs[h]) for h in heads]
    xs = [_split_bf16(x) for x in x_inv]
    for h in heads:
        sl = slice(h * DN_HEAD_DIM, (h + 1) * DN_HEAD_DIM)
        w_ref[:, sl] = _dot3(xs[h], _split_bf16(kbs[h])).astype(w_ref.dtype)
        u_ref[:, sl] = _dot3(xs[h], _split_bf16(vbs[h]))


def dn_chunk(proj, small, conv_w, alog_pad, dtb_pad):
    s = proj.shape[0]
    C = DN_CHUNK
    n_chunks = s // C
    w3 = 3 * DN_WIDTH
    row_spec = lambda width, dt: pl.BlockSpec((C, width), lambda c: (c, 0))
    vm = 2 * (_nbytes((C, w3), F32) + _nbytes((SUBLANES, w3), F32)) + 8 * _nbytes((C, w3), F32)
    return pl.pallas_call(
        _dn_chunk_body,
        grid=(n_chunks,),
        in_specs=[pl.BlockSpec((C, w3), lambda c: (c, COL_QKV // w3)),
                  pl.BlockSpec((SUBLANES, w3), lambda c: (jnp.maximum(c * (C // SUBLANES) - 1, 0), COL_QKV // w3)),
                  pl.BlockSpec((C, LANES), lambda c: (c, 0)),
                  pl.BlockSpec((CONV_WIDTH, w3), lambda c: (0, 0)),
                  pl.BlockSpec((1, LANES), lambda c: (0, 0)),
                  pl.BlockSpec((1, LANES), lambda c: (0, 0))],
        out_specs=[row_spec(DN_WIDTH, BF16), row_spec(DN_WIDTH, BF16), row_spec(DN_WIDTH, BF16),
                   row_spec(DN_WIDTH, F32), row_spec(DN_HEADS * C, BF16),
                   pl.BlockSpec((DN_HEADS, LANES), lambda c: (c, 0))],
        out_shape=[jax.ShapeDtypeStruct((s, DN_WIDTH), BF16),
                   jax.ShapeDtypeStruct((s, DN_WIDTH), BF16),
                   jax.ShapeDtypeStruct((s, DN_WIDTH), BF16),
                   jax.ShapeDtypeStruct((s, DN_WIDTH), F32),
                   jax.ShapeDtypeStruct((s, DN_HEADS * C), BF16),
                   jax.ShapeDtypeStruct((n_chunks * DN_HEADS, LANES), F32)],
        scratch_shapes=[pltpu.VMEM((SUBLANES + C, w3), F32)],
        compiler_params=_params(("parallel",), vm),
        name="dn_chunk",
    )(proj, proj, small, conv_w, alog_pad, dtb_pad)


def _dn_scan_body(qd_ref, kd_ref, w_ref, u_ref, att_ref, gl_ref, z_ref, nrm_ref, o_ref, st_ref):
    c = pl.program_id(0)
    C = DN_CHUNK

    @pl.when(c == 0)
    def _():
        st_ref[...] = jnp.zeros_like(st_ref)

    heads = range(DN_HEADS)
    sls = [slice(h * DN_HEAD_DIM, (h + 1) * DN_HEAD_DIM) for h in heads]
    states = [st_ref[h] for h in heads]
    sbs = [s.astype(BF16) for s in states]
    w_s = [_dot(w_ref[:, sls[h]], sbs[h]) for h in heads]
    q_s = [_dot(qd_ref[:, sls[h]], sbs[h]) for h in heads]
    vbs = [(u_ref[:, sls[h]] - w_s[h]).astype(BF16) for h in heads]
    outs = [q_s[h] + _dot(att_ref[:, h * C:(h + 1) * C], vbs[h]) for h in heads]
    for h in heads:
        st_ref[h] = states[h] * gl_ref[h:h + 1, :] + _dot_tn(kd_ref[:, sls[h]], vbs[h])
    for h in heads:
        o = outs[h]
        ms = jnp.mean(o * o, axis=-1, keepdims=True)
        z = z_ref[:, sls[h]]
        o_ref[:, sls[h]] = (o * lax.rsqrt(ms + RMS_EPS) * nrm_ref[...] * (z * _sigmoid(z))).astype(o_ref.dtype)


def dn_scan(qd, kd, w, u, att, gl, proj, dn_norm):
    s = qd.shape[0]
    C = DN_CHUNK
    row = lambda width: pl.BlockSpec((C, width), lambda c: (c, 0))
    vm = 2 * 6 * _nbytes((C, DN_WIDTH), F32) + 2 * _nbytes((DN_HEADS, DN_HEAD_DIM, DN_HEAD_DIM), F32)
    return pl.pallas_call(
        _dn_scan_body,
        grid=(s // C,),
        in_specs=[row(DN_WIDTH), row(DN_WIDTH), row(DN_WIDTH), row(DN_WIDTH), row(DN_HEADS * C),
                  pl.BlockSpec((DN_HEADS, LANES), lambda c: (c, 0)),
                  pl.BlockSpec((C, DN_WIDTH), lambda c: (c, COL_Z // DN_WIDTH)),
                  pl.BlockSpec((1, DN_HEAD_DIM), lambda c: (0, 0))],
        out_specs=row(DN_WIDTH),
        out_shape=jax.ShapeDtypeStruct((s, DN_WIDTH), BF16),
        scratch_shapes=[pltpu.VMEM((DN_HEADS, DN_HEAD_DIM, DN_HEAD_DIM), F32)],
        compiler_params=_params(("arbitrary",), vm),
        name="dn_scan",
    )(qd, kd, w, u, att, gl, proj, dn_norm.reshape(1, DN_HEAD_DIM))


def _rope_sa_body(q_ref, k_ref, v_ref, c_ref, s_ref, qo_ref, ko_ref, vo_ref):
    cs = c_ref[...]
    sn = s_ref[...]
    scale = SA_HEAD_DIM ** -0.5 * LOG2_E
    for h in range(SA_HEADS):
        sl = slice(h * SA_HEAD_DIM, (h + 1) * SA_HEAD_DIM)
        x = q_ref[:, sl]
        qo_ref[:, sl] = ((x * cs + pltpu.roll(x, SA_HEAD_DIM // 2, axis=1) * sn) * scale).astype(qo_ref.dtype)
        x = k_ref[:, sl]
        ko_ref[:, sl] = (x * cs + pltpu.roll(x, SA_HEAD_DIM // 2, axis=1) * sn).astype(ko_ref.dtype)
    vo_ref[...] = v_ref[...].T.astype(vo_ref.dtype)


def rope_sa(proj, cos_t, sin_t, tm=512):
    s = proj.shape[0]
    tm = min(tm, s)
    col = lambda off: pl.BlockSpec((tm, SA_WIDTH), lambda i: (i, off // SA_WIDTH))
    tab = pl.BlockSpec((tm, LANES), lambda i: (i, 0))
    out = pl.BlockSpec((tm, SA_WIDTH), lambda i: (i, 0))
    vm = 2 * 3 * (_nbytes((tm, SA_WIDTH), F32) + _nbytes((tm, SA_WIDTH), BF16)) + 4 * _nbytes((tm, SA_WIDTH), F32)
    return pl.pallas_call(
        _rope_sa_body,
        grid=(s // tm,),
        in_specs=[col(COL_QSA), col(COL_KSA), col(COL_VSA), tab, tab],
        out_specs=[out, out, pl.BlockSpec((SA_WIDTH, tm), lambda i: (0, i))],
        out_shape=[jax.ShapeDtypeStruct((s, SA_WIDTH), BF16), jax.ShapeDtypeStruct((s, SA_WIDTH), BF16),
                   jax.ShapeDtypeStruct((SA_WIDTH, s), BF16)],
        compiler_params=_params(("parallel",), vm),
        name="rope_sa",
    )(proj, proj, proj, cos_t, sin_t)


def _rope_ix_body(q_ref, sm_ref, c_ref, s_ref, qo_ref, klo_ref, khi_ref):
    cs = c_ref[...]
    sn = s_ref[...]
    lane = lax.broadcasted_iota(I32, cs.shape, 1)
    first = (lane & (IDX_HEAD_DIM - 1)) < IDX_HEAD_DIM // 2
    half = IDX_HEAD_DIM // 2

    def rot(x):
        swapped = jnp.where(first, pltpu.roll(x, LANES - half, axis=1), pltpu.roll(x, half, axis=1))
        return x * cs + swapped * sn

    for j in range(IDX_HEADS * IDX_HEAD_DIM // LANES):
        sl = slice(j * LANES, (j + 1) * LANES)
        qo_ref[:, sl] = rot(q_ref[:, sl]).astype(qo_ref.dtype)
    k_lo = jnp.where(lane < IDX_HEAD_DIM, rot(sm_ref[...]), 0.0)
    klo_ref[...] = k_lo.astype(klo_ref.dtype)
    khi_ref[...] = pltpu.roll(k_lo, IDX_HEAD_DIM, axis=1).astype(khi_ref.dtype)


def rope_ix(proj, small, cos_t, sin_t, tm=512):
    s = proj.shape[0]
    wq = IDX_HEADS * IDX_HEAD_DIM
    tab = pl.BlockSpec((tm, LANES), lambda i: (i, 0))
    vm = 2 * (_nbytes((tm, wq), F32) + _nbytes((tm, wq), BF16)) + 4 * _nbytes((tm, wq), F32)
    return pl.pallas_call(
        _rope_ix_body,
        grid=(s // tm,),
        in_specs=[pl.BlockSpec((tm, wq), lambda i: (i, COL_QIX // wq)), tab, tab, tab],
        out_specs=[pl.BlockSpec((tm, wq), lambda i: (i, 0)), tab, tab],
        out_shape=[jax.ShapeDtypeStruct((s, wq), BF16),
                   jax.ShapeDtypeStruct((s, LANES), BF16),
                   jax.ShapeDtypeStruct((s, LANES), BF16)],
        compiler_params=_params(("parallel",), vm),
        name="rope_ix",
    )(proj, small, cos_t, sin_t)


def _index_body(q_ref, klo_ref, khi_ref, sm_ref, mask_ref, keys_ref, hi_ref, lo_ref, *, tq, tk, topk):
    i = pl.program_id(0)
    nkt = keys_ref.shape[0]
    nk = ((i + 1) * tq + tk - 1) // tk
    w_t = (sm_ref[...] * (IDX_HEADS ** -0.5 * IDX_HEAD_DIM ** -0.5)).T
    key_l = lax.broadcasted_iota(I32, (tk, tq), 0)
    qry_g = i * tq + lax.broadcasted_iota(I32, (tk, tq), 1)

    def score_tile(kt, carry):
        off = pl.multiple_of(kt * tk, tk)
        k_lo = klo_ref[pl.ds(off, tk), :]
        k_hi = khi_ref[pl.ds(off, tk), :]
        acc = jnp.zeros((tk, tq), F32)
        for j in range(IDX_HEADS // 2):
            qp = q_ref[:, j * LANES:(j + 1) * LANES]
            acc = acc + w_t[SM_WIX + 2 * j:SM_WIX + 2 * j + 1, :] * jnp.maximum(_dot_nt(k_lo, qp), 0.0)
            acc = acc + w_t[SM_WIX + 2 * j + 1:SM_WIX + 2 * j + 2, :] * jnp.maximum(_dot_nt(k_hi, qp), 0.0)
        sc = jnp.where(kt * tk + key_l <= qry_g, acc, -jnp.inf)
        bits = pltpu.bitcast(sc, I32)
        key = bits ^ ((bits >> 31) & 0x7FFFFFFF)
        keys_ref[kt] = key
        hi_ref[kt] = (key >> 16).astype(I16)
        return carry

    lax.fori_loop(0, nk, score_tile, 0)

    cnt_rows = 2 * I16_ROWS

    def count16(ref, cand, strict):
        def body(kt, cnt):
            t = ref[kt]
            hit = jnp.where((t > cand) if strict else (t >= cand), jnp.int16(1), jnp.int16(0))
            for j in range(tk // cnt_rows):
                cnt = cnt + hit[j * cnt_rows:(j + 1) * cnt_rows]
            return cnt

        cnt = lax.fori_loop(0, nk, body, jnp.zeros((cnt_rows, tq), I16))
        return jnp.sum(cnt.astype(F32), axis=0, keepdims=True)

    def kth_largest16(ref, kth):
        zero = jnp.zeros((1, tq), I32)
        ans = jnp.where(count16(ref, zero.astype(I16), False) >= kth, zero, I16_MIN)

        def bit_body(b, ans):
            cand = ans + lax.shift_left(jnp.int32(1), 14 - b)
            return jnp.where(count16(ref, cand.astype(I16), False) >= kth, cand, ans)

        return lax.fori_loop(0, 15, bit_body, ans)

    kf = jnp.full((1, tq), float(topk), F32)
    t_hi = kth_largest16(hi_ref, kf)
    above = count16(hi_ref, t_hi.astype(I16), True)

    def low_tile(kt, carry):
        key = keys_ref[kt]
        lo = (key & 0xFFFF) + I16_MIN
        lo_ref[kt] = jnp.where((key >> 16) == t_hi, lo, I16_MIN).astype(I16)
        return carry

    lax.fori_loop(0, nk, low_tile, 0)
    t_lo = kth_largest16(lo_ref, kf - above)
    ans = t_hi * 65536 + (t_lo - I16_MIN)

    def put(kt, sel):
        mask_ref[0, pl.ds(pl.multiple_of(kt * tk, tk), tk), :] = jnp.where(sel, 0.0, NEG_BIG).astype(mask_ref.dtype)

    def write(kt, cnt):
        sel = (keys_ref[kt] >= ans) & (kt * tk + key_l <= qry_g)
        put(kt, sel)
        return cnt + jnp.sum(jnp.where(sel, 1.0, 0.0).reshape(tk // cnt_rows, cnt_rows, tq), axis=0)

    kept = jnp.sum(lax.fori_loop(0, nk, write, jnp.zeros((cnt_rows, tq), F32)), axis=0, keepdims=True)

    @pl.when(jnp.max(kept) > float(topk))
    def _():
        def count_gt(kt, cnt):
            gt = (keys_ref[kt] > ans) & (kt * tk + key_l <= qry_g)
            return cnt + jnp.sum(jnp.where(gt, 1.0, 0.0), axis=0, keepdims=True)

        need = kf - lax.fori_loop(0, nk, count_gt, jnp.zeros((1, tq), F32))
        ri = lax.broadcasted_iota(I32, (tk, tk), 0)
        ci = lax.broadcasted_iota(I32, (tk, tk), 1)
        upto = jnp.where(ri >= ci, 1.0, 0.0).astype(BF16)

        def rewrite(kt, seen):
            key = keys_ref[kt]
            causal = kt * tk + key_l <= qry_g
            tie = (key == ans) & causal
            rank = seen + _dot(upto, jnp.where(tie, 1.0, 0.0).astype(BF16))
            put(kt, ((key > ans) & causal) | (tie & (rank <= need)))
            return rank[tk - 1:tk, :]

        lax.fori_loop(0, nk, rewrite, jnp.zeros((1, tq), F32))

    def clear(kt, carry):
        mask_ref[0, pl.ds(pl.multiple_of(kt * tk, tk), tk), :] = jnp.full((tk, tq), NEG_BIG, mask_ref.dtype)
        return carry

    lax.fori_loop(nk, nkt, clear, 0)


def index_mask(q_ix, k_lo, k_hi, small, topk, tq, tk, t_attn):
    s = q_ix.shape[0]
    nkt = s // tk
    per = t_attn // tq
    wq = IDX_HEADS * IDX_HEAD_DIM
    vm = (2 * (_nbytes((tq, wq), BF16) + 2 * _nbytes((s, LANES), BF16) + _nbytes((tq, LANES), F32)
               + _nbytes((s, tq), BF16)) + 2 * _nbytes((s, tq), I32) + 8 * _nbytes((tk, tq), F32))
    return pl.pallas_call(
        functools.partial(_index_body, tq=tq, tk=tk, topk=topk),
        grid=(s // tq,),
        in_specs=[pl.BlockSpec((tq, wq), lambda i: (i, 0)),
                  pl.BlockSpec((s, LANES), lambda i: (0, 0)),
                  pl.BlockSpec((s, LANES), lambda i: (0, 0)),
                  pl.BlockSpec((tq, LANES), lambda i: (i, 0))],
        out_specs=pl.BlockSpec((1, s, tq), lambda i: (i // per, 0, i % per)),
        out_shape=jax.ShapeDtypeStruct((s // t_attn, s, t_attn), BF16),
        scratch_shapes=[pltpu.VMEM((nkt, tk, tq), I32), pltpu.VMEM((nkt, tk, tq), I16), pltpu.VMEM((nkt, tk, tq), I16)],
        compiler_params=_params(("parallel",), vm),
        name="index_mask",
    )(q_ix, k_lo, k_hi, small)


def _attn_body(qi_ref, ki_ref, q_ref, k_ref, vt_ref, mk_ref, o_ref, m_ref, l_ref, al_ref, acc_ref, bias_ref, s_ref):
    p = pl.program_id(0)
    qi = qi_ref[p]
    ki = ki_ref[p]

    @pl.when(ki == 0)
    def _():
        m_ref[...] = jnp.full_like(m_ref, NEG_BIG)
        l_ref[...] = jnp.zeros_like(l_ref)
        acc_ref[...] = jnp.zeros_like(acc_ref)

    bias_ref[...] = mk_ref[0].astype(F32)
    tk, tq = bias_ref.shape
    part = 4 * SUBLANES

    def fold(x, op):
        return op(op(x.reshape(tk // part, part, tq), axis=0), axis=0, keepdims=True)

    heads = range(SA_HEADS)
    sls = [slice(h * SA_HEAD_DIM, (h + 1) * SA_HEAD_DIM) for h in heads]
    for h in heads:
        s = _dot_nt(k_ref[:, sls[h]], q_ref[:, sls[h]]) + bias_ref[...]
        s_ref[h] = s
        m_old = m_ref[h:h + 1, :]
        m_new = jnp.maximum(m_old, fold(s, jnp.max))
        al_ref[h:h + 1, :] = jnp.exp2(m_old - m_new)
        m_ref[h:h + 1, :] = m_new
    for h in heads:
        pr = jnp.exp2(s_ref[h] - m_ref[h:h + 1, :])
        alpha = al_ref[h:h + 1, :]
        l_ref[h:h + 1, :] = alpha * l_ref[h:h + 1, :] + fold(pr, jnp.sum)
        acc_ref[h] = alpha * acc_ref[h] + _dot(vt_ref[sls[h], :], pr.astype(BF16))

    @pl.when(ki == qi)
    def _():
        for h in range(SA_HEADS):
            sl = slice(h * SA_HEAD_DIM, (h + 1) * SA_HEAD_DIM)
            o_ref[:, sl] = (acc_ref[h] / l_ref[h:h + 1, :]).T.astype(o_ref.dtype)


def masked_attention(q, k, vt, mask_t, t):
    s = q.shape[0]
    nb = s // t
    pairs = [(a, b) for a in range(nb) for b in range(a + 1)]
    qi = jnp.asarray([a for a, _ in pairs], I32)
    ki = jnp.asarray([b for _, b in pairs], I32)
    vm = (2 * (4 * _nbytes((t, SA_WIDTH), BF16) + _nbytes((t, t), BF16)) + _nbytes((t, SA_WIDTH), F32)
          + 2 * _nbytes((SA_HEADS, t), F32) + 8 * _nbytes((t, t), F32))
    grid_spec = pltpu.PrefetchScalarGridSpec(
        num_scalar_prefetch=2,
        grid=(len(pairs),),
        in_specs=[pl.BlockSpec((t, SA_WIDTH), lambda p, qi, ki: (qi[p], 0)),
                  pl.BlockSpec((t, SA_WIDTH), lambda p, qi, ki: (ki[p], 0)),
                  pl.BlockSpec((SA_WIDTH, t), lambda p, qi, ki: (0, ki[p])),
                  pl.BlockSpec((1, t, t), lambda p, qi, ki: (qi[p], ki[p], 0))],
        out_specs=pl.BlockSpec((t, SA_WIDTH), lambda p, qi, ki: (qi[p], 0)),
        scratch_shapes=[pltpu.VMEM((SA_HEADS, t), F32),
                        pltpu.VMEM((SA_HEADS, t), F32),
                        pltpu.VMEM((SA_HEADS, t), F32),
                        pltpu.VMEM((SA_HEADS, SA_HEAD_DIM, t), F32),
                        pltpu.VMEM((t, t), F32),
                        pltpu.VMEM((SA_HEADS, t, t), F32)],
    )
    return pl.pallas_call(
        _attn_body,
        grid_spec=grid_spec,
        out_shape=jax.ShapeDtypeStruct((s, SA_WIDTH), BF16),
        compiler_params=_params(("arbitrary",), vm),
        name="masked_attention",
    )(qi, ki, q, k, vt, mask_t)


def _merge_body(odn_ref, osa_ref, wdn_ref, wsa_ref, gdn_ref, gsa_ref, o_ref):
    y_dn = _dot(odn_ref[...], wdn_ref[...])
    y_sa = _dot(osa_ref[...], wsa_ref[...])
    o_ref[...] = (_sigmoid(gdn_ref[...]) * y_dn + _sigmoid(gsa_ref[...]) * y_sa).astype(o_ref.dtype)


def merge_branches(o_dn, o_sa, w_dn, w_sa, proj, tm=512, tn=512):
    s, kd = o_dn.shape
    d = w_dn.shape[1]
    vm = 2 * (2 * _nbytes((tm, kd), BF16) + 2 * _nbytes((kd, tn), BF16) + 2 * _nbytes((tm, tn), F32)
              + _nbytes((tm, tn), BF16)) + 6 * _nbytes((tm, tn), F32)
    return pl.pallas_call(
        _merge_body,
        grid=(s // tm, d // tn),
        in_specs=[pl.BlockSpec((tm, kd), lambda i, j: (i, 0)),
                  pl.BlockSpec((tm, kd), lambda i, j: (i, 0)),
                  pl.BlockSpec((kd, tn), lambda i, j: (0, j)),
                  pl.BlockSpec((kd, tn), lambda i, j: (0, j)),
                  pl.BlockSpec((tm, tn), lambda i, j: (i, COL_GDN // tn + j)),
                  pl.BlockSpec((tm, tn), lambda i, j: (i, COL_GSA // tn + j))],
        out_specs=pl.BlockSpec((tm, tn), lambda i, j: (i, j)),
        out_shape=jax.ShapeDtypeStruct((s, d), BF16),
        compiler_params=_params(("parallel", "parallel"), vm),
        name="merge_branches",
    )(o_dn, o_sa, w_dn, w_sa, proj, proj)


def _out_proj_body(m_ref, w_ref, x_ref, gn_ref, o_ref, hn_ref):
    y = x_ref[...] + _dot(m_ref[...], w_ref[...])
    o_ref[...] = y
    hn_ref[...] = _rms(y, gn_ref[...]).astype(hn_ref.dtype)


def out_proj_norm(merged, w_o, x, next_gain, tm=512):
    s, k = merged.shape
    d = w_o.shape[1]
    tm = min(tm, s)
    vm = 2 * (_nbytes((tm, k), BF16) + _nbytes((k, d), BF16) + 2 * _nbytes((tm, d), F32) + _nbytes((tm, d), BF16)) + 3 * _nbytes((tm, d), F32)
    row = lambda width: pl.BlockSpec((tm, width), lambda i: (i, 0))
    return pl.pallas_call(
        _out_proj_body,
        grid=(s // tm,),
        in_specs=[row(k), pl.BlockSpec((k, d), lambda i: (0, 0)), row(d), pl.BlockSpec((1, d), lambda i: (0, 0))],
        out_specs=[row(d), row(d)],
        out_shape=[jax.ShapeDtypeStruct((s, d), F32), jax.ShapeDtypeStruct((s, d), BF16)],
        compiler_params=_params(("parallel",), vm),
        name="out_proj",
    )(merged, w_o, x, next_gain.reshape(1, d))


def _ffn_body(h_ref, x_ref, wg_ref, wu_ref, wd_ref, gn_ref, o_ref, hn_ref):
    f = pl.program_id(1)

    @pl.when(f == 0)
    def _():
        o_ref[...] = jnp.zeros_like(o_ref)

    h = h_ref[...]
    g = _dot(h, wg_ref[...])
    u = _dot(h, wu_ref[...])
    o_ref[...] += _dot((g * _sigmoid(g) * u).astype(BF16), wd_ref[...])

    @pl.when(f == pl.num_programs(1) - 1)
    def _():
        y = x_ref[...] + o_ref[...]
        o_ref[...] = y
        hn_ref[...] = _rms(y, gn_ref[...]).astype(hn_ref.dtype)


def ffn_dense(h, x, w_gate, w_up, w_down, next_gain, hn_dtype, tm=512, tf=512):
    s, d = h.shape
    ff = w_gate.shape[1]
    vm = (2 * (_nbytes((tm, d), BF16) + _nbytes((tm, d), hn_dtype) + 2 * _nbytes((tm, d), F32)
               + 3 * _nbytes((d, tf), BF16)) + 3 * _nbytes((tm, d), F32) + 6 * _nbytes((tm, tf), F32))
    row = pl.BlockSpec((tm, d), lambda i, f: (i, 0))
    return pl.pallas_call(
        _ffn_body,
        grid=(s // tm, ff // tf),
        in_specs=[row, row,
                  pl.BlockSpec((d, tf), lambda i, f: (0, f)),
                  pl.BlockSpec((d, tf), lambda i, f: (0, f)),
                  pl.BlockSpec((tf, d), lambda i, f: (f, 0)),
                  pl.BlockSpec((1, d), lambda i, f: (0, 0))],
        out_specs=[row, row],
        out_shape=[jax.ShapeDtypeStruct((s, d), F32), jax.ShapeDtypeStruct((s, d), hn_dtype)],
        compiler_params=_params(("parallel", "arbitrary"), vm),
        name="ffn_dense",
    )(h, x, w_gate, w_up, w_down, next_gain.reshape(1, d))


def _router_body(h_ref, wr_ref, w_ref, m1_ref, m2_ref, rank_ref, tot_ref, cnt_ref):
    logits = _dot(h_ref[...], wr_ref[...])
    lane = lax.broadcasted_iota(I32, logits.shape, 1)
    lg = jnp.where(lane < N_EXPERTS, logits, -jnp.inf)
    m1 = jnp.max(lg, axis=1, keepdims=True)
    i1 = jnp.min(jnp.where(lg == m1, lane, LANES), axis=1, keepdims=True)
    lg2 = jnp.where(lane == i1, -jnp.inf, lg)
    m2 = jnp.max(lg2, axis=1, keepdims=True)
    i2 = jnp.min(jnp.where(lg2 == m2, lane, LANES), axis=1, keepdims=True)
    e = jnp.exp(m2 - m1)
    first = lane == i1
    second = lane == i2
    w_ref[...] = jnp.where(first, 1.0 / (1.0 + e), 0.0) + jnp.where(second, e / (1.0 + e), 0.0)
    m1_ref[...] = jnp.where(first, 1.0, 0.0)
    m2_ref[...] = jnp.where(second, 1.0, 0.0)

    @pl.when(pl.program_id(0) == 0)
    def _():
        cnt_ref[...] = jnp.zeros_like(cnt_ref)

    tm = logits.shape[0]
    sel = jnp.where(first | second, 1.0, 0.0).astype(BF16)
    ri = lax.broadcasted_iota(I32, (tm, tm), 0)
    ci = lax.broadcasted_iota(I32, (tm, tm), 1)
    before = jnp.where(ri > ci, 1.0, 0.0).astype(BF16)
    run = cnt_ref[0:1, :]
    rank_ref[...] = run + _dot(before, sel)
    run = run + jnp.sum(sel.astype(F32), axis=0, keepdims=True)
    cnt_ref[...] = jnp.broadcast_to(run, cnt_ref.shape)
    tot_ref[...] = jnp.broadcast_to(run, tot_ref.shape)


def route_tokens(h, w_router_pad, tm=512):
    s, d = h.shape
    tm = min(tm, s)
    vm = 2 * (_nbytes((tm, d), BF16) + _nbytes((d, LANES), BF16) + 4 * _nbytes((tm, LANES), F32)) + 4 * _nbytes((tm, tm), F32)
    row = pl.BlockSpec((tm, LANES), lambda i: (i, 0))
    return pl.pallas_call(
        _router_body,
        grid=(s // tm,),
        in_specs=[pl.BlockSpec((tm, d), lambda i: (i, 0)),
                  pl.BlockSpec((d, LANES), lambda i: (0, 0))],
        out_specs=[row, row, row, row, pl.BlockSpec((SUBLANES, LANES), lambda i: (0, 0))],
        out_shape=[jax.ShapeDtypeStruct((s, LANES), F32)] * 4 + [jax.ShapeDtypeStruct((SUBLANES, LANES), F32)],
        scratch_shapes=[pltpu.VMEM((SUBLANES, LANES), F32)],
        compiler_params=_params(("arbitrary",), vm),
        name="moe_router",
    )(h, w_router_pad)


def _plan_body(w_ref, m1_ref, m2_ref, rank_ref, tot_ref, pw_ref, meta_ref, *, tile):
    lane8 = lax.broadcasted_iota(I32, (SUBLANES, LANES), 1)
    n = tot_ref[...]
    padded = jnp.floor((n + (tile - 1.0)) * (1.0 / tile)) * tile
    ends = padded
    d = 1
    while d < N_EXPERTS:
        ends = ends + jnp.where(lane8 >= d, pltpu.roll(ends, d, axis=1), 0.0)
        d *= 2
    start = (ends - padded)[0:1, :]
    posf = start + rank_ref[...]
    m1 = m1_ref[...]
    m2 = m2_ref[...]
    w = w_ref[...]
    lane = lax.broadcasted_iota(I32, w.shape, 1)
    cols = [jnp.sum(m1 * posf, axis=1, keepdims=True), jnp.sum(m2 * posf, axis=1, keepdims=True),
            jnp.sum(m1 * w, axis=1, keepdims=True), jnp.sum(m2 * w, axis=1, keepdims=True)]
    out = jnp.zeros_like(w)
    for j, col in enumerate(cols):
        out = jnp.where(lane == j, col, out)
    pw_ref[...] = out
    tile_start = lane8.astype(F32) * tile
    owner = jnp.zeros((SUBLANES, LANES), F32)
    for e in range(N_EXPERTS - 1):
        owner = owner + jnp.where(tile_start >= ends[:, e:e + 1], 1.0, 0.0)
    n_used = ends[:, N_EXPERTS - 1:N_EXPERTS] * (1.0 / tile)
    row8 = lax.broadcasted_iota(I32, (SUBLANES, LANES), 0)
    meta_ref[...] = jnp.where(row8 == 0, owner, jnp.broadcast_to(n_used, owner.shape))


def dispatch_plan(w, m1, m2, rank, totals, tile):
    s = w.shape[0]
    full = pl.BlockSpec((s, LANES), lambda: (0, 0))
    small = pl.BlockSpec((SUBLANES, LANES), lambda: (0, 0))
    vm = 2 * 5 * _nbytes((s, LANES), F32) + 8 * _nbytes((s, LANES), F32)
    return pl.pallas_call(
        functools.partial(_plan_body, tile=float(tile)),
        in_specs=[full, full, full, full, small],
        out_specs=[full, small],
        out_shape=[jax.ShapeDtypeStruct((s, LANES), F32), jax.ShapeDtypeStruct((SUBLANES, LANES), F32)],
        compiler_params=pltpu.CompilerParams(vmem_limit_bytes=int(min(vm, VMEM_CAP_BYTES))),
        name="moe_plan",
    )(w, m1, m2, rank, totals)


def _row_copy(src_ref, src_row, dst_ref, dst_row, sem):
    return pltpu.make_async_copy(src_ref.at[pl.ds(src_row, 1), :], dst_ref.at[pl.ds(dst_row, 1), :], sem)


def _dispatch_body(p1_ref, p2_ref, x_ref, xg_in_ref, xg_ref, sem, *, tt):
    del xg_in_ref
    base = pl.program_id(0) * tt

    def issue(r, carry):
        _row_copy(x_ref, r, xg_ref, p1_ref[base + r], sem).start()
        _row_copy(x_ref, r, xg_ref, p2_ref[base + r], sem).start()
        return carry

    lax.fori_loop(0, tt, issue, 0)

    def drain(r, carry):
        _row_copy(x_ref, 0, xg_ref, 0, sem).wait()
        _row_copy(x_ref, 0, xg_ref, 0, sem).wait()
        return carry

    lax.fori_loop(0, tt, drain, 0)


def dispatch_rows(x, pos1, pos2, n_rows, tt=256):
    s, d = x.shape
    tt = min(tt, s)
    grid_spec = pltpu.PrefetchScalarGridSpec(
        num_scalar_prefetch=2,
        grid=(s // tt,),
        in_specs=[pl.BlockSpec((tt, d), lambda i, p1, p2: (i, 0)),
                  pl.BlockSpec(memory_space=pl.ANY)],
        out_specs=pl.BlockSpec(memory_space=pl.ANY),
        scratch_shapes=[pltpu.SemaphoreType.DMA(())],
    )
    return pl.pallas_call(
        functools.partial(_dispatch_body, tt=tt),
        grid_spec=grid_spec,
        out_shape=jax.ShapeDtypeStruct((n_rows, d), x.dtype),
        input_output_aliases={3: 0},
        compiler_params=_params(("arbitrary",), 4 * _nbytes((tt, d), F32)),
        name="moe_dispatch",
    )(pos1, pos2, x, jnp.zeros((n_rows, d), x.dtype))


def _combine_body(p1_ref, p2_ref, x_ref, pw_ref, gn_ref, y_ref, o_ref, hn_ref, b1_ref, b2_ref, sem, *, tt):
    base = pl.program_id(0) * tt

    def issue(r, carry):
        _row_copy(y_ref, p1_ref[base + r], b1_ref, r, sem).start()
        _row_copy(y_ref, p2_ref[base + r], b2_ref, r, sem).start()
        return carry

    lax.fori_loop(0, tt, issue, 0)

    def drain(r, carry):
        _row_copy(y_ref, 0, b1_ref, 0, sem).wait()
        _row_copy(y_ref, 0, b2_ref, 0, sem).wait()
        return carry

    lax.fori_loop(0, tt, drain, 0)
    pw = pw_ref[...]
    y = x_ref[...] + pw[:, 2:3] * b1_ref[...] + pw[:, 3:4] * b2_ref[...]
    o_ref[...] = y
    hn_ref[...] = _rms(y, gn_ref[...]).astype(hn_ref.dtype)


def combine_rows(x, pw, y, pos1, pos2, next_gain, hn_dtype, tt=256):
    s, d = x.shape
    tt = min(tt, s)
    grid_spec = pltpu.PrefetchScalarGridSpec(
        num_scalar_prefetch=2,
        grid=(s // tt,),
        in_specs=[pl.BlockSpec((tt, d), lambda i, p1, p2: (i, 0)),
                  pl.BlockSpec((tt, LANES), lambda i, p1, p2: (i, 0)),
                  pl.BlockSpec((1, d), lambda i, p1, p2: (0, 0)),
                  pl.BlockSpec(memory_space=pl.ANY)],
        out_specs=[pl.BlockSpec((tt, d), lambda i, p1, p2: (i, 0)), pl.BlockSpec((tt, d), lambda i, p1, p2: (i, 0))],
        scratch_shapes=[pltpu.VMEM((tt, d), F32), pltpu.VMEM((tt, d), F32), pltpu.SemaphoreType.DMA(())],
    )
    return pl.pallas_call(
        functools.partial(_combine_body, tt=tt),
        grid_spec=grid_spec,
        out_shape=[jax.ShapeDtypeStruct((s, d), F32), jax.ShapeDtypeStruct((s, d), hn_dtype)],
        compiler_params=_params(("arbitrary",), 12 * _nbytes((tt, d), F32)),
        name="moe_combine",
    )(pos1, pos2, x, pw, next_gain.reshape(1, d), y)


def _moe_body(te_ref, nu_ref, xg_ref, gain_ref, wg_ref, wu_ref, wd_ref, o_ref, h_ref, acc_ref):
    i = pl.program_id(0)
    f = pl.program_id(1)

    @pl.when(i < nu_ref[0])
    def _():
        @pl.when(f == 0)
        def _():
            h_ref[...] = _rms(xg_ref[...], gain_ref[...]).astype(h_ref.dtype)
            acc_ref[...] = jnp.zeros_like(acc_ref)

        h = h_ref[...]
        g = _dot(h, wg_ref[0])
        u = _dot(h, wu_ref[0])
        acc_ref[...] += _dot((g * _sigmoid(g) * u).astype(BF16), wd_ref[0])

        @pl.when(f == pl.num_programs(1) - 1)
        def _():
            o_ref[...] = acc_ref[...]

    @pl.when((i >= nu_ref[0]) & (f == 0))
    def _():
        o_ref[...] = jnp.zeros_like(o_ref)


def moe_experts(xg, gain, tile_expert, n_used, w_gate, w_up, w_down, tm, tf=256):
    p, d = xg.shape
    ff = w_gate.shape[2]
    nf = ff // tf
    vm = (2 * (2 * _nbytes((tm, d), F32) + 3 * _nbytes((d, tf), BF16)) + _nbytes((tm, d), BF16)
          + _nbytes((tm, d), F32) + 6 * _nbytes((tm, tf), F32))

    def row(i, f, te, nu):
        return (jnp.minimum(i, nu[0] - 1), 0)

    def fcol(i, f, nu):
        return jnp.where(i < nu[0], f, nf - 1)

    grid_spec = pltpu.PrefetchScalarGridSpec(
        num_scalar_prefetch=2,
        grid=(p // tm, nf),
        in_specs=[pl.BlockSpec((tm, d), row),
                  pl.BlockSpec((1, d), lambda i, f, te, nu: (0, 0)),
                  pl.BlockSpec((1, d, tf), lambda i, f, te, nu: (te[i], 0, fcol(i, f, nu))),
                  pl.BlockSpec((1, d, tf), lambda i, f, te, nu: (te[i], 0, fcol(i, f, nu))),
                  pl.BlockSpec((1, tf, d), lambda i, f, te, nu: (te[i], fcol(i, f, nu), 0))],
        out_specs=pl.BlockSpec((tm, d), lambda i, f, te, nu: (i, 0)),
        scratch_shapes=[pltpu.VMEM((tm, d), BF16), pltpu.VMEM((tm, d), F32)],
    )
    return pl.pallas_call(
        _moe_body,
        grid_spec=grid_spec,
        out_shape=jax.ShapeDtypeStruct((p, d), F32),
        compiler_params=_params(("arbitrary", "arbitrary"), vm),
        name="moe_experts",
    )(tile_expert, n_used, xg, gain.reshape(1, d), w_gate, w_up, w_down)


def _rope_tables(positions, dim, reps):
    inv_freq = ROPE_THETA ** (-jnp.arange(0, dim, 2, dtype=F32) / dim)
    ang = positions.astype(F32)[:, None] * inv_freq
    cos, sin = jnp.cos(ang), jnp.sin(ang)
    return jnp.tile(cos, (1, 2 * reps)), jnp.tile(jnp.concatenate([-sin, sin], axis=1), (1, reps))


def _split_w_in(w):
    d = w.shape[0]
    o_a = 4 * DN_WIDTH
    o_qsa = o_a + 2 * DN_HEADS
    o_kix = o_qsa + 3 * SA_WIDTH + IDX_HEADS * IDX_HEAD_DIM
    o_wix = o_kix + IDX_HEAD_DIM
    o_g = o_wix + IDX_HEADS
    main = jnp.concatenate([w[:, :o_a], w[:, o_qsa:o_kix], w[:, o_g:]], axis=1)
    small = jnp.concatenate([w[:, o_kix:o_wix], w[:, o_a:o_qsa], w[:, o_wix:o_g],
                             jnp.zeros((d, LANES - IDX_HEAD_DIM - 2 * DN_HEADS - IDX_HEADS), w.dtype)], axis=1)
    return main.astype(BF16), small.astype(BF16)


def _pad_lanes(v, offset):
    return jnp.zeros((1, LANES), F32).at[0, offset:offset + v.shape[0]].set(v.astype(F32))


def _mixer(x, h, cos_sa, sin_sa, cos_ix, sin_ix, ffn_gain, w_in, conv_w, a_log, dt_bias, dn_norm,
           w_dn_out, w_sa_out, w_o):
    s = x.shape[0]
    w_main, w_small = _split_w_in(w_in)
    proj = matmul(h, w_main, F32, tm=min(1024, s), tn=512, name="in_proj")
    small = matmul(h, w_small, F32, tm=min(1024, s), tn=LANES, name="in_proj_small")

    qd, kd, w, u, att, gl = dn_chunk(proj, small, conv_w.astype(F32), _pad_lanes(a_log, SM_A), _pad_lanes(dt_bias, SM_A))
    o_dn = dn_scan(qd, kd, w, u, att, gl, proj, dn_norm.astype(F32))

    q_sa, k_sa, vt_sa = rope_sa(proj, cos_sa, sin_sa)
    q_ix, k_lo, k_hi = rope_ix(proj, small, cos_ix, sin_ix)
    t_attn = min(512, s)
    mask_t = index_mask(q_ix, k_lo, k_hi, small, min(TOPK_MAX, s // 4), tq=min(256, s), tk=t_attn, t_attn=t_attn)
    o_sa = masked_attention(q_sa, k_sa, vt_sa, mask_t, t=t_attn)

    merged = merge_branches(o_dn, o_sa, w_dn_out.astype(BF16), w_sa_out.astype(BF16), proj)
    return out_proj_norm(merged, w_o.astype(BF16), x, ffn_gain.astype(F32))


def _moe_layer(x, h, norm_gain, w_router, w_gate, w_up, w_down, next_gain, hn_dtype):
    s, d = x.shape
    tile = min(512, s)
    n_tiles = (2 * s) // tile + N_EXPERTS
    wr = jnp.zeros((d, LANES), BF16).at[:, :N_EXPERTS].set(w_router.astype(BF16))
    w, m1, m2, rank, totals = route_tokens(h, wr)
    pw, meta = dispatch_plan(w, m1, m2, rank, totals, tile)
    pos1 = pw[:, 0].astype(I32)
    pos2 = pw[:, 1].astype(I32)
    tile_expert = meta[0, :n_tiles].astype(I32)
    n_used = meta[1, :1].astype(I32)
    xg = dispatch_rows(x, pos1, pos2, n_tiles * tile)
    y = moe_experts(xg, norm_gain.astype(F32), tile_expert, n_used, w_gate.astype(BF16), w_up.astype(BF16),
                    w_down.astype(BF16), tm=tile)
    return combine_rows(x, pw, y, pos1, pos2, next_gain.astype(F32), hn_dtype)


def kernel(x, positions, norm_mix, w_in, conv_w, a_log, dt_bias, dn_norm, w_dn_out, w_sa_out, w_o, norm_ffn, dense_w_gate, dense_w_up, dense_w_down, moe_router, moe_w_gate, moe_w_up, moe_w_down, final_norm):
    b, s, d = x.shape
    depth = norm_mix.shape[0]
    outs = []
    for bi in range(b):
        xb = x[bi]
        pos = positions[bi]
        cos_sa, sin_sa = _rope_tables(pos, SA_HEAD_DIM, 1)
        cos_ix, sin_ix = _rope_tables(pos, IDX_HEAD_DIM, 2)
        h = rmsnorm(xb, norm_mix[0], BF16)
        for layer in range(depth):
            xb, h = _mixer(xb, h, cos_sa, sin_sa, cos_ix, sin_ix, norm_ffn[layer], w_in[layer], conv_w[layer],
                           a_log[layer], dt_bias[layer], dn_norm[layer], w_dn_out[layer], w_sa_out[layer], w_o[layer])
            last = layer == depth - 1
            next_gain = (final_norm if last else norm_mix[layer + 1]).astype(F32)
            hn_dtype = x.dtype if last else BF16
            j = layer // 2
            if layer % 2 == 0:
                xb, h = ffn_dense(h, xb, dense_w_gate[j].astype(BF16), dense_w_up[j].astype(BF16),
                                  dense_w_down[j].astype(BF16), next_gain, hn_dtype)
            else:
                xb, h = _moe_layer(xb, h, norm_ffn[layer], moe_router[j], moe_w_gate[j], moe_w_up[j], moe_w_down[j],
                                   next_gain, hn_dtype)
        outs.append(h)
    return jnp.stack(outs, axis=0)
```

```python
import functools

import jax
import jax.numpy as jnp
from jax import lax
from jax.experimental import pallas as pl
from jax.experimental.pallas import tpu as pltpu

F32 = jnp.float32
BF16 = jnp.bfloat16
I32 = jnp.int32
I16 = jnp.int16

RMS_EPS = 1e-6
L2_EPS = 1e-6
DN_HEADS = 8
DN_HEAD_DIM = 128
DN_WIDTH = DN_HEADS * DN_HEAD_DIM
CONV_WIDTH = 4
DN_CHUNK = 64
SA_HEADS = 8
SA_HEAD_DIM = 128
SA_WIDTH = SA_HEADS * SA_HEAD_DIM
IDX_HEADS = 16
IDX_HEAD_DIM = 64
TOPK_MAX = 256
ROPE_THETA = 10000.0
N_EXPERTS = 8

LANES = 128
SUBLANES = 8
VMEM_CAP_BYTES = 56 * 2**20
NEG_BIG = -1e30
I16_MIN = -2**15
I16_ROWS = 16
LOG2_E = 1.4426950408889634

COL_QKV = 0
COL_Z = 3072
COL_GDN = 4096
COL_GSA = 6144
BLK_QSA, BLK_KSA, BLK_VSA, BLK_QIX = 0, 1, 2, 3
SM_KIX = 0
SM_A = 64
SM_B = 72
SM_WIX = 80


def _params(semantics, vmem_bytes):
    return pltpu.CompilerParams(dimension_semantics=semantics,
                                vmem_limit_bytes=int(min(max(vmem_bytes, 16 * 2**20), VMEM_CAP_BYTES)))


def _nbytes(shape, dtype):
    n = 1
    for s in shape:
        n *= s
    return n * jnp.dtype(dtype).itemsize


def _sigmoid(x):
    return 1.0 / (1.0 + jnp.exp(-x))


def _dot(a, b):
    return jnp.dot(a, b, preferred_element_type=F32)


def _dot_nt(a, b):
    return lax.dot_general(a, b, (((1,), (1,)), ((), ())), preferred_element_type=F32)


def _dot_tn(a, b):
    return lax.dot_general(a, b, (((0,), (0,)), ((), ())), preferred_element_type=F32)


def _split_bf16(x):
    hi = x.astype(BF16)
    return hi, (x - hi.astype(F32)).astype(BF16)


def _dot3(a, b):
    return _dot(a[0], b[0]) + _dot(a[1], b[0]) + _dot(a[0], b[1])


def _rms(x, gain):
    ms = jnp.mean(x * x, axis=-1, keepdims=True)
    return x * lax.rsqrt(ms + RMS_EPS) * gain


def _rmsnorm_body(x_ref, g_ref, o_ref):
    o_ref[...] = _rms(x_ref[...], g_ref[...]).astype(o_ref.dtype)


def rmsnorm(x, gain, out_dtype, tm=512):
    m, d = x.shape
    vm = 2 * (_nbytes((tm, d), F32) + _nbytes((tm, d), out_dtype)) + 4 * _nbytes((tm, d), F32)
    return pl.pallas_call(
        _rmsnorm_body,
        grid=(m // tm,),
        in_specs=[pl.BlockSpec((tm, d), lambda i: (i, 0)),
                  pl.BlockSpec((1, d), lambda i: (0, 0))],
        out_specs=pl.BlockSpec((tm, d), lambda i: (i, 0)),
        out_shape=jax.ShapeDtypeStruct((m, d), out_dtype),
        compiler_params=_params(("parallel",), vm),
        name="rmsnorm",
    )(x, gain.reshape(1, d))


def _mm_body(a_ref, b_ref, o_ref):
    o_ref[...] = _dot(a_ref[...], b_ref[...]).astype(o_ref.dtype)


def matmul(a, b, out_dtype, tm, tn, name):
    m, k = a.shape
    n = b.shape[1]
    vm = 2 * (_nbytes((tm, k), a.dtype) + _nbytes((k, tn), b.dtype) + _nbytes((tm, tn), out_dtype))
    vm += 2 * _nbytes((tm, tn), F32)
    return pl.pallas_call(
        _mm_body,
        grid=(m // tm, n // tn),
        in_specs=[pl.BlockSpec((tm, k), lambda i, j: (i, 0)),
                  pl.BlockSpec((k, tn), lambda i, j: (0, j))],
        out_specs=pl.BlockSpec((tm, tn), lambda i, j: (i, j)),
        out_shape=jax.ShapeDtypeStruct((m, n), out_dtype),
        compiler_params=_params(("parallel", "parallel"), vm),
        name=name,
    )(a, b)


def _dn_chunk_body(xc_ref, xp_ref, sm_ref, cw_ref, alog_ref, dtb_ref,
                   qd_ref, kd_ref, w_ref, u_ref, att_ref, gl_ref, ext_ref):
    c = pl.program_id(0)
    C = DN_CHUNK
    halo = SUBLANES
    ext_ref[0:halo, :] = jnp.where(c > 0, xp_ref[...], 0.0)
    ext_ref[halo:halo + C, :] = xc_ref[...]
    cw = cw_ref[...]
    y = cw[0:1, :] * ext_ref[pl.ds(halo - CONV_WIDTH + 1, C), :]
    for j in range(1, CONV_WIDTH):
        y = y + cw[j:j + 1, :] * ext_ref[pl.ds(halo - CONV_WIDTH + 1 + j, C), :]
    y = y * _sigmoid(y)

    sm = sm_ref[...]
    xa = sm + dtb_ref[...]
    softplus = jnp.maximum(xa, 0.0) + jnp.log1p(jnp.exp(-jnp.abs(xa)))
    g = -jnp.exp(alog_ref[...]) * softplus
    beta = _sigmoid(sm)

    row = lax.broadcasted_iota(I32, (C, LANES), 0)
    gc = g
    d = 1
    while d < C:
        gc = gc + jnp.where(row >= d, pltpu.roll(gc, d, axis=0), 0.0)
        d *= 2
    gct = jnp.concatenate([gc, jnp.zeros_like(gc)], axis=0).T
    ex = jnp.exp(gc)
    gc_last = gc[C - 1:C, :]
    exl = jnp.exp(gc_last - gc)
    gl_ref[...] = jnp.exp(jnp.broadcast_to(gct[SM_A:SM_A + DN_HEADS, C - 1:C], (DN_HEADS, LANES)))

    ri = lax.broadcasted_iota(I32, (C, C), 0)
    ci = lax.broadcasted_iota(I32, (C, C), 1)
    tril = ri >= ci
    strict = ri > ci
    eye = jnp.where(ri == ci, 1.0, 0.0).astype(F32)
    lvl_masks = []
    lb = 0
    while (1 << lb) < C:
        lvl_masks.append(((ri >> (lb + 1)) == (ci >> (lb + 1)))
                         & (((ri >> lb) & 1) == 1) & (((ci >> lb) & 1) == 0))
        lb += 1

    heads = range(DN_HEADS)
    a_mats, kbs, vbs, excols = [], [], [], []
    for h in heads:
        sl = slice(h * DN_HEAD_DIM, (h + 1) * DN_HEAD_DIM)
        qh = y[:, h * DN_HEAD_DIM:(h + 1) * DN_HEAD_DIM]
        kh = y[:, DN_WIDTH + h * DN_HEAD_DIM:DN_WIDTH + (h + 1) * DN_HEAD_DIM]
        vh = y[:, 2 * DN_WIDTH + h * DN_HEAD_DIM:2 * DN_WIDTH + (h + 1) * DN_HEAD_DIM]
        qn = qh * lax.rsqrt(jnp.sum(qh * qh, axis=-1, keepdims=True) + L2_EPS) * (DN_HEAD_DIM ** -0.5)
        kn = kh * lax.rsqrt(jnp.sum(kh * kh, axis=-1, keepdims=True) + L2_EPS)
        bcol = beta[:, SM_B + h:SM_B + h + 1]
        gcol = gc[:, SM_A + h:SM_A + h + 1]
        grow = gct[SM_A + h:SM_A + h + 1, 0:C]
        dec = jnp.exp(jnp.where(tril, gcol - grow, -jnp.inf))
        kb = kn * bcol
        knb = kn.astype(BF16)
        excol = ex[:, SM_A + h:SM_A + h + 1]
        a_mats.append(jnp.where(strict, _dot_nt(kb.astype(BF16), knb) * dec, 0.0))
        kbs.append(kb * excol)
        vbs.append(vh * bcol)
        att = jnp.where(tril, _dot_nt(qn.astype(BF16), knb) * dec, 0.0)
        att_ref[:, h * C:(h + 1) * C] = att.astype(att_ref.dtype)
        qd_ref[:, sl] = (qn * excol).astype(qd_ref.dtype)
        kd_ref[:, sl] = (kn * exl[:, SM_A + h:SM_A + h + 1]).astype(kd_ref.dtype)

    x_inv = [eye - jnp.where(lvl_masks[0], a, 0.0) for a in a_mats]
    for lm in lvl_masks[1:]:
        xs = [_split_bf16(x) for x in x_inv]
        ts = [_dot3(xs[h], _split_bf16(jnp.where(lm, a_mats[h], 0.0))) for h in heads]
        x_inv = [x_inv[h] - _dot3(_split_bf16(ts[h]), xs[h]) for h in heads]
    xs = [_split_bf16(x) for x in x_inv]
    for h in heads:
        sl = slice(h * DN_HEAD_DIM, (h + 1) * DN_HEAD_DIM)
        w_ref[:, sl] = _dot3(xs[h], _split_bf16(kbs[h])).astype(w_ref.dtype)
        u_ref[:, sl] = _dot3(xs[h], _split_bf16(vbs[h]))


def dn_chunk(proj, small, conv_w, alog_pad, dtb_pad):
    s = proj.shape[0]
    C = DN_CHUNK
    n_chunks = s // C
    w3 = 3 * DN_WIDTH
    row_spec = lambda width, dt: pl.BlockSpec((C, width), lambda c: (c, 0))
    vm = 2 * (_nbytes((C, w3), F32) + _nbytes((SUBLANES, w3), F32)) + 8 * _nbytes((C, w3), F32)
    return pl.pallas_call(
        _dn_chunk_body,
        grid=(n_chunks,),
        in_specs=[pl.BlockSpec((C, w3), lambda c: (c, COL_QKV // w3)),
                  pl.BlockSpec((SUBLANES, w3), lambda c: (jnp.maximum(c * (C // SUBLANES) - 1, 0), COL_QKV // w3)),
                  pl.BlockSpec((C, LANES), lambda c: (c, 0)),
                  pl.BlockSpec((CONV_WIDTH, w3), lambda c: (0, 0)),
                  pl.BlockSpec((1, LANES), lambda c: (0, 0)),
                  pl.BlockSpec((1, LANES), lambda c: (0, 0))],
        out_specs=[row_spec(DN_WIDTH, BF16), row_spec(DN_WIDTH, BF16), row_spec(DN_WIDTH, BF16),
                   row_spec(DN_WIDTH, F32), row_spec(DN_HEADS * C, BF16),
                   pl.BlockSpec((DN_HEADS, LANES), lambda c: (c, 0))],
        out_shape=[jax.ShapeDtypeStruct((s, DN_WIDTH), BF16),
                   jax.ShapeDtypeStruct((s, DN_WIDTH), BF16),
                   jax.ShapeDtypeStruct((s, DN_WIDTH), BF16),
                   jax.ShapeDtypeStruct((s, DN_WIDTH), F32),
                   jax.ShapeDtypeStruct((s, DN_HEADS * C), BF16),
                   jax.ShapeDtypeStruct((n_chunks * DN_HEADS, LANES), F32)],
        scratch_shapes=[pltpu.VMEM((SUBLANES + C, w3), F32)],
        compiler_params=_params(("parallel",), vm),
        name="dn_chunk",
    )(proj, proj, small, conv_w, alog_pad, dtb_pad)


def _dn_scan_body(qd_ref, kd_ref, w_ref, u_ref, att_ref, gl_ref, z_ref, nrm_ref, o_ref, st_ref):
    c = pl.program_id(0)
    C = DN_CHUNK

    @pl.when(c == 0)
    def _():
        st_ref[...] = jnp.zeros_like(st_ref)

    heads = range(DN_HEADS)
    sls = [slice(h * DN_HEAD_DIM, (h + 1) * DN_HEAD_DIM) for h in heads]
    states = [st_ref[h] for h in heads]
    sbs = [s.astype(BF16) for s in states]
    w_s = [_dot(w_ref[:, sls[h]], sbs[h]) for h in heads]
    q_s = [_dot(qd_ref[:, sls[h]], sbs[h]) for h in heads]
    vbs = [(u_ref[:, sls[h]] - w_s[h]).astype(BF16) for h in heads]
    outs = [q_s[h] + _dot(att_ref[:, h * C:(h + 1) * C], vbs[h]) for h in heads]
    for h in heads:
        st_ref[h] = states[h] * gl_ref[h:h + 1, :] + _dot_tn(kd_ref[:, sls[h]], vbs[h])
    for h in heads:
        o = outs[h]
        ms = jnp.mean(o * o, axis=-1, keepdims=True)
        z = z_ref[:, sls[h]]
        o_ref[:, sls[h]] = (o * lax.rsqrt(ms + RMS_EPS) * nrm_ref[...] * (z * _sigmoid(z))).astype(o_ref.dtype)


def dn_scan(qd, kd, w, u, att, gl, proj, dn_norm):
    s = qd.shape[0]
    C = DN_CHUNK
    row = lambda width: pl.BlockSpec((C, width), lambda c: (c, 0))
    vm = 2 * 6 * _nbytes((C, DN_WIDTH), F32) + 2 * _nbytes((DN_HEADS, DN_HEAD_DIM, DN_HEAD_DIM), F32)
    return pl.pallas_call(
        _dn_scan_body,
        grid=(s // C,),
        in_specs=[row(DN_WIDTH), row(DN_WIDTH), row(DN_WIDTH), row(DN_WIDTH), row(DN_HEADS * C),
                  pl.BlockSpec((DN_HEADS, LANES), lambda c: (c, 0)),
                  pl.BlockSpec((C, DN_WIDTH), lambda c: (c, COL_Z // DN_WIDTH)),
                  pl.BlockSpec((1, DN_HEAD_DIM), lambda c: (0, 0))],
        out_specs=row(DN_WIDTH),
        out_shape=jax.ShapeDtypeStruct((s, DN_WIDTH), BF16),
        scratch_shapes=[pltpu.VMEM((DN_HEADS, DN_HEAD_DIM, DN_HEAD_DIM), F32)],
        compiler_params=_params(("arbitrary",), vm),
        name="dn_scan",
    )(qd, kd, w, u, att, gl, proj, dn_norm.reshape(1, DN_HEAD_DIM))


def _rotate_half(x, cs, sn, head_dim, first):
    half = head_dim // 2
    if head_dim == LANES:
        swapped = pltpu.roll(x, half, axis=1)
    else:
        swapped = jnp.where(first, pltpu.roll(x, LANES - half, axis=1), pltpu.roll(x, half, axis=1))
    return x * cs + swapped * sn


def _proj_rope_body(a_ref, b_ref, c_ref, s_ref, o_ref, *, head_dim, scale):
    acc = _dot(a_ref[...], b_ref[...])
    cs = c_ref[...]
    sn = s_ref[...]
    lane = lax.broadcasted_iota(I32, cs.shape, 1)
    first = (lane & (head_dim - 1)) < head_dim // 2
    for j in range(acc.shape[1] // LANES):
        sl = slice(j * LANES, (j + 1) * LANES)
        o_ref[:, sl] = (_rotate_half(acc[:, sl], cs, sn, head_dim, first) * scale).astype(o_ref.dtype)


def proj_rope(h, w_att, col_block, cos_t, sin_t, head_dim, scale=1.0, tm=512):
    s, d = h.shape
    tm = min(tm, s)
    n = SA_WIDTH
    vm = 2 * (_nbytes((tm, d), BF16) + _nbytes((d, n), BF16) + _nbytes((tm, n), BF16)) + 4 * _nbytes((tm, n), F32)
    tab = pl.BlockSpec((tm, LANES), lambda i: (i, 0))
    return pl.pallas_call(
        functools.partial(_proj_rope_body, head_dim=head_dim, scale=scale),
        grid=(s // tm,),
        in_specs=[pl.BlockSpec((tm, d), lambda i: (i, 0)),
                  pl.BlockSpec((d, n), lambda i: (0, col_block)), tab, tab],
        out_specs=pl.BlockSpec((tm, n), lambda i: (i, 0)),
        out_shape=jax.ShapeDtypeStruct((s, n), BF16),
        compiler_params=_params(("parallel",), vm),
        name="proj_rope",
    )(h, w_att, cos_t, sin_t)


def _rope_kix_body(sm_ref, c_ref, s_ref, klo_ref, khi_ref):
    cs = c_ref[...]
    lane = lax.broadcasted_iota(I32, cs.shape, 1)
    first = (lane & (IDX_HEAD_DIM - 1)) < IDX_HEAD_DIM // 2
    k_rot = _rotate_half(sm_ref[...], cs, s_ref[...], IDX_HEAD_DIM, first)
    k_lo = jnp.where(lane < IDX_HEAD_DIM, k_rot, 0.0)
    klo_ref[...] = k_lo.astype(klo_ref.dtype)
    khi_ref[...] = pltpu.roll(k_lo, IDX_HEAD_DIM, axis=1).astype(khi_ref.dtype)


def rope_kix(small, cos_t, sin_t, tm=512):
    s = small.shape[0]
    tm = min(tm, s)
    tab = pl.BlockSpec((tm, LANES), lambda i: (i, 0))
    return pl.pallas_call(
        _rope_kix_body,
        grid=(s // tm,),
        in_specs=[tab, tab, tab],
        out_specs=[tab, tab],
        out_shape=[jax.ShapeDtypeStruct((s, LANES), BF16)] * 2,
        compiler_params=_params(("parallel",), 16 * _nbytes((tm, LANES), F32)),
        name="rope_kix",
    )(small, cos_t, sin_t)


def _index_body(q_ref, klo_ref, khi_ref, sm_ref, mask_ref, keys_ref, hi_ref, lo_ref, *, tq, tk, topk):
    i = pl.program_id(0)
    nkt = keys_ref.shape[0]
    nk = ((i + 1) * tq + tk - 1) // tk
    w_t = (sm_ref[...] * (IDX_HEADS ** -0.5 * IDX_HEAD_DIM ** -0.5)).T
    key_l = lax.broadcasted_iota(I32, (tk, tq), 0)
    qry_g = i * tq + lax.broadcasted_iota(I32, (tk, tq), 1)

    def score_tile(kt, carry):
        off = pl.multiple_of(kt * tk, tk)
        k_lo = klo_ref[pl.ds(off, tk), :]
        k_hi = khi_ref[pl.ds(off, tk), :]
        acc = jnp.zeros((tk, tq), F32)
        for j in range(IDX_HEADS // 2):
            qp = q_ref[:, j * LANES:(j + 1) * LANES]
            acc = acc + w_t[SM_WIX + 2 * j:SM_WIX + 2 * j + 1, :] * jnp.maximum(_dot_nt(k_lo, qp), 0.0)
            acc = acc + w_t[SM_WIX + 2 * j + 1:SM_WIX + 2 * j + 2, :] * jnp.maximum(_dot_nt(k_hi, qp), 0.0)
        sc = jnp.where(kt * tk + key_l <= qry_g, acc, -jnp.inf)
        bits = pltpu.bitcast(sc, I32)
        key = bits ^ ((bits >> 31) & 0x7FFFFFFF)
        keys_ref[kt] = key
        hi_ref[kt] = (key >> 16).astype(I16)
        return carry

    lax.fori_loop(0, nk, score_tile, 0)

    cnt_rows = 2 * I16_ROWS

    def count16(ref, cand, strict):
        def body(kt, cnt):
            t = ref[kt]
            hit = jnp.where((t > cand) if strict else (t >= cand), jnp.int16(1), jnp.int16(0))
            for j in range(tk // cnt_rows):
                cnt = cnt + hit[j * cnt_rows:(j + 1) * cnt_rows]
            return cnt

        cnt = lax.fori_loop(0, nk, body, jnp.zeros((cnt_rows, tq), I16))
        return jnp.sum(cnt.astype(F32), axis=0, keepdims=True)

    def kth_largest16(ref, kth):
        zero = jnp.zeros((1, tq), I32)
        ans = jnp.where(count16(ref, zero.astype(I16), False) >= kth, zero, I16_MIN)

        def bit_body(b, ans):
            cand = ans + lax.shift_left(jnp.int32(1), 14 - b)
            return jnp.where(count16(ref, cand.astype(I16), False) >= kth, cand, ans)

        return lax.fori_loop(0, 15, bit_body, ans)

    kf = jnp.full((1, tq), float(topk), F32)
    t_hi = kth_largest16(hi_ref, kf)
    above = count16(hi_ref, t_hi.astype(I16), True)

    def low_tile(kt, carry):
        key = keys_ref[kt]
        lo = (key & 0xFFFF) + I16_MIN
        lo_ref[kt] = jnp.where((key >> 16) == t_hi, lo, I16_MIN).astype(I16)
        return carry

    lax.fori_loop(0, nk, low_tile, 0)
    t_lo = kth_largest16(lo_ref, kf - above)
    ans = t_hi * 65536 + (t_lo - I16_MIN)

    def put(kt, sel):
        mask_ref[0, pl.ds(pl.multiple_of(kt * tk, tk), tk), :] = jnp.where(sel, 0.0, NEG_BIG).astype(mask_ref.dtype)

    def write(kt, cnt):
        sel = (keys_ref[kt] >= ans) & (kt * tk + key_l <= qry_g)
        put(kt, sel)
        return cnt + jnp.sum(jnp.where(sel, 1.0, 0.0).reshape(tk // cnt_rows, cnt_rows, tq), axis=0)

    kept = jnp.sum(lax.fori_loop(0, nk, write, jnp.zeros((cnt_rows, tq), F32)), axis=0, keepdims=True)

    @pl.when(jnp.max(kept) > float(topk))
    def _():
        def count_gt(kt, cnt):
            gt = (keys_ref[kt] > ans) & (kt * tk + key_l <= qry_g)
            return cnt + jnp.sum(jnp.where(gt, 1.0, 0.0), axis=0, keepdims=True)

        need = kf - lax.fori_loop(0, nk, count_gt, jnp.zeros((1, tq), F32))
        ri = lax.broadcasted_iota(I32, (tk, tk), 0)
        ci = lax.broadcasted_iota(I32, (tk, tk), 1)
        upto = jnp.where(ri >= ci, 1.0, 0.0).astype(BF16)

        def rewrite(kt, seen):
            key = keys_ref[kt]
            causal = kt * tk + key_l <= qry_g
            tie = (key == ans) & causal
            rank = seen + _dot(upto, jnp.where(tie, 1.0, 0.0).astype(BF16))
            put(kt, ((key > ans) & causal) | (tie & (rank <= need)))
            return rank[tk - 1:tk, :]

        lax.fori_loop(0, nk, rewrite, jnp.zeros((1, tq), F32))

    def clear(kt, carry):
        mask_ref[0, pl.ds(pl.multiple_of(kt * tk, tk), tk), :] = jnp.full((tk, tq), NEG_BIG, mask_ref.dtype)
        return carry

    lax.fori_loop(nk, nkt, clear, 0)


def index_mask(q_ix, k_lo, k_hi, small, topk, tq, tk, t_attn):
    s = q_ix.shape[0]
    nkt = s // tk
    per = t_attn // tq
    wq = IDX_HEADS * IDX_HEAD_DIM
    vm = (2 * (_nbytes((tq, wq), BF16) + 2 * _nbytes((s, LANES), BF16) + _nbytes((tq, LANES), F32)
               + _nbytes((s, tq), BF16)) + 2 * _nbytes((s, tq), I32) + 8 * _nbytes((tk, tq), F32))
    return pl.pallas_call(
        functools.partial(_index_body, tq=tq, tk=tk, topk=topk),
        grid=(s // tq,),
        in_specs=[pl.BlockSpec((tq, wq), lambda i: (i, 0)),
                  pl.BlockSpec((s, LANES), lambda i: (0, 0)),
                  pl.BlockSpec((s, LANES), lambda i: (0, 0)),
                  pl.BlockSpec((tq, LANES), lambda i: (i, 0))],
        out_specs=pl.BlockSpec((1, s, tq), lambda i: (i // per, 0, i % per)),
        out_shape=jax.ShapeDtypeStruct((s // t_attn, s, t_attn), BF16),
        scratch_shapes=[pltpu.VMEM((nkt, tk, tq), I32), pltpu.VMEM((nkt, tk, tq), I16), pltpu.VMEM((nkt, tk, tq), I16)],
        compiler_params=_params(("parallel",), vm),
        name="index_mask",
    )(q_ix, k_lo, k_hi, small)


def _attn_body(qi_ref, ki_ref, q_ref, k_ref, v_ref, mk_ref, o_ref, m_ref, l_ref, al_ref, acc_ref, bias_ref, s_ref):
    p = pl.program_id(0)
    qi = qi_ref[p]
    ki = ki_ref[p]

    @pl.when(ki == 0)
    def _():
        m_ref[...] = jnp.full_like(m_ref, NEG_BIG)
        l_ref[...] = jnp.zeros_like(l_ref)
        acc_ref[...] = jnp.zeros_like(acc_ref)

    bias_ref[...] = mk_ref[0].astype(F32)
    tk, tq = bias_ref.shape
    part = 4 * SUBLANES

    def fold(x, op):
        return op(op(x.reshape(tk // part, part, tq), axis=0), axis=0, keepdims=True)

    heads = range(SA_HEADS)
    sls = [slice(h * SA_HEAD_DIM, (h + 1) * SA_HEAD_DIM) for h in heads]
    for h in heads:
        s = _dot_nt(k_ref[:, sls[h]], q_ref[:, sls[h]]) + bias_ref[...]
        s_ref[h] = s
        m_old = m_ref[h:h + 1, :]
        m_new = jnp.maximum(m_old, fold(s, jnp.max))
        al_ref[h:h + 1, :] = jnp.exp2(m_old - m_new)
        m_ref[h:h + 1, :] = m_new
    for h in heads:
        pr = jnp.exp2(s_ref[h] - m_ref[h:h + 1, :])
        alpha = al_ref[h:h + 1, :]
        l_ref[h:h + 1, :] = alpha * l_ref[h:h + 1, :] + fold(pr, jnp.sum)
        acc_ref[h] = alpha * acc_ref[h] + _dot_tn(v_ref[:, sls[h]], pr.astype(BF16))

    @pl.when(ki == qi)
    def _():
        for h in range(SA_HEADS):
            sl = slice(h * SA_HEAD_DIM, (h + 1) * SA_HEAD_DIM)
            o_ref[:, sl] = (acc_ref[h] / l_ref[h:h + 1, :]).T.astype(o_ref.dtype)


def masked_attention(q, k, v, mask_t, t):
    s = q.shape[0]
    nb = s // t
    pairs = [(a, b) for a in range(nb) for b in range(a + 1)]
    qi = jnp.asarray([a for a, _ in pairs], I32)
    ki = jnp.asarray([b for _, b in pairs], I32)
    vm = (2 * (4 * _nbytes((t, SA_WIDTH), BF16) + _nbytes((t, t), BF16)) + _nbytes((t, SA_WIDTH), F32)
          + 2 * _nbytes((SA_HEADS, t), F32) + 8 * _nbytes((t, t), F32))
    grid_spec = pltpu.PrefetchScalarGridSpec(
        num_scalar_prefetch=2,
        grid=(len(pairs),),
        in_specs=[pl.BlockSpec((t, SA_WIDTH), lambda p, qi, ki: (qi[p], 0)),
                  pl.BlockSpec((t, SA_WIDTH), lambda p, qi, ki: (ki[p], 0)),
                  pl.BlockSpec((t, SA_WIDTH), lambda p, qi, ki: (ki[p], 0)),
                  pl.BlockSpec((1, t, t), lambda p, qi, ki: (qi[p], ki[p], 0))],
        out_specs=pl.BlockSpec((t, SA_WIDTH), lambda p, qi, ki: (qi[p], 0)),
        scratch_shapes=[pltpu.VMEM((SA_HEADS, t), F32),
                        pltpu.VMEM((SA_HEADS, t), F32),
                        pltpu.VMEM((SA_HEADS, t), F32),
                        pltpu.VMEM((SA_HEADS, SA_HEAD_DIM, t), F32),
                        pltpu.VMEM((t, t), F32),
                        pltpu.VMEM((SA_HEADS, t, t), F32)],
    )
    return pl.pallas_call(
        _attn_body,
        grid_spec=grid_spec,
        out_shape=jax.ShapeDtypeStruct((s, SA_WIDTH), BF16),
        compiler_params=_params(("arbitrary",), vm),
        name="masked_attention",
    )(qi, ki, q, k, v, mask_t)


def _merge_body(odn_ref, osa_ref, wdn_ref, wsa_ref, gdn_ref, gsa_ref, o_ref):
    y_dn = _dot(odn_ref[...], wdn_ref[...])
    y_sa = _dot(osa_ref[...], wsa_ref[...])
    o_ref[...] = (_sigmoid(gdn_ref[...]) * y_dn + _sigmoid(gsa_ref[...]) * y_sa).astype(o_ref.dtype)


def merge_branches(o_dn, o_sa, w_dn, w_sa, proj, tm=512, tn=512):
    s, kd = o_dn.shape
    d = w_dn.shape[1]
    vm = 2 * (2 * _nbytes((tm, kd), BF16) + 2 * _nbytes((kd, tn), BF16) + 2 * _nbytes((tm, tn), F32)
              + _nbytes((tm, tn), BF16)) + 6 * _nbytes((tm, tn), F32)
    return pl.pallas_call(
        _merge_body,
        grid=(s // tm, d // tn),
        in_specs=[pl.BlockSpec((tm, kd), lambda i, j: (i, 0)),
                  pl.BlockSpec((tm, kd), lambda i, j: (i, 0)),
                  pl.BlockSpec((kd, tn), lambda i, j: (0, j)),
                  pl.BlockSpec((kd, tn), lambda i, j: (0, j)),
                  pl.BlockSpec((tm, tn), lambda i, j: (i, COL_GDN // tn + j)),
                  pl.BlockSpec((tm, tn), lambda i, j: (i, COL_GSA // tn + j))],
        out_specs=pl.BlockSpec((tm, tn), lambda i, j: (i, j)),
        out_shape=jax.ShapeDtypeStruct((s, d), BF16),
        compiler_params=_params(("parallel", "parallel"), vm),
        name="merge_branches",
    )(o_dn, o_sa, w_dn, w_sa, proj, proj)


def _out_proj_body(m_ref, w_ref, x_ref, gn_ref, o_ref, hn_ref):
    y = x_ref[...] + _dot(m_ref[...], w_ref[...])
    o_ref[...] = y
    hn_ref[...] = _rms(y, gn_ref[...]).astype(hn_ref.dtype)


def out_proj_norm(merged, w_o, x, next_gain, tm=512):
    s, k = merged.shape
    d = w_o.shape[1]
    tm = min(tm, s)
    vm = 2 * (_nbytes((tm, k), BF16) + _nbytes((k, d), BF16) + 2 * _nbytes((tm, d), F32) + _nbytes((tm, d), BF16)) + 3 * _nbytes((tm, d), F32)
    row = lambda width: pl.BlockSpec((tm, width), lambda i: (i, 0))
    return pl.pallas_call(
        _out_proj_body,
        grid=(s // tm,),
        in_specs=[row(k), pl.BlockSpec((k, d), lambda i: (0, 0)), row(d), pl.BlockSpec((1, d), lambda i: (0, 0))],
        out_specs=[row(d), row(d)],
        out_shape=[jax.ShapeDtypeStruct((s, d), F32), jax.ShapeDtypeStruct((s, d), BF16)],
        compiler_params=_params(("parallel",), vm),
        name="out_proj",
    )(merged, w_o, x, next_gain.reshape(1, d))


def _ffn_body(h_ref, x_ref, wg_ref, wu_ref, wd_ref, gn_ref, o_ref, hn_ref):
    f = pl.program_id(1)

    @pl.when(f == 0)
    def _():
        o_ref[...] = jnp.zeros_like(o_ref)

    h = h_ref[...]
    g = _dot(h, wg_ref[...])
    u = _dot(h, wu_ref[...])
    o_ref[...] += _dot((g * _sigmoid(g) * u).astype(BF16), wd_ref[...])

    @pl.when(f == pl.num_programs(1) - 1)
    def _():
        y = x_ref[...] + o_ref[...]
        o_ref[...] = y
        hn_ref[...] = _rms(y, gn_ref[...]).astype(hn_ref.dtype)


def ffn_dense(h, x, w_gate, w_up, w_down, next_gain, hn_dtype, tm=512, tf=512):
    s, d = h.shape
    ff = w_gate.shape[1]
    vm = (2 * (_nbytes((tm, d), BF16) + _nbytes((tm, d), hn_dtype) + 2 * _nbytes((tm, d), F32)
               + 3 * _nbytes((d, tf), BF16)) + 3 * _nbytes((tm, d), F32) + 6 * _nbytes((tm, tf), F32))
    row = pl.BlockSpec((tm, d), lambda i, f: (i, 0))
    return pl.pallas_call(
        _ffn_body,
        grid=(s // tm, ff // tf),
        in_specs=[row, row,
                  pl.BlockSpec((d, tf), lambda i, f: (0, f)),
                  pl.BlockSpec((d, tf), lambda i, f: (0, f)),
                  pl.BlockSpec((tf, d), lambda i, f: (f, 0)),
                  pl.BlockSpec((1, d), lambda i, f: (0, 0))],
        out_specs=[row, row],
        out_shape=[jax.ShapeDtypeStruct((s, d), F32), jax.ShapeDtypeStruct((s, d), hn_dtype)],
        compiler_params=_params(("parallel", "arbitrary"), vm),
        name="ffn_dense",
    )(h, x, w_gate, w_up, w_down, next_gain.reshape(1, d))


def _router_body(h_ref, wr_ref, w_ref, m1_ref, m2_ref, rank_ref, tot_ref, cnt_ref):
    logits = _dot(h_ref[...], wr_ref[...])
    lane = lax.broadcasted_iota(I32, logits.shape, 1)
    lg = jnp.where(lane < N_EXPERTS, logits, -jnp.inf)
    m1 = jnp.max(lg, axis=1, keepdims=True)
    i1 = jnp.min(jnp.where(lg == m1, lane, LANES), axis=1, keepdims=True)
    lg2 = jnp.where(lane == i1, -jnp.inf, lg)
    m2 = jnp.max(lg2, axis=1, keepdims=True)
    i2 = jnp.min(jnp.where(lg2 == m2, lane, LANES), axis=1, keepdims=True)
    e = jnp.exp(m2 - m1)
    first = lane == i1
    second = lane == i2
    w_ref[...] = jnp.where(first, 1.0 / (1.0 + e), 0.0) + jnp.where(second, e / (1.0 + e), 0.0)
    m1_ref[...] = jnp.where(first, 1.0, 0.0)
    m2_ref[...] = jnp.where(second, 1.0, 0.0)

    @pl.when(pl.program_id(0) == 0)
    def _():
        cnt_ref[...] = jnp.zeros_like(cnt_ref)

    tm = logits.shape[0]
    sel = jnp.where(first | second, 1.0, 0.0).astype(BF16)
    ri = lax.broadcasted_iota(I32, (tm, tm), 0)
    ci = lax.broadcasted_iota(I32, (tm, tm), 1)
    before = jnp.where(ri > ci, 1.0, 0.0).astype(BF16)
    run = cnt_ref[0:1, :]
    rank_ref[...] = run + _dot(before, sel)
    run = run + jnp.sum(sel.astype(F32), axis=0, keepdims=True)
    cnt_ref[...] = jnp.broadcast_to(run, cnt_ref.shape)
    tot_ref[...] = jnp.broadcast_to(run, tot_ref.shape)


def route_tokens(h, w_router_pad, tm=512):
    s, d = h.shape
    tm = min(tm, s)
    vm = 2 * (_nbytes((tm, d), BF16) + _nbytes((d, LANES), BF16) + 4 * _nbytes((tm, LANES), F32)) + 4 * _nbytes((tm, tm), F32)
    row = pl.BlockSpec((tm, LANES), lambda i: (i, 0))
    return pl.pallas_call(
        _router_body,
        grid=(s // tm,),
        in_specs=[pl.BlockSpec((tm, d), lambda i: (i, 0)),
                  pl.BlockSpec((d, LANES), lambda i: (0, 0))],
        out_specs=[row, row, row, row, pl.BlockSpec((SUBLANES, LANES), lambda i: (0, 0))],
        out_shape=[jax.ShapeDtypeStruct((s, LANES), F32)] * 4 + [jax.ShapeDtypeStruct((SUBLANES, LANES), F32)],
        scratch_shapes=[pltpu.VMEM((SUBLANES, LANES), F32)],
        compiler_params=_params(("arbitrary",), vm),
        name="moe_router",
    )(h, w_router_pad)


def _plan_body(w_ref, m1_ref, m2_ref, rank_ref, tot_ref, pw_ref, meta_ref, *, tile):
    lane8 = lax.broadcasted_iota(I32, (SUBLANES, LANES), 1)
    n = tot_ref[...]
    padded = jnp.floor((n + (tile - 1.0)) * (1.0 / tile)) * tile
    ends = padded
    d = 1
    while d < N_EXPERTS:
        ends = ends + jnp.where(lane8 >= d, pltpu.roll(ends, d, axis=1), 0.0)
        d *= 2
    start = (ends - padded)[0:1, :]
    posf = start + rank_ref[...]
    m1 = m1_ref[...]
    m2 = m2_ref[...]
    w = w_ref[...]
    lane = lax.broadcasted_iota(I32, w.shape, 1)
    cols = [jnp.sum(m1 * posf, axis=1, keepdims=True), jnp.sum(m2 * posf, axis=1, keepdims=True),
            jnp.sum(m1 * w, axis=1, keepdims=True), jnp.sum(m2 * w, axis=1, keepdims=True)]
    out = jnp.zeros_like(w)
    for j, col in enumerate(cols):
        out = jnp.where(lane == j, col, out)
    pw_ref[...] = out
    tile_start = lane8.astype(F32) * tile
    owner = jnp.zeros((SUBLANES, LANES), F32)
    for e in range(N_EXPERTS - 1):
        owner = owner + jnp.where(tile_start >= ends[:, e:e + 1], 1.0, 0.0)
    n_used = ends[:, N_EXPERTS - 1:N_EXPERTS] * (1.0 / tile)
    row8 = lax.broadcasted_iota(I32, (SUBLANES, LANES), 0)
    meta_ref[...] = jnp.where(row8 == 0, owner, jnp.broadcast_to(n_used, owner.shape))


def dispatch_plan(w, m1, m2, rank, totals, tile):
    s = w.shape[0]
    full = pl.BlockSpec((s, LANES), lambda: (0, 0))
    small = pl.BlockSpec((SUBLANES, LANES), lambda: (0, 0))
    vm = 2 * 5 * _nbytes((s, LANES), F32) + 8 * _nbytes((s, LANES), F32)
    return pl.pallas_call(
        functools.partial(_plan_body, tile=float(tile)),
        in_specs=[full, full, full, full, small],
        out_specs=[full, small],
        out_shape=[jax.ShapeDtypeStruct((s, LANES), F32), jax.ShapeDtypeStruct((SUBLANES, LANES), F32)],
        compiler_params=pltpu.CompilerParams(vmem_limit_bytes=int(min(vm, VMEM_CAP_BYTES))),
        name="moe_plan",
    )(w, m1, m2, rank, totals)


def _row_copy(src_ref, src_row, dst_ref, dst_row, sem):
    return pltpu.make_async_copy(src_ref.at[pl.ds(src_row, 1), :], dst_ref.at[pl.ds(dst_row, 1), :], sem)


def _dispatch_body(p1_ref, p2_ref, x_ref, xg_in_ref, xg_ref, sem, *, tt):
    del xg_in_ref
    base = pl.program_id(0) * tt

    def issue(r, carry):
        _row_copy(x_ref, r, xg_ref, p1_ref[base + r], sem).start()
        _row_copy(x_ref, r, xg_ref, p2_ref[base + r], sem).start()
        return carry

    lax.fori_loop(0, tt, issue, 0)

    def drain(r, carry):
        _row_copy(x_ref, 0, xg_ref, 0, sem).wait()
        _row_copy(x_ref, 0, xg_ref, 0, sem).wait()
        return carry

    lax.fori_loop(0, tt, drain, 0)


def dispatch_rows(x, pos1, pos2, n_rows, tt=256):
    s, d = x.shape
    tt = min(tt, s)
    grid_spec = pltpu.PrefetchScalarGridSpec(
        num_scalar_prefetch=2,
        grid=(s // tt,),
        in_specs=[pl.BlockSpec((tt, d), lambda i, p1, p2: (i, 0)),
                  pl.BlockSpec(memory_space=pl.ANY)],
        out_specs=pl.BlockSpec(memory_space=pl.ANY),
        scratch_shapes=[pltpu.SemaphoreType.DMA(())],
    )
    return pl.pallas_call(
        functools.partial(_dispatch_body, tt=tt),
        grid_spec=grid_spec,
        out_shape=jax.ShapeDtypeStruct((n_rows, d), x.dtype),
        input_output_aliases={3: 0},
        compiler_params=_params(("arbitrary",), 4 * _nbytes((tt, d), F32)),
        name="moe_dispatch",
    )(pos1, pos2, x, jnp.zeros((n_rows, d), x.dtype))


def _combine_body(p1_ref, p2_ref, x_ref, pw_ref, gn_ref, y_ref, o_ref, hn_ref, b1_ref, b2_ref, sem, *, tt):
    base = pl.program_id(0) * tt

    def issue(r, carry):
        _row_copy(y_ref, p1_ref[base + r], b1_ref, r, sem).start()
        _row_copy(y_ref, p2_ref[base + r], b2_ref, r, sem).start()
        return carry

    lax.fori_loop(0, tt, issue, 0)

    def drain(r, carry):
        _row_copy(y_ref, 0, b1_ref, 0, sem).wait()
        _row_copy(y_ref, 0, b2_ref, 0, sem).wait()
        return carry

    lax.fori_loop(0, tt, drain, 0)
    pw = pw_ref[...]
    y = x_ref[...] + pw[:, 2:3] * b1_ref[...] + pw[:, 3:4] * b2_ref[...]
    o_ref[...] = y
    hn_ref[...] = _rms(y, gn_ref[...]).astype(hn_ref.dtype)


def combine_rows(x, pw, y, pos1, pos2, next_gain, hn_dtype, tt=256):
    s, d = x.shape
    tt = min(tt, s)
    grid_spec = pltpu.PrefetchScalarGridSpec(
        num_scalar_prefetch=2,
        grid=(s // tt,),
        in_specs=[pl.BlockSpec((tt, d), lambda i, p1, p2: (i, 0)),
                  pl.BlockSpec((tt, LANES), lambda i, p1, p2: (i, 0)),
                  pl.BlockSpec((1, d), lambda i, p1, p2: (0, 0)),
                  pl.BlockSpec(memory_space=pl.ANY)],
        out_specs=[pl.BlockSpec((tt, d), lambda i, p1, p2: (i, 0)), pl.BlockSpec((tt, d), lambda i, p1, p2: (i, 0))],
        scratch_shapes=[pltpu.VMEM((tt, d), F32), pltpu.VMEM((tt, d), F32), pltpu.SemaphoreType.DMA(())],
    )
    return pl.pallas_call(
        functools.partial(_combine_body, tt=tt),
        grid_spec=grid_spec,
        out_shape=[jax.ShapeDtypeStruct((s, d), F32), jax.ShapeDtypeStruct((s, d), hn_dtype)],
        compiler_params=_params(("arbitrary",), 12 * _nbytes((tt, d), F32)),
        name="moe_combine",
    )(pos1, pos2, x, pw, next_gain.reshape(1, d), y)


def _moe_body(te_ref, nu_ref, xg_ref, gain_ref, wg_ref, wu_ref, wd_ref, o_ref, h_ref, acc_ref):
    i = pl.program_id(0)
    f = pl.program_id(1)

    @pl.when(i < nu_ref[0])
    def _():
        @pl.when(f == 0)
        def _():
            h_ref[...] = _rms(xg_ref[...], gain_ref[...]).astype(h_ref.dtype)
            acc_ref[...] = jnp.zeros_like(acc_ref)

        h = h_ref[...]
        g = _dot(h, wg_ref[0])
        u = _dot(h, wu_ref[0])
        acc_ref[...] += _dot((g * _sigmoid(g) * u).astype(BF16), wd_ref[0])

        @pl.when(f == pl.num_programs(1) - 1)
        def _():
            o_ref[...] = acc_ref[...]

    @pl.when((i >= nu_ref[0]) & (f == 0))
    def _():
        o_ref[...] = jnp.zeros_like(o_ref)


def moe_experts(xg, gain, tile_expert, n_used, w_gate, w_up, w_down, tm, tf=256):
    p, d = xg.shape
    ff = w_gate.shape[2]
    nf = ff // tf
    vm = (2 * (2 * _nbytes((tm, d), F32) + 3 * _nbytes((d, tf), BF16)) + _nbytes((tm, d), BF16)
          + _nbytes((tm, d), F32) + 6 * _nbytes((tm, tf), F32))

    def row(i, f, te, nu):
        return (jnp.minimum(i, nu[0] - 1), 0)

    def fcol(i, f, nu):
        return jnp.where(i < nu[0], f, nf - 1)

    grid_spec = pltpu.PrefetchScalarGridSpec(
        num_scalar_prefetch=2,
        grid=(p // tm, nf),
        in_specs=[pl.BlockSpec((tm, d), row),
                  pl.BlockSpec((1, d), lambda i, f, te, nu: (0, 0)),
                  pl.BlockSpec((1, d, tf), lambda i, f, te, nu: (te[i], 0, fcol(i, f, nu))),
                  pl.BlockSpec((1, d, tf), lambda i, f, te, nu: (te[i], 0, fcol(i, f, nu))),
                  pl.BlockSpec((1, tf, d), lambda i, f, te, nu: (te[i], fcol(i, f, nu), 0))],
        out_specs=pl.BlockSpec((tm, d), lambda i, f, te, nu: (i, 0)),
        scratch_shapes=[pltpu.VMEM((tm, d), BF16), pltpu.VMEM((tm, d), F32)],
    )
    return pl.pallas_call(
        _moe_body,
        grid_spec=grid_spec,
        out_shape=jax.ShapeDtypeStruct((p, d), F32),
        compiler_params=_params(("arbitrary", "arbitrary"), vm),
        name="moe_experts",
    )(tile_expert, n_used, xg, gain.reshape(1, d), w_gate, w_up, w_down)


def _rope_tables(positions, dim, reps):
    inv_freq = ROPE_THETA ** (-jnp.arange(0, dim, 2, dtype=F32) / dim)
    ang = positions.astype(F32)[:, None] * inv_freq
    cos, sin = jnp.cos(ang), jnp.sin(ang)
    return jnp.tile(cos, (1, 2 * reps)), jnp.tile(jnp.concatenate([-sin, sin], axis=1), (1, reps))


def _split_w_in(w):
    d = w.shape[0]
    o_a = 4 * DN_WIDTH
    o_qsa = o_a + 2 * DN_HEADS
    o_kix = o_qsa + 3 * SA_WIDTH + IDX_HEADS * IDX_HEAD_DIM
    o_wix = o_kix + IDX_HEAD_DIM
    o_g = o_wix + IDX_HEADS
    main = jnp.concatenate([w[:, :o_a], w[:, o_g:]], axis=1)
    small = jnp.concatenate([w[:, o_kix:o_wix], w[:, o_a:o_qsa], w[:, o_wix:o_g],
                             jnp.zeros((d, LANES - IDX_HEAD_DIM - 2 * DN_HEADS - IDX_HEADS), w.dtype)], axis=1)
    return main.astype(BF16), w[:, o_qsa:o_kix].astype(BF16), small.astype(BF16)


def _pad_lanes(v, offset):
    return jnp.zeros((1, LANES), F32).at[0, offset:offset + v.shape[0]].set(v.astype(F32))


def _mixer(x, h, cos_sa, sin_sa, cos_ix, sin_ix, ffn_gain, w_in, conv_w, a_log, dt_bias, dn_norm,
           w_dn_out, w_sa_out, w_o):
    s = x.shape[0]
    w_main, w_att, w_small = _split_w_in(w_in)
    proj = matmul(h, w_main, F32, tm=min(1024, s), tn=512, name="in_proj")
    small = matmul(h, w_small, F32, tm=min(1024, s), tn=LANES, name="in_proj_small")

    qd, kd, w, u, att, gl = dn_chunk(proj, small, conv_w.astype(F32), _pad_lanes(a_log, SM_A), _pad_lanes(dt_bias, SM_A))
    o_dn = dn_scan(qd, kd, w, u, att, gl, proj, dn_norm.astype(F32))

    q_sa = proj_rope(h, w_att, BLK_QSA, cos_sa, sin_sa, SA_HEAD_DIM, scale=SA_HEAD_DIM ** -0.5 * LOG2_E)
    k_sa = proj_rope(h, w_att, BLK_KSA, cos_sa, sin_sa, SA_HEAD_DIM)
    v_sa = matmul(h, w_att[:, BLK_VSA * SA_WIDTH:(BLK_VSA + 1) * SA_WIDTH], BF16, tm=min(512, s), tn=SA_WIDTH, name="v_proj")
    q_ix = proj_rope(h, w_att, BLK_QIX, cos_ix, sin_ix, IDX_HEAD_DIM)
    k_lo, k_hi = rope_kix(small, cos_ix, sin_ix)
    t_attn = min(512, s)
    mask_t = index_mask(q_ix, k_lo, k_hi, small, min(TOPK_MAX, s // 4), tq=min(256, s), tk=t_attn, t_attn=t_attn)
    o_sa = masked_attention(q_sa, k_sa, v_sa, mask_t, t=t_attn)

    merged = merge_branches(o_dn, o_sa, w_dn_out.astype(BF16), w_sa_out.astype(BF16), proj)
    return out_proj_norm(merged, w_o.astype(BF16), x, ffn_gain.astype(F32))


def _moe_layer(x, h, norm_gain, w_router, w_gate, w_up, w_down, next_gain, hn_dtype):
    s, d = x.shape
    tile = min(512, s)
    n_tiles = (2 * s) // tile + N_EXPERTS
    wr = jnp.zeros((d, LANES), BF16).at[:, :N_EXPERTS].set(w_router.astype(BF16))
    w, m1, m2, rank, totals = route_tokens(h, wr)
    pw, meta = dispatch_plan(w, m1, m2, rank, totals, tile)
    pos1 = pw[:, 0].astype(I32)
    pos2 = pw[:, 1].astype(I32)
    tile_expert = meta[0, :n_tiles].astype(I32)
    n_used = meta[1, :1].astype(I32)
    xg = dispatch_rows(x, pos1, pos2, n_tiles * tile)
    y = moe_experts(xg, norm_gain.astype(F32), tile_expert, n_used, w_gate.astype(BF16), w_up.astype(BF16),
                    w_down.astype(BF16), tm=tile)
    return combine_rows(x, pw, y, pos1, pos2, next_gain.astype(F32), hn_dtype)


def kernel(x, positions, norm_mix, w_in, conv_w, a_log, dt_bias, dn_norm, w_dn_out, w_sa_out, w_o, norm_ffn, dense_w_gate, dense_w_up, dense_w_down, moe_router, moe_w_gate, moe_w_up, moe_w_down, final_norm):
    b, s, d = x.shape
    depth = norm_mix.shape[0]
    outs = []
    for bi in range(b):
        xb = x[bi]
        pos = positions[bi]
        cos_sa, sin_sa = _rope_tables(pos, SA_HEAD_DIM, 1)
        cos_ix, sin_ix = _rope_tables(pos, IDX_HEAD_DIM, 2)
        h = rmsnorm(xb, norm_mix[0], BF16)
        for layer in range(depth):
            xb, h = _mixer(xb, h, cos_sa, sin_sa, cos_ix, sin_ix, norm_ffn[layer], w_in[layer], conv_w[layer],
                           a_log[layer], dt_bias[layer], dn_norm[layer], w_dn_out[layer], w_sa_out[layer], w_o[layer])
            last = layer == depth - 1
            next_gain = (final_norm if last else norm_mix[layer + 1]).astype(F32)
            hn_dtype = x.dtype if last else BF16
            j = layer // 2
            if layer % 2 == 0:
                xb, h = ffn_dense(h, xb, dense_w_gate[j].astype(BF16), dense_w_up[j].astype(BF16),
                                  dense_w_down[j].astype(BF16), next_gain, hn_dtype)
            else:
                xb, h = _moe_layer(xb, h, norm_ffn[layer], moe_router[j], moe_w_gate[j], moe_w_up[j], moe_w_down[j],
                                   next_gain, hn_dtype)
        outs.append(h)
    return jnp.stack(outs, axis=0)
```

```python
import functools

import jax
import jax.numpy as jnp
from jax import lax
from jax.experimental import pallas as pl
from jax.experimental.pallas import tpu as pltpu

F32 = jnp.float32
BF16 = jnp.bfloat16
I32 = jnp.int32
I16 = jnp.int16

RMS_EPS = 1e-6
L2_EPS = 1e-6
DN_HEADS = 8
DN_HEAD_DIM = 128
DN_WIDTH = DN_HEADS * DN_HEAD_DIM
CONV_WIDTH = 4
DN_CHUNK = 64
SA_HEADS = 8
SA_HEAD_DIM = 128
SA_WIDTH = SA_HEADS * SA_HEAD_DIM
IDX_HEADS = 16
IDX_HEAD_DIM = 64
TOPK_MAX = 256
ROPE_THETA = 10000.0
N_EXPERTS = 8

LANES = 128
SUBLANES = 8
VMEM_CAP_BYTES = 56 * 2**20
NEG_BIG = -1e30
I16_MIN = -2**15
I16_ROWS = 16
LOG2_E = 1.4426950408889634

COL_QKV = 0
COL_Z = 3072
COL_GDN = 4096
COL_GSA = 6144
BLK_QSA, BLK_KSA, BLK_VSA, BLK_QIX = 0, 1, 2, 3
SM_KIX = 0
SM_A = 64
SM_B = 72
SM_WIX = 80


def _params(semantics, vmem_bytes):
    return pltpu.CompilerParams(dimension_semantics=semantics,
                                vmem_limit_bytes=int(min(max(vmem_bytes, 16 * 2**20), VMEM_CAP_BYTES)))


def _hbm(*arrays):
    return tuple(pltpu.with_memory_space_constraint(a, pltpu.HBM) for a in arrays)


def _nbytes(shape, dtype):
    n = 1
    for s in shape:
        n *= s
    return n * jnp.dtype(dtype).itemsize


def _sigmoid(x):
    return 1.0 / (1.0 + jnp.exp(-x))


def _dot(a, b):
    return jnp.dot(a, b, preferred_element_type=F32)


def _dot_nt(a, b):
    return lax.dot_general(a, b, (((1,), (1,)), ((), ())), preferred_element_type=F32)


def _dot_tn(a, b):
    return lax.dot_general(a, b, (((0,), (0,)), ((), ())), preferred_element_type=F32)


def _split_bf16(x):
    hi = x.astype(BF16)
    return hi, (x - hi.astype(F32)).astype(BF16)


def _dot3(a, b):
    return _dot(a[0], b[0]) + _dot(a[1], b[0]) + _dot(a[0], b[1])


def _rms(x, gain):
    ms = jnp.mean(x * x, axis=-1, keepdims=True)
    return x * lax.rsqrt(ms + RMS_EPS) * gain


def _rmsnorm_body(x_ref, g_ref, o_ref):
    o_ref[...] = _rms(x_ref[...], g_ref[...]).astype(o_ref.dtype)


def rmsnorm(x, gain, out_dtype, tm=512):
    m, d = x.shape
    vm = 2 * (_nbytes((tm, d), F32) + _nbytes((tm, d), out_dtype)) + 4 * _nbytes((tm, d), F32)
    return pl.pallas_call(
        _rmsnorm_body,
        grid=(m // tm,),
        in_specs=[pl.BlockSpec((tm, d), lambda i: (i, 0)),
                  pl.BlockSpec((1, d), lambda i: (0, 0))],
        out_specs=pl.BlockSpec((tm, d), lambda i: (i, 0)),
        out_shape=jax.ShapeDtypeStruct((m, d), out_dtype),
        compiler_params=_params(("parallel",), vm),
        name="rmsnorm",
    )(*_hbm(x, gain.reshape(1, d)))


def _mm_body(a_ref, b_ref, o_ref):
    o_ref[...] = _dot(a_ref[...], b_ref[...]).astype(o_ref.dtype)


def matmul(a, b, out_dtype, tm, tn, name):
    m, k = a.shape
    n = b.shape[1]
    vm = 2 * (_nbytes((tm, k), a.dtype) + _nbytes((k, tn), b.dtype) + _nbytes((tm, tn), out_dtype))
    vm += 2 * _nbytes((tm, tn), F32)
    return pl.pallas_call(
        _mm_body,
        grid=(m // tm, n // tn),
        in_specs=[pl.BlockSpec((tm, k), lambda i, j: (i, 0)),
                  pl.BlockSpec((k, tn), lambda i, j: (0, j))],
        out_specs=pl.BlockSpec((tm, tn), lambda i, j: (i, j)),
        out_shape=jax.ShapeDtypeStruct((m, n), out_dtype),
        compiler_params=_params(("parallel", "parallel"), vm),
        name=name,
    )(*_hbm(a, b))


def _dn_chunk_body(xc_ref, xp_ref, sm_ref, cw_ref, alog_ref, dtb_ref,
                   qd_ref, kd_ref, w_ref, u_ref, att_ref, gl_ref, ext_ref):
    c = pl.program_id(0)
    C = DN_CHUNK
    halo = SUBLANES
    ext_ref[0:halo, :] = jnp.where(c > 0, xp_ref[...], 0.0)
    ext_ref[halo:halo + C, :] = xc_ref[...]
    cw = cw_ref[...]
    y = cw[0:1, :] * ext_ref[pl.ds(halo - CONV_WIDTH + 1, C), :]
    for j in range(1, CONV_WIDTH):
        y = y + cw[j:j + 1, :] * ext_ref[pl.ds(halo - CONV_WIDTH + 1 + j, C), :]
    y = y * _sigmoid(y)

    sm = sm_ref[...]
    xa = sm + dtb_ref[...]
    softplus = jnp.maximum(xa, 0.0) + jnp.log1p(jnp.exp(-jnp.abs(xa)))
    g = -jnp.exp(alog_ref[...]) * softplus
    beta = _sigmoid(sm)

    row = lax.broadcasted_iota(I32, (C, LANES), 0)
    gc = g
    d = 1
    while d < C:
        gc = gc + jnp.where(row >= d, pltpu.roll(gc, d, axis=0), 0.0)
        d *= 2
    gct = jnp.concatenate([gc, jnp.zeros_like(gc)], axis=0).T
    ex = jnp.exp(gc)
    gc_last = gc[C - 1:C, :]
    exl = jnp.exp(gc_last - gc)
    gl_ref[...] = jnp.exp(jnp.broadcast_to(gct[SM_A:SM_A + DN_HEADS, C - 1:C], (DN_HEADS, LANES)))

    ri = lax.broadcasted_iota(I32, (C, C), 0)
    ci = lax.broadcasted_iota(I32, (C, C), 1)
    tril = ri >= ci
    strict = ri > ci
    eye = jnp.where(ri == ci, 1.0, 0.0).astype(F32)
    lvl_masks = []
    lb = 0
    while (1 << lb) < C:
        lvl_masks.append(((ri >> (lb + 1)) == (ci >> (lb + 1)))
                         & (((ri >> lb) & 1) == 1) & (((ci >> lb) & 1) == 0))
        lb += 1

    heads = range(DN_HEADS)
    a_mats, kbs, vbs, excols = [], [], [], []
    for h in heads:
        sl = slice(h * DN_HEAD_DIM, (h + 1) * DN_HEAD_DIM)
        qh = y[:, h * DN_HEAD_DIM:(h + 1) * DN_HEAD_DIM]
        kh = y[:, DN_WIDTH + h * DN_HEAD_DIM:DN_WIDTH + (h + 1) * DN_HEAD_DIM]
        vh = y[:, 2 * DN_WIDTH + h * DN_HEAD_DIM:2 * DN_WIDTH + (h + 1) * DN_HEAD_DIM]
        qn = qh * lax.rsqrt(jnp.sum(qh * qh, axis=-1, keepdims=True) + L2_EPS) * (DN_HEAD_DIM ** -0.5)
        kn = kh * lax.rsqrt(jnp.sum(kh * kh, axis=-1, keepdims=True) + L2_EPS)
        bcol = beta[:, SM_B + h:SM_B + h + 1]
        gcol = gc[:, SM_A + h:SM_A + h + 1]
        grow = gct[SM_A + h:SM_A + h + 1, 0:C]
        dec = jnp.exp(jnp.where(tril, gcol - grow, -jnp.inf))
        kb = kn * bcol
        knb = kn.astype(BF16)
        excol = ex[:, SM_A + h:SM_A + h + 1]
        a_mats.append(jnp.where(strict, _dot_nt(kb.astype(BF16), knb) * dec, 0.0))
        kbs.append(kb * excol)
        vbs.append(vh * bcol)
        att = jnp.where(tril, _dot_nt(qn.astype(BF16), knb) * dec, 0.0)
        att_ref[:, h * C:(h + 1) * C] = att.astype(att_ref.dtype)
        qd_ref[:, sl] = (qn * excol).astype(qd_ref.dtype)
        kd_ref[:, sl] = (kn * exl[:, SM_A + h:SM_A + h + 1]).astype(kd_ref.dtype)

    x_inv = [eye - jnp.where(lvl_masks[0], a, 0.0) for a in a_mats]
    for lm in lvl_masks[1:]:
        xs = [_split_bf16(x) for x in x_inv]
        ts = [_dot3(xs[h], _split_bf16(jnp.where(lm, a_mats[h], 0.0))) for h in heads]
        x_inv = [x_inv[h] - _dot3(_split_bf16(ts[h]), xs[h]) for h in heads]
    xs = [_split_bf16(x) for x in x_inv]
    for h in heads:
        sl = slice(h * DN_HEAD_DIM, (h + 1) * DN_HEAD_DIM)
        w_ref[:, sl] = _dot3(xs[h], _split_bf16(kbs[h])).astype(w_ref.dtype)
        u_ref[:, sl] = _dot3(xs[h], _split_bf16(vbs[h]))


def dn_chunk(proj, small, conv_w, alog_pad, dtb_pad):
    s = proj.shape[0]
    C = DN_CHUNK
    n_chunks = s // C
    w3 = 3 * DN_WIDTH
    row_spec = lambda width, dt: pl.BlockSpec((C, width), lambda c: (c, 0))
    vm = 2 * (_nbytes((C, w3), F32) + _nbytes((SUBLANES, w3), F32)) + 8 * _nbytes((C, w3), F32)
    return pl.pallas_call(
        _dn_chunk_body,
        grid=(n_chunks,),
        in_specs=[pl.BlockSpec((C, w3), lambda c: (c, COL_QKV // w3)),
                  pl.BlockSpec((SUBLANES, w3), lambda c: (jnp.maximum(c * (C // SUBLANES) - 1, 0), COL_QKV // w3)),
                  pl.BlockSpec((C, LANES), lambda c: (c, 0)),
                  pl.BlockSpec((CONV_WIDTH, w3), lambda c: (0, 0)),
                  pl.BlockSpec((1, LANES), lambda c: (0, 0)),
                  pl.BlockSpec((1, LANES), lambda c: (0, 0))],
        out_specs=[row_spec(DN_WIDTH, BF16), row_spec(DN_WIDTH, BF16), row_spec(DN_WIDTH, BF16),
                   row_spec(DN_WIDTH, F32), row_spec(DN_HEADS * C, BF16),
                   pl.BlockSpec((DN_HEADS, LANES), lambda c: (c, 0))],
        out_shape=[jax.ShapeDtypeStruct((s, DN_WIDTH), BF16),
                   jax.ShapeDtypeStruct((s, DN_WIDTH), BF16),
                   jax.ShapeDtypeStruct((s, DN_WIDTH), BF16),
                   jax.ShapeDtypeStruct((s, DN_WIDTH), F32),
                   jax.ShapeDtypeStruct((s, DN_HEADS * C), BF16),
                   jax.ShapeDtypeStruct((n_chunks * DN_HEADS, LANES), F32)],
        scratch_shapes=[pltpu.VMEM((SUBLANES + C, w3), F32)],
        compiler_params=_params(("parallel",), vm),
        name="dn_chunk",
    )(*_hbm(proj, proj, small, conv_w, alog_pad, dtb_pad))


def _dn_scan_body(qd_ref, kd_ref, w_ref, u_ref, att_ref, gl_ref, z_ref, nrm_ref, o_ref, st_ref):
    c = pl.program_id(0)
    C = DN_CHUNK

    @pl.when(c == 0)
    def _():
        st_ref[...] = jnp.zeros_like(st_ref)

    heads = range(DN_HEADS)
    sls = [slice(h * DN_HEAD_DIM, (h + 1) * DN_HEAD_DIM) for h in heads]
    states = [st_ref[h] for h in heads]
    sbs = [s.astype(BF16) for s in states]
    w_s = [_dot(w_ref[:, sls[h]], sbs[h]) for h in heads]
    q_s = [_dot(qd_ref[:, sls[h]], sbs[h]) for h in heads]
    vbs = [(u_ref[:, sls[h]] - w_s[h]).astype(BF16) for h in heads]
    outs = [q_s[h] + _dot(att_ref[:, h * C:(h + 1) * C], vbs[h]) for h in heads]
    for h in heads:
        st_ref[h] = states[h] * gl_ref[h:h + 1, :] + _dot_tn(kd_ref[:, sls[h]], vbs[h])
    for h in heads:
        o = outs[h]
        ms = jnp.mean(o * o, axis=-1, keepdims=True)
        z = z_ref[:, sls[h]]
        o_ref[:, sls[h]] = (o * lax.rsqrt(ms + RMS_EPS) * nrm_ref[...] * (z * _sigmoid(z))).astype(o_ref.dtype)


def dn_scan(qd, kd, w, u, att, gl, proj, dn_norm):
    s = qd.shape[0]
    C = DN_CHUNK
    row = lambda width: pl.BlockSpec((C, width), lambda c: (c, 0))
    vm = 2 * 6 * _nbytes((C, DN_WIDTH), F32) + 2 * _nbytes((DN_HEADS, DN_HEAD_DIM, DN_HEAD_DIM), F32)
    return pl.pallas_call(
        _dn_scan_body,
        grid=(s // C,),
        in_specs=[row(DN_WIDTH), row(DN_WIDTH), row(DN_WIDTH), row(DN_WIDTH), row(DN_HEADS * C),
                  pl.BlockSpec((DN_HEADS, LANES), lambda c: (c, 0)),
                  pl.BlockSpec((C, DN_WIDTH), lambda c: (c, COL_Z // DN_WIDTH)),
                  pl.BlockSpec((1, DN_HEAD_DIM), lambda c: (0, 0))],
        out_specs=row(DN_WIDTH),
        out_shape=jax.ShapeDtypeStruct((s, DN_WIDTH), BF16),
        scratch_shapes=[pltpu.VMEM((DN_HEADS, DN_HEAD_DIM, DN_HEAD_DIM), F32)],
        compiler_params=_params(("arbitrary",), vm),
        name="dn_scan",
    )(*_hbm(qd, kd, w, u, att, gl, proj, dn_norm.reshape(1, DN_HEAD_DIM)))


def _rotate_half(x, cs, sn, head_dim, first):
    half = head_dim // 2
    if head_dim == LANES:
        swapped = pltpu.roll(x, half, axis=1)
    else:
        swapped = jnp.where(first, pltpu.roll(x, LANES - half, axis=1), pltpu.roll(x, half, axis=1))
    return x * cs + swapped * sn


def _proj_rope_body(a_ref, b_ref, c_ref, s_ref, o_ref, *, head_dim, scale):
    acc = _dot(a_ref[...], b_ref[...])
    cs = c_ref[...]
    sn = s_ref[...]
    lane = lax.broadcasted_iota(I32, cs.shape, 1)
    first = (lane & (head_dim - 1)) < head_dim // 2
    for j in range(acc.shape[1] // LANES):
        sl = slice(j * LANES, (j + 1) * LANES)
        o_ref[:, sl] = (_rotate_half(acc[:, sl], cs, sn, head_dim, first) * scale).astype(o_ref.dtype)


def proj_rope(h, w_att, col_block, cos_t, sin_t, head_dim, scale=1.0, tm=512):
    s, d = h.shape
    tm = min(tm, s)
    n = SA_WIDTH
    vm = 2 * (_nbytes((tm, d), BF16) + _nbytes((d, n), BF16) + _nbytes((tm, n), BF16)) + 4 * _nbytes((tm, n), F32)
    tab = pl.BlockSpec((tm, LANES), lambda i: (i, 0))
    return pl.pallas_call(
        functools.partial(_proj_rope_body, head_dim=head_dim, scale=scale),
        grid=(s // tm,),
        in_specs=[pl.BlockSpec((tm, d), lambda i: (i, 0)),
                  pl.BlockSpec((d, n), lambda i: (0, col_block)), tab, tab],
        out_specs=pl.BlockSpec((tm, n), lambda i: (i, 0)),
        out_shape=jax.ShapeDtypeStruct((s, n), BF16),
        compiler_params=_params(("parallel",), vm),
        name="proj_rope",
    )(*_hbm(h, w_att, cos_t, sin_t))


def _rope_kix_body(sm_ref, c_ref, s_ref, klo_ref, khi_ref):
    cs = c_ref[...]
    lane = lax.broadcasted_iota(I32, cs.shape, 1)
    first = (lane & (IDX_HEAD_DIM - 1)) < IDX_HEAD_DIM // 2
    k_rot = _rotate_half(sm_ref[...], cs, s_ref[...], IDX_HEAD_DIM, first)
    k_lo = jnp.where(lane < IDX_HEAD_DIM, k_rot, 0.0)
    klo_ref[...] = k_lo.astype(klo_ref.dtype)
    khi_ref[...] = pltpu.roll(k_lo, IDX_HEAD_DIM, axis=1).astype(khi_ref.dtype)


def rope_kix(small, cos_t, sin_t, tm=512):
    s = small.shape[0]
    tm = min(tm, s)
    tab = pl.BlockSpec((tm, LANES), lambda i: (i, 0))
    return pl.pallas_call(
        _rope_kix_body,
        grid=(s // tm,),
        in_specs=[tab, tab, tab],
        out_specs=[tab, tab],
        out_shape=[jax.ShapeDtypeStruct((s, LANES), BF16)] * 2,
        compiler_params=_params(("parallel",), 16 * _nbytes((tm, LANES), F32)),
        name="rope_kix",
    )(*_hbm(small, cos_t, sin_t))


def _index_body(q_ref, klo_ref, khi_ref, sm_ref, mask_ref, keys_ref, hi_ref, lo_ref, *, tq, tk, topk):
    i = pl.program_id(0)
    nkt = keys_ref.shape[0]
    nk = ((i + 1) * tq + tk - 1) // tk
    w_t = (sm_ref[...] * (IDX_HEADS ** -0.5 * IDX_HEAD_DIM ** -0.5)).T
    key_l = lax.broadcasted_iota(I32, (tk, tq), 0)
    qry_g = i * tq + lax.broadcasted_iota(I32, (tk, tq), 1)

    def score_tile(kt, carry):
        off = pl.multiple_of(kt * tk, tk)
        k_lo = klo_ref[pl.ds(off, tk), :]
        k_hi = khi_ref[pl.ds(off, tk), :]
        acc = jnp.zeros((tk, tq), F32)
        for j in range(IDX_HEADS // 2):
            qp = q_ref[:, j * LANES:(j + 1) * LANES]
            acc = acc + w_t[SM_WIX + 2 * j:SM_WIX + 2 * j + 1, :] * jnp.maximum(_dot_nt(k_lo, qp), 0.0)
            acc = acc + w_t[SM_WIX + 2 * j + 1:SM_WIX + 2 * j + 2, :] * jnp.maximum(_dot_nt(k_hi, qp), 0.0)
        sc = jnp.where(kt * tk + key_l <= qry_g, acc, -jnp.inf)
        bits = pltpu.bitcast(sc, I32)
        key = bits ^ ((bits >> 31) & 0x7FFFFFFF)
        keys_ref[kt] = key
        hi_ref[kt] = (key >> 16).astype(I16)
        return carry

    lax.fori_loop(0, nk, score_tile, 0)

    cnt_rows = 2 * I16_ROWS

    def count16(ref, cand, strict):
        def body(kt, cnt):
            t = ref[kt]
            hit = jnp.where((t > cand) if strict else (t >= cand), jnp.int16(1), jnp.int16(0))
            for j in range(tk // cnt_rows):
                cnt = cnt + hit[j * cnt_rows:(j + 1) * cnt_rows]
            return cnt

        cnt = lax.fori_loop(0, nk, body, jnp.zeros((cnt_rows, tq), I16))
        return jnp.sum(cnt.astype(F32), axis=0, keepdims=True)

    def kth_largest16(ref, kth):
        zero = jnp.zeros((1, tq), I32)
        ans = jnp.where(count16(ref, zero.astype(I16), False) >= kth, zero, I16_MIN)

        def bit_body(b, ans):
            cand = ans + lax.shift_left(jnp.int32(1), 14 - b)
            return jnp.where(count16(ref, cand.astype(I16), False) >= kth, cand, ans)

        return lax.fori_loop(0, 15, bit_body, ans)

    kf = jnp.full((1, tq), float(topk), F32)
    t_hi = kth_largest16(hi_ref, kf)
    above = count16(hi_ref, t_hi.astype(I16), True)

    def low_tile(kt, carry):
        key = keys_ref[kt]
        lo = (key & 0xFFFF) + I16_MIN
        lo_ref[kt] = jnp.where((key >> 16) == t_hi, lo, I16_MIN).astype(I16)
        return carry

    lax.fori_loop(0, nk, low_tile, 0)
    t_lo = kth_largest16(lo_ref, kf - above)
    ans = t_hi * 65536 + (t_lo - I16_MIN)

    def put(kt, sel):
        mask_ref[0, pl.ds(pl.multiple_of(kt * tk, tk), tk), :] = jnp.where(sel, 0.0, NEG_BIG).astype(mask_ref.dtype)

    def write(kt, cnt):
        sel = (keys_ref[kt] >= ans) & (kt * tk + key_l <= qry_g)
        put(kt, sel)
        return cnt + jnp.sum(jnp.where(sel, 1.0, 0.0).reshape(tk // cnt_rows, cnt_rows, tq), axis=0)

    kept = jnp.sum(lax.fori_loop(0, nk, write, jnp.zeros((cnt_rows, tq), F32)), axis=0, keepdims=True)

    @pl.when(jnp.max(kept) > float(topk))
    def _():
        def count_gt(kt, cnt):
            gt = (keys_ref[kt] > ans) & (kt * tk + key_l <= qry_g)
            return cnt + jnp.sum(jnp.where(gt, 1.0, 0.0), axis=0, keepdims=True)

        need = kf - lax.fori_loop(0, nk, count_gt, jnp.zeros((1, tq), F32))
        ri = lax.broadcasted_iota(I32, (tk, tk), 0)
        ci = lax.broadcasted_iota(I32, (tk, tk), 1)
        upto = jnp.where(ri >= ci, 1.0, 0.0).astype(BF16)

        def rewrite(kt, seen):
            key = keys_ref[kt]
            causal = kt * tk + key_l <= qry_g
            tie = (key == ans) & causal
            rank = seen + _dot(upto, jnp.where(tie, 1.0, 0.0).astype(BF16))
            put(kt, ((key > ans) & causal) | (tie & (rank <= need)))
            return rank[tk - 1:tk, :]

        lax.fori_loop(0, nk, rewrite, jnp.zeros((1, tq), F32))

    def clear(kt, carry):
        mask_ref[0, pl.ds(pl.multiple_of(kt * tk, tk), tk), :] = jnp.full((tk, tq), NEG_BIG, mask_ref.dtype)
        return carry

    lax.fori_loop(nk, nkt, clear, 0)


def index_mask(q_ix, k_lo, k_hi, small, topk, tq, tk, t_attn):
    s = q_ix.shape[0]
    nkt = s // tk
    per = t_attn // tq
    wq = IDX_HEADS * IDX_HEAD_DIM
    vm = (2 * (_nbytes((tq, wq), BF16) + 2 * _nbytes((s, LANES), BF16) + _nbytes((tq, LANES), F32)
               + _nbytes((s, tq), BF16)) + 2 * _nbytes((s, tq), I32) + 8 * _nbytes((tk, tq), F32))
    return pl.pallas_call(
        functools.partial(_index_body, tq=tq, tk=tk, topk=topk),
        grid=(s // tq,),
        in_specs=[pl.BlockSpec((tq, wq), lambda i: (i, 0)),
                  pl.BlockSpec((s, LANES), lambda i: (0, 0)),
                  pl.BlockSpec((s, LANES), lambda i: (0, 0)),
                  pl.BlockSpec((tq, LANES), lambda i: (i, 0))],
        out_specs=pl.BlockSpec((1, s, tq), lambda i: (i // per, 0, i % per)),
        out_shape=jax.ShapeDtypeStruct((s // t_attn, s, t_attn), BF16),
        scratch_shapes=[pltpu.VMEM((nkt, tk, tq), I32), pltpu.VMEM((nkt, tk, tq), I16), pltpu.VMEM((nkt, tk, tq), I16)],
        compiler_params=_params(("parallel",), vm),
        name="index_mask",
    )(*_hbm(q_ix, k_lo, k_hi, small))


def _attn_body(qi_ref, ki_ref, q_ref, k_ref, v_ref, mk_ref, o_ref, m_ref, l_ref, al_ref, acc_ref, bias_ref, s_ref):
    p = pl.program_id(0)
    qi = qi_ref[p]
    ki = ki_ref[p]

    @pl.when(ki == 0)
    def _():
        m_ref[...] = jnp.full_like(m_ref, NEG_BIG)
        l_ref[...] = jnp.zeros_like(l_ref)
        acc_ref[...] = jnp.zeros_like(acc_ref)

    bias_ref[...] = mk_ref[0].astype(F32)
    tk, tq = bias_ref.shape
    part = 4 * SUBLANES

    def fold(x, op):
        return op(op(x.reshape(tk // part, part, tq), axis=0), axis=0, keepdims=True)

    heads = range(SA_HEADS)
    sls = [slice(h * SA_HEAD_DIM, (h + 1) * SA_HEAD_DIM) for h in heads]
    for h in heads:
        s = _dot_nt(k_ref[:, sls[h]], q_ref[:, sls[h]]) + bias_ref[...]
        s_ref[h] = s
        m_old = m_ref[h:h + 1, :]
        m_new = jnp.maximum(m_old, fold(s, jnp.max))
        al_ref[h:h + 1, :] = jnp.exp2(m_old - m_new)
        m_ref[h:h + 1, :] = m_new
    for h in heads:
        pr = jnp.exp2(s_ref[h] - m_ref[h:h + 1, :])
        alpha = al_ref[h:h + 1, :]
        l_ref[h:h + 1, :] = alpha * l_ref[h:h + 1, :] + fold(pr, jnp.sum)
        acc_ref[h] = alpha * acc_ref[h] + _dot_tn(v_ref[:, sls[h]], pr.astype(BF16))

    @pl.when(ki == qi)
    def _():
        for h in range(SA_HEADS):
            sl = slice(h * SA_HEAD_DIM, (h + 1) * SA_HEAD_DIM)
            o_ref[:, sl] = (acc_ref[h] / l_ref[h:h + 1, :]).T.astype(o_ref.dtype)


def masked_attention(q, k, v, mask_t, t):
    s = q.shape[0]
    nb = s // t
    pairs = [(a, b) for a in range(nb) for b in range(a + 1)]
    qi = jnp.asarray([a for a, _ in pairs], I32)
    ki = jnp.asarray([b for _, b in pairs], I32)
    vm = (2 * (4 * _nbytes((t, SA_WIDTH), BF16) + _nbytes((t, t), BF16)) + _nbytes((t, SA_WIDTH), F32)
          + 2 * _nbytes((SA_HEADS, t), F32) + 8 * _nbytes((t, t), F32))
    grid_spec = pltpu.PrefetchScalarGridSpec(
        num_scalar_prefetch=2,
        grid=(len(pairs),),
        in_specs=[pl.BlockSpec((t, SA_WIDTH), lambda p, qi, ki: (qi[p], 0)),
                  pl.BlockSpec((t, SA_WIDTH), lambda p, qi, ki: (ki[p], 0)),
                  pl.BlockSpec((t, SA_WIDTH), lambda p, qi, ki: (ki[p], 0)),
                  pl.BlockSpec((1, t, t), lambda p, qi, ki: (qi[p], ki[p], 0))],
        out_specs=pl.BlockSpec((t, SA_WIDTH), lambda p, qi, ki: (qi[p], 0)),
        scratch_shapes=[pltpu.VMEM((SA_HEADS, t), F32),
                        pltpu.VMEM((SA_HEADS, t), F32),
                        pltpu.VMEM((SA_HEADS, t), F32),
                        pltpu.VMEM((SA_HEADS, SA_HEAD_DIM, t), F32),
                        pltpu.VMEM((t, t), F32),
                        pltpu.VMEM((SA_HEADS, t, t), F32)],
    )
    return pl.pallas_call(
        _attn_body,
        grid_spec=grid_spec,
        out_shape=jax.ShapeDtypeStruct((s, SA_WIDTH), BF16),
        compiler_params=_params(("arbitrary",), vm),
        name="masked_attention",
    )(qi, ki, *_hbm(q, k, v, mask_t))


def _merge_body(odn_ref, osa_ref, wdn_ref, wsa_ref, gdn_ref, gsa_ref, o_ref):
    y_dn = _dot(odn_ref[...], wdn_ref[...])
    y_sa = _dot(osa_ref[...], wsa_ref[...])
    o_ref[...] = (_sigmoid(gdn_ref[...]) * y_dn + _sigmoid(gsa_ref[...]) * y_sa).astype(o_ref.dtype)


def merge_branches(o_dn, o_sa, w_dn, w_sa, proj, tm=512, tn=512):
    s, kd = o_dn.shape
    d = w_dn.shape[1]
    vm = 2 * (2 * _nbytes((tm, kd), BF16) + 2 * _nbytes((kd, tn), BF16) + 2 * _nbytes((tm, tn), F32)
              + _nbytes((tm, tn), BF16)) + 6 * _nbytes((tm, tn), F32)
    return pl.pallas_call(
        _merge_body,
        grid=(s // tm, d // tn),
        in_specs=[pl.BlockSpec((tm, kd), lambda i, j: (i, 0)),
                  pl.BlockSpec((tm, kd), lambda i, j: (i, 0)),
                  pl.BlockSpec((kd, tn), lambda i, j: (0, j)),
                  pl.BlockSpec((kd, tn), lambda i, j: (0, j)),
                  pl.BlockSpec((tm, tn), lambda i, j: (i, COL_GDN // tn + j)),
                  pl.BlockSpec((tm, tn), lambda i, j: (i, COL_GSA // tn + j))],
        out_specs=pl.BlockSpec((tm, tn), lambda i, j: (i, j)),
        out_shape=jax.ShapeDtypeStruct((s, d), BF16),
        compiler_params=_params(("parallel", "parallel"), vm),
        name="merge_branches",
    )(*_hbm(o_dn, o_sa, w_dn, w_sa, proj, proj))


def _out_proj_body(m_ref, w_ref, x_ref, gn_ref, o_ref, hn_ref):
    y = x_ref[...] + _dot(m_ref[...], w_ref[...])
    o_ref[...] = y
    hn_ref[...] = _rms(y, gn_ref[...]).astype(hn_ref.dtype)


def out_proj_norm(merged, w_o, x, next_gain, tm=512):
    s, k = merged.shape
    d = w_o.shape[1]
    tm = min(tm, s)
    vm = 2 * (_nbytes((tm, k), BF16) + _nbytes((k, d), BF16) + 2 * _nbytes((tm, d), F32) + _nbytes((tm, d), BF16)) + 3 * _nbytes((tm, d), F32)
    row = lambda width: pl.BlockSpec((tm, width), lambda i: (i, 0))
    return pl.pallas_call(
        _out_proj_body,
        grid=(s // tm,),
        in_specs=[row(k), pl.BlockSpec((k, d), lambda i: (0, 0)), row(d), pl.BlockSpec((1, d), lambda i: (0, 0))],
        out_specs=[row(d), row(d)],
        out_shape=[jax.ShapeDtypeStruct((s, d), F32), jax.ShapeDtypeStruct((s, d), BF16)],
        compiler_params=_params(("parallel",), vm),
        name="out_proj",
    )(*_hbm(merged, w_o, x, next_gain.reshape(1, d)))


def _ffn_body(h_ref, x_ref, wg_ref, wu_ref, wd_ref, gn_ref, o_ref, hn_ref):
    f = pl.program_id(1)

    @pl.when(f == 0)
    def _():
        o_ref[...] = jnp.zeros_like(o_ref)

    h = h_ref[...]
    g = _dot(h, wg_ref[...])
    u = _dot(h, wu_ref[...])
    o_ref[...] += _dot((g * _sigmoid(g) * u).astype(BF16), wd_ref[...])

    @pl.when(f == pl.num_programs(1) - 1)
    def _():
        y = x_ref[...] + o_ref[...]
        o_ref[...] = y
        hn_ref[...] = _rms(y, gn_ref[...]).astype(hn_ref.dtype)


def ffn_dense(h, x, w_gate, w_up, w_down, next_gain, hn_dtype, tm=512, tf=512):
    s, d = h.shape
    ff = w_gate.shape[1]
    vm = (2 * (_nbytes((tm, d), BF16) + _nbytes((tm, d), hn_dtype) + 2 * _nbytes((tm, d), F32)
               + 3 * _nbytes((d, tf), BF16)) + 3 * _nbytes((tm, d), F32) + 6 * _nbytes((tm, tf), F32))
    row = pl.BlockSpec((tm, d), lambda i, f: (i, 0))
    return pl.pallas_call(
        _ffn_body,
        grid=(s // tm, ff // tf),
        in_specs=[row, row,
                  pl.BlockSpec((d, tf), lambda i, f: (0, f)),
                  pl.BlockSpec((d, tf), lambda i, f: (0, f)),
                  pl.BlockSpec((tf, d), lambda i, f: (f, 0)),
                  pl.BlockSpec((1, d), lambda i, f: (0, 0))],
        out_specs=[row, row],
        out_shape=[jax.ShapeDtypeStruct((s, d), F32), jax.ShapeDtypeStruct((s, d), hn_dtype)],
        compiler_params=_params(("parallel", "arbitrary"), vm),
        name="ffn_dense",
    )(*_hbm(h, x, w_gate, w_up, w_down, next_gain.reshape(1, d)))


def _router_body(h_ref, wr_ref, w_ref, m1_ref, m2_ref, rank_ref, tot_ref, cnt_ref):
    logits = _dot(h_ref[...], wr_ref[...])
    lane = lax.broadcasted_iota(I32, logits.shape, 1)
    lg = jnp.where(lane < N_EXPERTS, logits, -jnp.inf)
    m1 = jnp.max(lg, axis=1, keepdims=True)
    i1 = jnp.min(jnp.where(lg == m1, lane, LANES), axis=1, keepdims=True)
    lg2 = jnp.where(lane == i1, -jnp.inf, lg)
    m2 = jnp.max(lg2, axis=1, keepdims=True)
    i2 = jnp.min(jnp.where(lg2 == m2, lane, LANES), axis=1, keepdims=True)
    e = jnp.exp(m2 - m1)
    first = lane == i1
    second = lane == i2
    w_ref[...] = jnp.where(first, 1.0 / (1.0 + e), 0.0) + jnp.where(second, e / (1.0 + e), 0.0)
    m1_ref[...] = jnp.where(first, 1.0, 0.0)
    m2_ref[...] = jnp.where(second, 1.0, 0.0)

    @pl.when(pl.program_id(0) == 0)
    def _():
        cnt_ref[...] = jnp.zeros_like(cnt_ref)

    tm = logits.shape[0]
    sel = jnp.where(first | second, 1.0, 0.0).astype(BF16)
    ri = lax.broadcasted_iota(I32, (tm, tm), 0)
    ci = lax.broadcasted_iota(I32, (tm, tm), 1)
    before = jnp.where(ri > ci, 1.0, 0.0).astype(BF16)
    run = cnt_ref[0:1, :]
    rank_ref[...] = run + _dot(before, sel)
    run = run + jnp.sum(sel.astype(F32), axis=0, keepdims=True)
    cnt_ref[...] = jnp.broadcast_to(run, cnt_ref.shape)
    tot_ref[...] = jnp.broadcast_to(run, tot_ref.shape)


def route_tokens(h, w_router_pad, tm=512):
    s, d = h.shape
    tm = min(tm, s)
    vm = 2 * (_nbytes((tm, d), BF16) + _nbytes((d, LANES), BF16) + 4 * _nbytes((tm, LANES), F32)) + 4 * _nbytes((tm, tm), F32)
    row = pl.BlockSpec((tm, LANES), lambda i: (i, 0))
    return pl.pallas_call(
        _router_body,
        grid=(s // tm,),
        in_specs=[pl.BlockSpec((tm, d), lambda i: (i, 0)),
                  pl.BlockSpec((d, LANES), lambda i: (0, 0))],
        out_specs=[row, row, row, row, pl.BlockSpec((SUBLANES, LANES), lambda i: (0, 0))],
        out_shape=[jax.ShapeDtypeStruct((s, LANES), F32)] * 4 + [jax.ShapeDtypeStruct((SUBLANES, LANES), F32)],
        scratch_shapes=[pltpu.VMEM((SUBLANES, LANES), F32)],
        compiler_params=_params(("arbitrary",), vm),
        name="moe_router",
    )(*_hbm(h, w_router_pad))


def _plan_body(w_ref, m1_ref, m2_ref, rank_ref, tot_ref, pw_ref, meta_ref, *, tile):
    lane8 = lax.broadcasted_iota(I32, (SUBLANES, LANES), 1)
    n = tot_ref[...]
    padded = jnp.floor((n + (tile - 1.0)) * (1.0 / tile)) * tile
    ends = padded
    d = 1
    while d < N_EXPERTS:
        ends = ends + jnp.where(lane8 >= d, pltpu.roll(ends, d, axis=1), 0.0)
        d *= 2
    start = (ends - padded)[0:1, :]
    posf = start + rank_ref[...]
    m1 = m1_ref[...]
    m2 = m2_ref[...]
    w = w_ref[...]
    lane = lax.broadcasted_iota(I32, w.shape, 1)
    cols = [jnp.sum(m1 * posf, axis=1, keepdims=True), jnp.sum(m2 * posf, axis=1, keepdims=True),
            jnp.sum(m1 * w, axis=1, keepdims=True), jnp.sum(m2 * w, axis=1, keepdims=True)]
    out = jnp.zeros_like(w)
    for j, col in enumerate(cols):
        out = jnp.where(lane == j, col, out)
    pw_ref[...] = out
    tile_start = lane8.astype(F32) * tile
    owner = jnp.zeros((SUBLANES, LANES), F32)
    for e in range(N_EXPERTS - 1):
        owner = owner + jnp.where(tile_start >= ends[:, e:e + 1], 1.0, 0.0)
    n_used = ends[:, N_EXPERTS - 1:N_EXPERTS] * (1.0 / tile)
    row8 = lax.broadcasted_iota(I32, (SUBLANES, LANES), 0)
    meta_ref[...] = jnp.where(row8 == 0, owner, jnp.broadcast_to(n_used, owner.shape))


def dispatch_plan(w, m1, m2, rank, totals, tile):
    s = w.shape[0]
    full = pl.BlockSpec((s, LANES), lambda: (0, 0))
    small = pl.BlockSpec((SUBLANES, LANES), lambda: (0, 0))
    vm = 2 * 5 * _nbytes((s, LANES), F32) + 8 * _nbytes((s, LANES), F32)
    return pl.pallas_call(
        functools.partial(_plan_body, tile=float(tile)),
        in_specs=[full, full, full, full, small],
        out_specs=[full, small],
        out_shape=[jax.ShapeDtypeStruct((s, LANES), F32), jax.ShapeDtypeStruct((SUBLANES, LANES), F32)],
        compiler_params=pltpu.CompilerParams(vmem_limit_bytes=int(min(vm, VMEM_CAP_BYTES))),
        name="moe_plan",
    )(*_hbm(w, m1, m2, rank, totals))


def _row_copy(src_ref, src_row, dst_ref, dst_row, sem):
    return pltpu.make_async_copy(src_ref.at[pl.ds(src_row, 1), :], dst_ref.at[pl.ds(dst_row, 1), :], sem)


def _dispatch_body(p1_ref, p2_ref, x_ref, xg_in_ref, xg_ref, sem, *, tt):
    del xg_in_ref
    base = pl.program_id(0) * tt

    def issue(r, carry):
        _row_copy(x_ref, r, xg_ref, p1_ref[base + r], sem).start()
        _row_copy(x_ref, r, xg_ref, p2_ref[base + r], sem).start()
        return carry

    lax.fori_loop(0, tt, issue, 0)

    def drain(r, carry):
        _row_copy(x_ref, 0, xg_ref, 0, sem).wait()
        _row_copy(x_ref, 0, xg_ref, 0, sem).wait()
        return carry

    lax.fori_loop(0, tt, drain, 0)


def dispatch_rows(x, pos1, pos2, n_rows, tt=256):
    s, d = x.shape
    tt = min(tt, s)
    grid_spec = pltpu.PrefetchScalarGridSpec(
        num_scalar_prefetch=2,
        grid=(s // tt,),
        in_specs=[pl.BlockSpec((tt, d), lambda i, p1, p2: (i, 0)),
                  pl.BlockSpec(memory_space=pl.ANY)],
        out_specs=pl.BlockSpec(memory_space=pl.ANY),
        scratch_shapes=[pltpu.SemaphoreType.DMA(())],
    )
    return pl.pallas_call(
        functools.partial(_dispatch_body, tt=tt),
        grid_spec=grid_spec,
        out_shape=jax.ShapeDtypeStruct((n_rows, d), x.dtype),
        input_output_aliases={3: 0},
        compiler_params=_params(("arbitrary",), 4 * _nbytes((tt, d), F32)),
        name="moe_dispatch",
    )(pos1, pos2, *_hbm(x, jnp.zeros((n_rows, d), x.dtype)))


def _combine_body(p1_ref, p2_ref, x_ref, pw_ref, gn_ref, y_ref, o_ref, hn_ref, b1_ref, b2_ref, sem, *, tt):
    base = pl.program_id(0) * tt

    def issue(r, carry):
        _row_copy(y_ref, p1_ref[base + r], b1_ref, r, sem).start()
        _row_copy(y_ref, p2_ref[base + r], b2_ref, r, sem).start()
        return carry

    lax.fori_loop(0, tt, issue, 0)

    def drain(r, carry):
        _row_copy(y_ref, 0, b1_ref, 0, sem).wait()
        _row_copy(y_ref, 0, b2_ref, 0, sem).wait()
        return carry

    lax.fori_loop(0, tt, drain, 0)
    pw = pw_ref[...]
    y = x_ref[...] + pw[:, 2:3] * b1_ref[...] + pw[:, 3:4] * b2_ref[...]
    o_ref[...] = y
    hn_ref[...] = _rms(y, gn_ref[...]).astype(hn_ref.dtype)


def combine_rows(x, pw, y, pos1, pos2, next_gain, hn_dtype, tt=256):
    s, d = x.shape
    tt = min(tt, s)
    grid_spec = pltpu.PrefetchScalarGridSpec(
        num_scalar_prefetch=2,
        grid=(s // tt,),
        in_specs=[pl.BlockSpec((tt, d), lambda i, p1, p2: (i, 0)),
                  pl.BlockSpec((tt, LANES), lambda i, p1, p2: (i, 0)),
                  pl.BlockSpec((1, d), lambda i, p1, p2: (0, 0)),
                  pl.BlockSpec(memory_space=pl.ANY)],
        out_specs=[pl.BlockSpec((tt, d), lambda i, p1, p2: (i, 0)), pl.BlockSpec((tt, d), lambda i, p1, p2: (i, 0))],
        scratch_shapes=[pltpu.VMEM((tt, d), F32), pltpu.VMEM((tt, d), F32), pltpu.SemaphoreType.DMA(())],
    )
    return pl.pallas_call(
        functools.partial(_combine_body, tt=tt),
        grid_spec=grid_spec,
        out_shape=[jax.ShapeDtypeStruct((s, d), F32), jax.ShapeDtypeStruct((s, d), hn_dtype)],
        compiler_params=_params(("arbitrary",), 12 * _nbytes((tt, d), F32)),
        name="moe_combine",
    )(pos1, pos2, *_hbm(x, pw, next_gain.reshape(1, d), y))


def _moe_body(te_ref, nu_ref, xg_ref, gain_ref, wg_ref, wu_ref, wd_ref, o_ref, h_ref, acc_ref):
    i = pl.program_id(0)
    f = pl.program_id(1)

    @pl.when(i < nu_ref[0])
    def _():
        @pl.when(f == 0)
        def _():
            h_ref[...] = _rms(xg_ref[...], gain_ref[...]).astype(h_ref.dtype)
            acc_ref[...] = jnp.zeros_like(acc_ref)

        h = h_ref[...]
        g = _dot(h, wg_ref[0])
        u = _dot(h, wu_ref[0])
        acc_ref[...] += _dot((g * _sigmoid(g) * u).astype(BF16), wd_ref[0])

        @pl.when(f == pl.num_programs(1) - 1)
        def _():
            o_ref[...] = acc_ref[...]

    @pl.when((i >= nu_ref[0]) & (f == 0))
    def _():
        o_ref[...] = jnp.zeros_like(o_ref)


def moe_experts(xg, gain, tile_expert, n_used, w_gate, w_up, w_down, tm, tf=256):
    p, d = xg.shape
    ff = w_gate.shape[2]
    nf = ff // tf
    vm = (2 * (2 * _nbytes((tm, d), F32) + 3 * _nbytes((d, tf), BF16)) + _nbytes((tm, d), BF16)
          + _nbytes((tm, d), F32) + 6 * _nbytes((tm, tf), F32))

    def row(i, f, te, nu):
        return (jnp.minimum(i, nu[0] - 1), 0)

    def fcol(i, f, nu):
        return jnp.where(i < nu[0], f, nf - 1)

    grid_spec = pltpu.PrefetchScalarGridSpec(
        num_scalar_prefetch=2,
        grid=(p // tm, nf),
        in_specs=[pl.BlockSpec((tm, d), row),
                  pl.BlockSpec((1, d), lambda i, f, te, nu: (0, 0)),
                  pl.BlockSpec((1, d, tf), lambda i, f, te, nu: (te[i], 0, fcol(i, f, nu))),
                  pl.BlockSpec((1, d, tf), lambda i, f, te, nu: (te[i], 0, fcol(i, f, nu))),
                  pl.BlockSpec((1, tf, d), lambda i, f, te, nu: (te[i], fcol(i, f, nu), 0))],
        out_specs=pl.BlockSpec((tm, d), lambda i, f, te, nu: (i, 0)),
        scratch_shapes=[pltpu.VMEM((tm, d), BF16), pltpu.VMEM((tm, d), F32)],
    )
    return pl.pallas_call(
        _moe_body,
        grid_spec=grid_spec,
        out_shape=jax.ShapeDtypeStruct((p, d), F32),
        compiler_params=_params(("arbitrary", "arbitrary"), vm),
        name="moe_experts",
    )(tile_expert, n_used, *_hbm(xg, gain.reshape(1, d), w_gate, w_up, w_down))


def _rope_tables(positions, dim, reps):
    inv_freq = ROPE_THETA ** (-jnp.arange(0, dim, 2, dtype=F32) / dim)
    ang = positions.astype(F32)[:, None] * inv_freq
    cos, sin = jnp.cos(ang), jnp.sin(ang)
    return jnp.tile(cos, (1, 2 * reps)), jnp.tile(jnp.concatenate([-sin, sin], axis=1), (1, reps))


def _split_w_in(w):
    d = w.shape[0]
    o_a = 4 * DN_WIDTH
    o_qsa = o_a + 2 * DN_HEADS
    o_kix = o_qsa + 3 * SA_WIDTH + IDX_HEADS * IDX_HEAD_DIM
    o_wix = o_kix + IDX_HEAD_DIM
    o_g = o_wix + IDX_HEADS
    main = jnp.concatenate([w[:, :o_a], w[:, o_g:]], axis=1)
    small = jnp.concatenate([w[:, o_kix:o_wix], w[:, o_a:o_qsa], w[:, o_wix:o_g],
                             jnp.zeros((d, LANES - IDX_HEAD_DIM - 2 * DN_HEADS - IDX_HEADS), w.dtype)], axis=1)
    return main.astype(BF16), w[:, o_qsa:o_kix].astype(BF16), small.astype(BF16)


def _pad_lanes(v, offset):
    return jnp.zeros((1, LANES), F32).at[0, offset:offset + v.shape[0]].set(v.astype(F32))


def _mixer(x, h, cos_sa, sin_sa, cos_ix, sin_ix, ffn_gain, w_in, conv_w, a_log, dt_bias, dn_norm,
           w_dn_out, w_sa_out, w_o):
    s = x.shape[0]
    w_main, w_att, w_small = _split_w_in(w_in)
    proj = matmul(h, w_main, F32, tm=min(1024, s), tn=512, name="in_proj")
    small = matmul(h, w_small, F32, tm=min(1024, s), tn=LANES, name="in_proj_small")

    qd, kd, w, u, att, gl = dn_chunk(proj, small, conv_w.astype(F32), _pad_lanes(a_log, SM_A), _pad_lanes(dt_bias, SM_A))
    o_dn = dn_scan(qd, kd, w, u, att, gl, proj, dn_norm.astype(F32))

    q_sa = proj_rope(h, w_att, BLK_QSA, cos_sa, sin_sa, SA_HEAD_DIM, scale=SA_HEAD_DIM ** -0.5 * LOG2_E)
    k_sa = proj_rope(h, w_att, BLK_KSA, cos_sa, sin_sa, SA_HEAD_DIM)
    v_sa = matmul(h, w_att[:, BLK_VSA * SA_WIDTH:(BLK_VSA + 1) * SA_WIDTH], BF16, tm=min(512, s), tn=SA_WIDTH, name="v_proj")
    q_ix = proj_rope(h, w_att, BLK_QIX, cos_ix, sin_ix, IDX_HEAD_DIM)
    k_lo, k_hi = rope_kix(small, cos_ix, sin_ix)
    t_attn = min(512, s)
    mask_t = index_mask(q_ix, k_lo, k_hi, small, min(TOPK_MAX, s // 4), tq=min(256, s), tk=t_attn, t_attn=t_attn)
    o_sa = masked_attention(q_sa, k_sa, v_sa, mask_t, t=t_attn)

    merged = merge_branches(o_dn, o_sa, w_dn_out.astype(BF16), w_sa_out.astype(BF16), proj)
    return out_proj_norm(merged, w_o.astype(BF16), x, ffn_gain.astype(F32))


def _moe_layer(x, h, norm_gain, w_router, w_gate, w_up, w_down, next_gain, hn_dtype):
    s, d = x.shape
    tile = min(512, s)
    n_tiles = (2 * s) // tile + N_EXPERTS
    wr = jnp.zeros((d, LANES), BF16).at[:, :N_EXPERTS].set(w_router.astype(BF16))
    w, m1, m2, rank, totals = route_tokens(h, wr)
    pw, meta = dispatch_plan(w, m1, m2, rank, totals, tile)
    pos1 = pw[:, 0].astype(I32)
    pos2 = pw[:, 1].astype(I32)
    tile_expert = meta[0, :n_tiles].astype(I32)
    n_used = meta[1, :1].astype(I32)
    xg = dispatch_rows(x, pos1, pos2, n_tiles * tile)
    y = moe_experts(xg, norm_gain.astype(F32), tile_expert, n_used, w_gate.astype(BF16), w_up.astype(BF16),
                    w_down.astype(BF16), tm=tile)
    return combine_rows(x, pw, y, pos1, pos2, next_gain.astype(F32), hn_dtype)


def kernel(x, positions, norm_mix, w_in, conv_w, a_log, dt_bias, dn_norm, w_dn_out, w_sa_out, w_o, norm_ffn, dense_w_gate, dense_w_up, dense_w_down, moe_router, moe_w_gate, moe_w_up, moe_w_down, final_norm):
    b, s, d = x.shape
    depth = norm_mix.shape[0]
    outs = []
    for bi in range(b):
        xb = x[bi]
        pos = positions[bi]
        cos_sa, sin_sa = _rope_tables(pos, SA_HEAD_DIM, 1)
        cos_ix, sin_ix = _rope_tables(pos, IDX_HEAD_DIM, 2)
        h = rmsnorm(xb, norm_mix[0], BF16)
        for layer in range(depth):
            xb, h = _mixer(xb, h, cos_sa, sin_sa, cos_ix, sin_ix, norm_ffn[layer], w_in[layer], conv_w[layer],
                           a_log[layer], dt_bias[layer], dn_norm[layer], w_dn_out[layer], w_sa_out[layer], w_o[layer])
            last = layer == depth - 1
            next_gain = (final_norm if last else norm_mix[layer + 1]).astype(F32)
            hn_dtype = x.dtype if last else BF16
            j = layer // 2
            if layer % 2 == 0:
                xb, h = ffn_dense(h, xb, dense_w_gate[j].astype(BF16), dense_w_up[j].astype(BF16),
                                  dense_w_down[j].astype(BF16), next_gain, hn_dtype)
            else:
                xb, h = _moe_layer(xb, h, norm_ffn[layer], moe_router[j], moe_w_gate[j], moe_w_up[j], moe_w_down[j],
                                   next_gain, hn_dtype)
        outs.append(h)
    return jnp.stack(outs, axis=0)
```

```python
import functools

import jax
import jax.numpy as jnp
from jax import lax
from jax.experimental import pallas as pl
from jax.experimental.pallas import tpu as pltpu

F32 = jnp.float32
BF16 = jnp.bfloat16
I32 = jnp.int32
I16 = jnp.int16

RMS_EPS = 1e-6
L2_EPS = 1e-6
DN_HEADS = 8
DN_HEAD_DIM = 128
DN_WIDTH = DN_HEADS * DN_HEAD_DIM
CONV_WIDTH = 4
DN_CHUNK = 64
SA_HEADS = 8
SA_HEAD_DIM = 128
SA_WIDTH = SA_HEADS * SA_HEAD_DIM
IDX_HEADS = 16
IDX_HEAD_DIM = 64
TOPK_MAX = 256
ROPE_THETA = 10000.0
N_EXPERTS = 8

LANES = 128
SUBLANES = 8
VMEM_CAP_BYTES = 56 * 2**20
NEG_BIG = -1e30
I16_MIN = -2**15
I16_ROWS = 16
LOG2_E = 1.4426950408889634

COL_QKV = 0
COL_Z = 3072
COL_GDN = 4096
COL_GSA = 6144
BLK_QSA, BLK_KSA, BLK_VSA, BLK_QIX = 0, 1, 2, 3
SM_A = 64
SM_B = 72
SM_WIX = 80


def _params(semantics, vmem_bytes):
    return pltpu.CompilerParams(dimension_semantics=semantics,
                                vmem_limit_bytes=int(min(max(vmem_bytes, 16 * 2**20), VMEM_CAP_BYTES)))


def _nbytes(shape, dtype):
    n = 1
    for s in shape:
        n *= s
    return n * jnp.dtype(dtype).itemsize


def _sigmoid(x):
    return 1.0 / (1.0 + jnp.exp(-x))


def _dot(a, b):
    return jnp.dot(a, b, preferred_element_type=F32)


def _dot_nt(a, b):
    return lax.dot_general(a, b, (((1,), (1,)), ((), ())), preferred_element_type=F32)


def _dot_tn(a, b):
    return lax.dot_general(a, b, (((0,), (0,)), ((), ())), preferred_element_type=F32)


def _split_bf16(x):
    hi = x.astype(BF16)
    return hi, (x - hi.astype(F32)).astype(BF16)


def _dot3(a, b):
    return _dot(a[0], b[0]) + _dot(a[1], b[0]) + _dot(a[0], b[1])


def _rms(x, gain):
    ms = jnp.mean(x * x, axis=-1, keepdims=True)
    return x * lax.rsqrt(ms + RMS_EPS) * gain


def _rmsnorm_body(x_ref, g_ref, o_ref):
    o_ref[...] = _rms(x_ref[...], g_ref[...]).astype(o_ref.dtype)


def rmsnorm(x, gain, out_dtype, tm=512):
    m, d = x.shape
    vm = 2 * (_nbytes((tm, d), F32) + _nbytes((tm, d), out_dtype)) + 4 * _nbytes((tm, d), F32)
    return pl.pallas_call(
        _rmsnorm_body,
        grid=(m // tm,),
        in_specs=[pl.BlockSpec((tm, d), lambda i: (i, 0)),
                  pl.BlockSpec((1, d), lambda i: (0, 0))],
        out_specs=pl.BlockSpec((tm, d), lambda i: (i, 0)),
        out_shape=jax.ShapeDtypeStruct((m, d), out_dtype),
        compiler_params=_params(("parallel",), vm),
        name="rmsnorm",
    )(x, gain.reshape(1, d))


def _mm_body(a_ref, b_ref, o_ref):
    o_ref[...] = _dot(a_ref[...], b_ref[...]).astype(o_ref.dtype)


def matmul(a, b, out_dtype, tm, tn, name):
    m, k = a.shape
    n = b.shape[1]
    vm = 2 * (_nbytes((tm, k), a.dtype) + _nbytes((k, tn), b.dtype) + _nbytes((tm, tn), out_dtype))
    vm += 2 * _nbytes((tm, tn), F32)
    return pl.pallas_call(
        _mm_body,
        grid=(m // tm, n // tn),
        in_specs=[pl.BlockSpec((tm, k), lambda i, j: (i, 0)),
                  pl.BlockSpec((k, tn), lambda i, j: (0, j))],
        out_specs=pl.BlockSpec((tm, tn), lambda i, j: (i, j)),
        out_shape=jax.ShapeDtypeStruct((m, n), out_dtype),
        compiler_params=_params(("parallel", "parallel"), vm),
        name=name,
    )(a, b)


def _dn_chunk_body(xc_ref, xp_ref, sm_ref, cw_ref, alog_ref, dtb_ref,
                   qd_ref, kd_ref, w_ref, u_ref, att_ref, gl_ref, ext_ref):
    c = pl.program_id(0)
    C = DN_CHUNK
    halo = SUBLANES
    ext_ref[0:halo, :] = jnp.where(c > 0, xp_ref[...], 0.0)
    ext_ref[halo:halo + C, :] = xc_ref[...]
    cw = cw_ref[...]
    y = cw[0:1, :] * ext_ref[pl.ds(halo - CONV_WIDTH + 1, C), :]
    for j in range(1, CONV_WIDTH):
        y = y + cw[j:j + 1, :] * ext_ref[pl.ds(halo - CONV_WIDTH + 1 + j, C), :]
    y = y * _sigmoid(y)

    sm = sm_ref[...]
    xa = sm + dtb_ref[...]
    softplus = jnp.maximum(xa, 0.0) + jnp.log1p(jnp.exp(-jnp.abs(xa)))
    g = -jnp.exp(alog_ref[...]) * softplus
    beta = _sigmoid(sm)

    row = lax.broadcasted_iota(I32, (C, LANES), 0)
    gc = g
    d = 1
    while d < C:
        gc = gc + jnp.where(row >= d, pltpu.roll(gc, d, axis=0), 0.0)
        d *= 2
    gct = jnp.concatenate([gc, jnp.zeros_like(gc)], axis=0).T
    ex = jnp.exp(gc)
    gc_last = gc[C - 1:C, :]
    exl = jnp.exp(gc_last - gc)
    gl_ref[...] = jnp.exp(jnp.broadcast_to(gct[SM_A:SM_A + DN_HEADS, C - 1:C], (DN_HEADS, LANES)))

    ri = lax.broadcasted_iota(I32, (C, C), 0)
    ci = lax.broadcasted_iota(I32, (C, C), 1)
    tril = ri >= ci
    strict = ri > ci
    eye = jnp.where(ri == ci, 1.0, 0.0).astype(F32)
    lvl_masks = []
    lb = 0
    while (1 << lb) < C:
        lvl_masks.append(((ri >> (lb + 1)) == (ci >> (lb + 1)))
                         & (((ri >> lb) & 1) == 1) & (((ci >> lb) & 1) == 0))
        lb += 1

    heads = range(DN_HEADS)
    a_mats, kbs, vbs = [], [], []
    for h in heads:
        sl = slice(h * DN_HEAD_DIM, (h + 1) * DN_HEAD_DIM)
        qh = y[:, h * DN_HEAD_DIM:(h + 1) * DN_HEAD_DIM]
        kh = y[:, DN_WIDTH + h * DN_HEAD_DIM:DN_WIDTH + (h + 1) * DN_HEAD_DIM]
        vh = y[:, 2 * DN_WIDTH + h * DN_HEAD_DIM:2 * DN_WIDTH + (h + 1) * DN_HEAD_DIM]
        qn = qh * lax.rsqrt(jnp.sum(qh * qh, axis=-1, keepdims=True) + L2_EPS) * (DN_HEAD_DIM ** -0.5)
        kn = kh * lax.rsqrt(jnp.sum(kh * kh, axis=-1, keepdims=True) + L2_EPS)
        bcol = beta[:, SM_B + h:SM_B + h + 1]
        gcol = gc[:, SM_A + h:SM_A + h + 1]
        grow = gct[SM_A + h:SM_A + h + 1, 0:C]
        dec = jnp.exp(jnp.where(tril, gcol - grow, -jnp.inf))
        kb = kn * bcol
        knb = kn.astype(BF16)
        excol = ex[:, SM_A + h:SM_A + h + 1]
        a_mats.append(jnp.where(strict, _dot_nt(kb.astype(BF16), knb) * dec, 0.0))
        kbs.append(kb * excol)
        vbs.append(vh * bcol)
        att = jnp.where(tril, _dot_nt(qn.astype(BF16), knb) * dec, 0.0)
        att_ref[:, h * C:(h + 1) * C] = att.astype(att_ref.dtype)
        qd_ref[:, sl] = (qn * excol).astype(qd_ref.dtype)
        kd_ref[:, sl] = (kn * exl[:, SM_A + h:SM_A + h + 1]).astype(kd_ref.dtype)

    x_inv = [eye - jnp.where(lvl_masks[0], a, 0.0) for a in a_mats]
    for lm in lvl_masks[1:]:
        xs = [_split_bf16(x) for x in x_inv]
        ts = [_dot3(xs[h], _split_bf16(jnp.where(lm, a_mats[h], 0.0))) for h in heads]
        x_inv = [x_inv[h] - _dot3(_split_bf16(ts[h]), xs[h]) for h in heads]
    xs = [_split_bf16(x) for x in x_inv]
    for h in heads:
        sl = slice(h * DN_HEAD_DIM, (h + 1) * DN_HEAD_DIM)
        w_ref[:, sl] = _dot3(xs[h], _split_bf16(kbs[h])).astype(w_ref.dtype)
        u_ref[:, sl] = _dot3(xs[h], _split_bf16(vbs[h]))


def dn_chunk(proj, small, conv_w, alog_pad, dtb_pad):
    s = proj.shape[0]
    C = DN_CHUNK
    n_chunks = s // C
    w3 = 3 * DN_WIDTH
    row_spec = lambda width: pl.BlockSpec((C, width), lambda c: (c, 0))
    vm = 2 * (_nbytes((C, w3), F32) + _nbytes((SUBLANES, w3), F32)) + 8 * _nbytes((C, w3), F32)
    return pl.pallas_call(
        _dn_chunk_body,
        grid=(n_chunks,),
        in_specs=[pl.BlockSpec((C, w3), lambda c: (c, COL_QKV // w3)),
                  pl.BlockSpec((SUBLANES, w3), lambda c: (jnp.maximum(c * (C // SUBLANES) - 1, 0), COL_QKV // w3)),
                  pl.BlockSpec((C, LANES), lambda c: (c, 0)),
                  pl.BlockSpec((CONV_WIDTH, w3), lambda c: (0, 0)),
                  pl.BlockSpec((1, LANES), lambda c: (0, 0)),
                  pl.BlockSpec((1, LANES), lambda c: (0, 0))],
        out_specs=[row_spec(DN_WIDTH), row_spec(DN_WIDTH), row_spec(DN_WIDTH),
                   row_spec(DN_WIDTH), row_spec(DN_HEADS * C),
                   pl.BlockSpec((DN_HEADS, LANES), lambda c: (c, 0))],
        out_shape=[jax.ShapeDtypeStruct((s, DN_WIDTH), BF16),
                   jax.ShapeDtypeStruct((s, DN_WIDTH), BF16),
                   jax.ShapeDtypeStruct((s, DN_WIDTH), BF16),
                   jax.ShapeDtypeStruct((s, DN_WIDTH), F32),
                   jax.ShapeDtypeStruct((s, DN_HEADS * C), BF16),
                   jax.ShapeDtypeStruct((n_chunks * DN_HEADS, LANES), F32)],
        scratch_shapes=[pltpu.VMEM((SUBLANES + C, w3), F32)],
        compiler_params=_params(("parallel",), vm),
        name="dn_chunk",
    )(proj, proj, small, conv_w, alog_pad, dtb_pad)


def _dn_scan_body(qd_ref, kd_ref, w_ref, u_ref, att_ref, gl_ref, z_ref, nrm_ref, o_ref, st_ref):
    c = pl.program_id(0)
    C = DN_CHUNK

    @pl.when(c == 0)
    def _():
        st_ref[...] = jnp.zeros_like(st_ref)

    heads = range(DN_HEADS)
    sls = [slice(h * DN_HEAD_DIM, (h + 1) * DN_HEAD_DIM) for h in heads]
    states = [st_ref[h] for h in heads]
    sbs = [s.astype(BF16) for s in states]
    w_s = [_dot(w_ref[:, sls[h]], sbs[h]) for h in heads]
    q_s = [_dot(qd_ref[:, sls[h]], sbs[h]) for h in heads]
    vbs = [(u_ref[:, sls[h]] - w_s[h]).astype(BF16) for h in heads]
    outs = [q_s[h] + _dot(att_ref[:, h * C:(h + 1) * C], vbs[h]) for h in heads]
    for h in heads:
        st_ref[h] = states[h] * gl_ref[h:h + 1, :] + _dot_tn(kd_ref[:, sls[h]], vbs[h])
    for h in heads:
        o = outs[h]
        ms = jnp.mean(o * o, axis=-1, keepdims=True)
        z = z_ref[:, sls[h]]
        o_ref[:, sls[h]] = (o * lax.rsqrt(ms + RMS_EPS) * nrm_ref[...] * (z * _sigmoid(z))).astype(o_ref.dtype)


def dn_scan(qd, kd, w, u, att, gl, proj, dn_norm):
    s = qd.shape[0]
    C = DN_CHUNK
    row = lambda width: pl.BlockSpec((C, width), lambda c: (c, 0))
    vm = 2 * 6 * _nbytes((C, DN_WIDTH), F32) + 2 * _nbytes((DN_HEADS, DN_HEAD_DIM, DN_HEAD_DIM), F32)
    return pl.pallas_call(
        _dn_scan_body,
        grid=(s // C,),
        in_specs=[row(DN_WIDTH), row(DN_WIDTH), row(DN_WIDTH), row(DN_WIDTH), row(DN_HEADS * C),
                  pl.BlockSpec((DN_HEADS, LANES), lambda c: (c, 0)),
                  pl.BlockSpec((C, DN_WIDTH), lambda c: (c, COL_Z // DN_WIDTH)),
                  pl.BlockSpec((1, DN_HEAD_DIM), lambda c: (0, 0))],
        out_specs=row(DN_WIDTH),
        out_shape=jax.ShapeDtypeStruct((s, DN_WIDTH), BF16),
        scratch_shapes=[pltpu.VMEM((DN_HEADS, DN_HEAD_DIM, DN_HEAD_DIM), F32)],
        compiler_params=_params(("arbitrary",), vm),
        name="dn_scan",
    )(qd, kd, w, u, att, gl, proj, dn_norm.reshape(1, DN_HEAD_DIM))


def _rotate_half(x, cs, sn, head_dim, first):
    half = head_dim // 2
    if head_dim == LANES:
        swapped = pltpu.roll(x, half, axis=1)
    else:
        swapped = jnp.where(first, pltpu.roll(x, LANES - half, axis=1), pltpu.roll(x, half, axis=1))
    return x * cs + swapped * sn


def _proj_rope_body(a_ref, b_ref, c_ref, s_ref, o_ref, *, head_dim, scale):
    acc = _dot(a_ref[...], b_ref[...])
    cs = c_ref[...]
    sn = s_ref[...]
    lane = lax.broadcasted_iota(I32, cs.shape, 1)
    first = (lane & (head_dim - 1)) < head_dim // 2
    for j in range(acc.shape[1] // LANES):
        sl = slice(j * LANES, (j + 1) * LANES)
        o_ref[:, sl] = (_rotate_half(acc[:, sl], cs, sn, head_dim, first) * scale).astype(o_ref.dtype)


def proj_rope(h, w_att, col_block, cos_t, sin_t, head_dim, scale=1.0, tm=512):
    s, d = h.shape
    tm = min(tm, s)
    n = SA_WIDTH
    vm = 2 * (_nbytes((tm, d), BF16) + _nbytes((d, n), BF16) + _nbytes((tm, n), BF16)) + 4 * _nbytes((tm, n), F32)
    tab = pl.BlockSpec((tm, LANES), lambda i: (i, 0))
    return pl.pallas_call(
        functools.partial(_proj_rope_body, head_dim=head_dim, scale=scale),
        grid=(s // tm,),
        in_specs=[pl.BlockSpec((tm, d), lambda i: (i, 0)),
                  pl.BlockSpec((d, n), lambda i: (0, col_block)), tab, tab],
        out_specs=pl.BlockSpec((tm, n), lambda i: (i, 0)),
        out_shape=jax.ShapeDtypeStruct((s, n), BF16),
        compiler_params=_params(("parallel",), vm),
        name="proj_rope",
    )(h, w_att, cos_t, sin_t)


def _rope_kix_body(sm_ref, c_ref, s_ref, klo_ref, khi_ref):
    cs = c_ref[...]
    lane = lax.broadcasted_iota(I32, cs.shape, 1)
    first = (lane & (IDX_HEAD_DIM - 1)) < IDX_HEAD_DIM // 2
    k_rot = _rotate_half(sm_ref[...], cs, s_ref[...], IDX_HEAD_DIM, first)
    k_lo = jnp.where(lane < IDX_HEAD_DIM, k_rot, 0.0)
    klo_ref[...] = k_lo.astype(klo_ref.dtype)
    khi_ref[...] = pltpu.roll(k_lo, IDX_HEAD_DIM, axis=1).astype(khi_ref.dtype)


def rope_kix(small, cos_t, sin_t, tm=512):
    s = small.shape[0]
    tm = min(tm, s)
    tab = pl.BlockSpec((tm, LANES), lambda i: (i, 0))
    return pl.pallas_call(
        _rope_kix_body,
        grid=(s // tm,),
        in_specs=[tab, tab, tab],
        out_specs=[tab, tab],
        out_shape=[jax.ShapeDtypeStruct((s, LANES), BF16)] * 2,
        compiler_params=_params(("parallel",), 16 * _nbytes((tm, LANES), F32)),
        name="rope_kix",
    )(small, cos_t, sin_t)


def _index_body(q_ref, klo_ref, khi_ref, sm_ref, mask_ref, keys_ref, hi_ref, lo_ref, *, tq, tk, topk):
    i = pl.program_id(0)
    nkt = keys_ref.shape[0]
    nk = ((i + 1) * tq + tk - 1) // tk
    w_t = (sm_ref[...] * (IDX_HEADS ** -0.5 * IDX_HEAD_DIM ** -0.5)).T
    key_l = lax.broadcasted_iota(I32, (tk, tq), 0)
    qry_g = i * tq + lax.broadcasted_iota(I32, (tk, tq), 1)

    def score_tile(kt, carry):
        off = pl.multiple_of(kt * tk, tk)
        k_lo = klo_ref[pl.ds(off, tk), :]
        k_hi = khi_ref[pl.ds(off, tk), :]
        acc = jnp.zeros((tk, tq), F32)
        for j in range(IDX_HEADS // 2):
            qp = q_ref[:, j * LANES:(j + 1) * LANES]
            acc = acc + w_t[SM_WIX + 2 * j:SM_WIX + 2 * j + 1, :] * jnp.maximum(_dot_nt(k_lo, qp), 0.0)
            acc = acc + w_t[SM_WIX + 2 * j + 1:SM_WIX + 2 * j + 2, :] * jnp.maximum(_dot_nt(k_hi, qp), 0.0)
        sc = jnp.where(kt * tk + key_l <= qry_g, acc, -jnp.inf)
        bits = pltpu.bitcast(sc, I32)
        key = bits ^ ((bits >> 31) & 0x7FFFFFFF)
        keys_ref[kt] = key
        hi_ref[kt] = (key >> 16).astype(I16)
        return carry

    lax.fori_loop(0, nk, score_tile, 0)

    cnt_rows = 2 * I16_ROWS

    def count16(ref, cand, strict):
        def body(kt, cnt):
            t = ref[kt]
            hit = jnp.where((t > cand) if strict else (t >= cand), jnp.int16(1), jnp.int16(0))
            for j in range(tk // cnt_rows):
                cnt = cnt + hit[j * cnt_rows:(j + 1) * cnt_rows]
            return cnt

        cnt = lax.fori_loop(0, nk, body, jnp.zeros((cnt_rows, tq), I16))
        return jnp.sum(cnt.astype(F32), axis=0, keepdims=True)

    def kth_largest16(ref, kth):
        zero = jnp.zeros((1, tq), I32)
        ans = jnp.where(count16(ref, zero.astype(I16), False) >= kth, zero, I16_MIN)

        def bit_body(b, ans):
            cand = ans + lax.shift_left(jnp.int32(1), 14 - b)
            return jnp.where(count16(ref, cand.astype(I16), False) >= kth, cand, ans)

        return lax.fori_loop(0, 15, bit_body, ans)

    kf = jnp.full((1, tq), float(topk), F32)
    t_hi = kth_largest16(hi_ref, kf)
    above = count16(hi_ref, t_hi.astype(I16), True)

    def low_tile(kt, carry):
        key = keys_ref[kt]
        lo = (key & 0xFFFF) + I16_MIN
        lo_ref[kt] = jnp.where((key >> 16) == t_hi, lo, I16_MIN).astype(I16)
        return carry

    lax.fori_loop(0, nk, low_tile, 0)
    t_lo = kth_largest16(lo_ref, kf - above)
    ans = t_hi * 65536 + (t_lo - I16_MIN)

    def put(kt, sel):
        mask_ref[0, pl.ds(pl.multiple_of(kt * tk, tk), tk), :] = jnp.where(sel, 0.0, NEG_BIG).astype(mask_ref.dtype)

    def write(kt, cnt):
        sel = (keys_ref[kt] >= ans) & (kt * tk + key_l <= qry_g)
        put(kt, sel)
        return cnt + jnp.sum(jnp.where(sel, 1.0, 0.0).reshape(tk // cnt_rows, cnt_rows, tq), axis=0)

    kept = jnp.sum(lax.fori_loop(0, nk, write, jnp.zeros((cnt_rows, tq), F32)), axis=0, keepdims=True)

    @pl.when(jnp.max(kept) > float(topk))
    def _():
        def count_gt(kt, cnt):
            gt = (keys_ref[kt] > ans) & (kt * tk + key_l <= qry_g)
            return cnt + jnp.sum(jnp.where(gt, 1.0, 0.0), axis=0, keepdims=True)

        need = kf - lax.fori_loop(0, nk, count_gt, jnp.zeros((1, tq), F32))
        ri = lax.broadcasted_iota(I32, (tk, tk), 0)
        ci = lax.broadcasted_iota(I32, (tk, tk), 1)
        upto = jnp.where(ri >= ci, 1.0, 0.0).astype(BF16)

        def rewrite(kt, seen):
            key = keys_ref[kt]
            causal = kt * tk + key_l <= qry_g
            tie = (key == ans) & causal
            rank = seen + _dot(upto, jnp.where(tie, 1.0, 0.0).astype(BF16))
            put(kt, ((key > ans) & causal) | (tie & (rank <= need)))
            return rank[tk - 1:tk, :]

        lax.fori_loop(0, nk, rewrite, jnp.zeros((1, tq), F32))

    def clear(kt, carry):
        mask_ref[0, pl.ds(pl.multiple_of(kt * tk, tk), tk), :] = jnp.full((tk, tq), NEG_BIG, mask_ref.dtype)
        return carry

    lax.fori_loop(nk, nkt, clear, 0)


def index_mask(q_ix, k_lo, k_hi, small, topk, tq, tk, t_attn):
    s = q_ix.shape[0]
    nkt = s // tk
    per = t_attn // tq
    wq = IDX_HEADS * IDX_HEAD_DIM
    vm = (2 * (_nbytes((tq, wq), BF16) + 2 * _nbytes((s, LANES), BF16) + _nbytes((tq, LANES), F32)
               + _nbytes((s, tq), BF16)) + 2 * _nbytes((s, tq), I32) + 8 * _nbytes((tk, tq), F32))
    return pl.pallas_call(
        functools.partial(_index_body, tq=tq, tk=tk, topk=topk),
        grid=(s // tq,),
        in_specs=[pl.BlockSpec((tq, wq), lambda i: (i, 0)),
                  pl.BlockSpec((s, LANES), lambda i: (0, 0)),
                  pl.BlockSpec((s, LANES), lambda i: (0, 0)),
                  pl.BlockSpec((tq, LANES), lambda i: (i, 0))],
        out_specs=pl.BlockSpec((1, s, tq), lambda i: (i // per, 0, i % per)),
        out_shape=jax.ShapeDtypeStruct((s // t_attn, s, t_attn), BF16),
        scratch_shapes=[pltpu.VMEM((nkt, tk, tq), I32), pltpu.VMEM((nkt, tk, tq), I16), pltpu.VMEM((nkt, tk, tq), I16)],
        compiler_params=_params(("parallel",), vm),
        name="index_mask",
    )(q_ix, k_lo, k_hi, small)


def _attn_body(qi_ref, ki_ref, q_ref, k_ref, v_ref, mk_ref, o_ref, m_ref, l_ref, al_ref, acc_ref, bias_ref, s_ref):
    p = pl.program_id(0)
    qi = qi_ref[p]
    ki = ki_ref[p]

    @pl.when(ki == 0)
    def _():
        m_ref[...] = jnp.full_like(m_ref, NEG_BIG)
        l_ref[...] = jnp.zeros_like(l_ref)
        acc_ref[...] = jnp.zeros_like(acc_ref)

    bias_ref[...] = mk_ref[0].astype(F32)
    tk, tq = bias_ref.shape
    part = 4 * SUBLANES

    def fold(x, op):
        return op(op(x.reshape(tk // part, part, tq), axis=0), axis=0, keepdims=True)

    heads = range(SA_HEADS)
    sls = [slice(h * SA_HEAD_DIM, (h + 1) * SA_HEAD_DIM) for h in heads]
    for h in heads:
        s = _dot_nt(k_ref[:, sls[h]], q_ref[:, sls[h]]) + bias_ref[...]
        s_ref[h] = s
        m_old = m_ref[h:h + 1, :]
        m_new = jnp.maximum(m_old, fold(s, jnp.max))
        al_ref[h:h + 1, :] = jnp.exp2(m_old - m_new)
        m_ref[h:h + 1, :] = m_new
    for h in heads:
        pr = jnp.exp2(s_ref[h] - m_ref[h:h + 1, :])
        alpha = al_ref[h:h + 1, :]
        l_ref[h:h + 1, :] = alpha * l_ref[h:h + 1, :] + fold(pr, jnp.sum)
        acc_ref[h] = alpha * acc_ref[h] + _dot_tn(v_ref[:, sls[h]], pr.astype(BF16))

    @pl.when(ki == qi)
    def _():
        for h in range(SA_HEADS):
            sl = slice(h * SA_HEAD_DIM, (h + 1) * SA_HEAD_DIM)
            o_ref[:, sl] = (acc_ref[h] / l_ref[h:h + 1, :]).T.astype(o_ref.dtype)


def masked_attention(q, k, v, mask_t, t):
    s = q.shape[0]
    nb = s // t
    pairs = [(a, b) for a in range(nb) for b in range(a + 1)]
    qi = jnp.asarray([a for a, _ in pairs], I32)
    ki = jnp.asarray([b for _, b in pairs], I32)
    vm = (2 * (4 * _nbytes((t, SA_WIDTH), BF16) + _nbytes((t, t), BF16)) + _nbytes((t, SA_WIDTH), F32)
          + 2 * _nbytes((SA_HEADS, t), F32) + 8 * _nbytes((t, t), F32))
    grid_spec = pltpu.PrefetchScalarGridSpec(
        num_scalar_prefetch=2,
        grid=(len(pairs),),
        in_specs=[pl.BlockSpec((t, SA_WIDTH), lambda p, qi, ki: (qi[p], 0)),
                  pl.BlockSpec((t, SA_WIDTH), lambda p, qi, ki: (ki[p], 0)),
                  pl.BlockSpec((t, SA_WIDTH), lambda p, qi, ki: (ki[p], 0)),
                  pl.BlockSpec((1, t, t), lambda p, qi, ki: (qi[p], ki[p], 0))],
        out_specs=pl.BlockSpec((t, SA_WIDTH), lambda p, qi, ki: (qi[p], 0)),
        scratch_shapes=[pltpu.VMEM((SA_HEADS, t), F32),
                        pltpu.VMEM((SA_HEADS, t), F32),
                        pltpu.VMEM((SA_HEADS, t), F32),
                        pltpu.VMEM((SA_HEADS, SA_HEAD_DIM, t), F32),
                        pltpu.VMEM((t, t), F32),
                        pltpu.VMEM((SA_HEADS, t, t), F32)],
    )
    return pl.pallas_call(
        _attn_body,
        grid_spec=grid_spec,
        out_shape=jax.ShapeDtypeStruct((s, SA_WIDTH), BF16),
        compiler_params=_params(("arbitrary",), vm),
        name="masked_attention",
    )(qi, ki, q, k, v, mask_t)


def _merge_body(odn_ref, osa_ref, wdn_ref, wsa_ref, gdn_ref, gsa_ref, o_ref):
    y_dn = _dot(odn_ref[...], wdn_ref[...])
    y_sa = _dot(osa_ref[...], wsa_ref[...])
    o_ref[...] = (_sigmoid(gdn_ref[...]) * y_dn + _sigmoid(gsa_ref[...]) * y_sa).astype(o_ref.dtype)


def merge_branches(o_dn, o_sa, w_dn, w_sa, proj, tm=512, tn=512):
    s, kd = o_dn.shape
    d = w_dn.shape[1]
    vm = 2 * (2 * _nbytes((tm, kd), BF16) + 2 * _nbytes((kd, tn), BF16) + 2 * _nbytes((tm, tn), F32)
              + _nbytes((tm, tn), BF16)) + 6 * _nbytes((tm, tn), F32)
    return pl.pallas_call(
        _merge_body,
        grid=(s // tm, d // tn),
        in_specs=[pl.BlockSpec((tm, kd), lambda i, j: (i, 0)),
                  pl.BlockSpec((tm, kd), lambda i, j: (i, 0)),
                  pl.BlockSpec((kd, tn), lambda i, j: (0, j)),
                  pl.BlockSpec((kd, tn), lambda i, j: (0, j)),
                  pl.BlockSpec((tm, tn), lambda i, j: (i, COL_GDN // tn + j)),
                  pl.BlockSpec((tm, tn), lambda i, j: (i, COL_GSA // tn + j))],
        out_specs=pl.BlockSpec((tm, tn), lambda i, j: (i, j)),
        out_shape=jax.ShapeDtypeStruct((s, d), BF16),
        compiler_params=_params(("parallel", "parallel"), vm),
        name="merge_branches",
    )(o_dn, o_sa, w_dn, w_sa, proj, proj)


def _out_proj_body(m_ref, w_ref, x_ref, gn_ref, o_ref, hn_ref):
    y = x_ref[...] + _dot(m_ref[...], w_ref[...])
    o_ref[...] = y
    hn_ref[...] = _rms(y, gn_ref[...]).astype(hn_ref.dtype)


def out_proj_norm(merged, w_o, x, next_gain, tm=512):
    s, k = merged.shape
    d = w_o.shape[1]
    tm = min(tm, s)
    vm = 2 * (_nbytes((tm, k), BF16) + _nbytes((k, d), BF16) + 2 * _nbytes((tm, d), F32) + _nbytes((tm, d), BF16)) + 3 * _nbytes((tm, d), F32)
    row = lambda width: pl.BlockSpec((tm, width), lambda i: (i, 0))
    return pl.pallas_call(
        _out_proj_body,
        grid=(s // tm,),
        in_specs=[row(k), pl.BlockSpec((k, d), lambda i: (0, 0)), row(d), pl.BlockSpec((1, d), lambda i: (0, 0))],
        out_specs=[row(d), row(d)],
        out_shape=[jax.ShapeDtypeStruct((s, d), F32), jax.ShapeDtypeStruct((s, d), BF16)],
        compiler_params=_params(("parallel",), vm),
        name="out_proj",
    )(merged, w_o, x, next_gain.reshape(1, d))


def _ffn_body(h_ref, x_ref, wg_ref, wu_ref, wd_ref, gn_ref, o_ref, hn_ref):
    f = pl.program_id(1)

    @pl.when(f == 0)
    def _():
        o_ref[...] = jnp.zeros_like(o_ref)

    h = h_ref[...]
    g = _dot(h, wg_ref[...])
    u = _dot(h, wu_ref[...])
    o_ref[...] += _dot((g * _sigmoid(g) * u).astype(BF16), wd_ref[...])

    @pl.when(f == pl.num_programs(1) - 1)
    def _():
        y = x_ref[...] + o_ref[...]
        o_ref[...] = y
        hn_ref[...] = _rms(y, gn_ref[...]).astype(hn_ref.dtype)


def ffn_dense(h, x, w_gate, w_up, w_down, next_gain, hn_dtype, tm=512, tf=512):
    s, d = h.shape
    ff = w_gate.shape[1]
    vm = (2 * (_nbytes((tm, d), BF16) + _nbytes((tm, d), hn_dtype) + 2 * _nbytes((tm, d), F32)
               + 3 * _nbytes((d, tf), BF16)) + 3 * _nbytes((tm, d), F32) + 6 * _nbytes((tm, tf), F32))
    row = pl.BlockSpec((tm, d), lambda i, f: (i, 0))
    return pl.pallas_call(
        _ffn_body,
        grid=(s // tm, ff // tf),
        in_specs=[row, row,
                  pl.BlockSpec((d, tf), lambda i, f: (0, f)),
                  pl.BlockSpec((d, tf), lambda i, f: (0, f)),
                  pl.BlockSpec((tf, d), lambda i, f: (f, 0)),
                  pl.BlockSpec((1, d), lambda i, f: (0, 0))],
        out_specs=[row, row],
        out_shape=[jax.ShapeDtypeStruct((s, d), F32), jax.ShapeDtypeStruct((s, d), hn_dtype)],
        compiler_params=_params(("parallel", "arbitrary"), vm),
        name="ffn_dense",
    )(h, x, w_gate, w_up, w_down, next_gain.reshape(1, d))


def _router_body(h_ref, wr_ref, w_ref, m1_ref, m2_ref, rank_ref, tot_ref, cnt_ref):
    logits = _dot(h_ref[...], wr_ref[...])
    lane = lax.broadcasted_iota(I32, logits.shape, 1)
    lg = jnp.where(lane < N_EXPERTS, logits, -jnp.inf)
    m1 = jnp.max(lg, axis=1, keepdims=True)
    i1 = jnp.min(jnp.where(lg == m1, lane, LANES), axis=1, keepdims=True)
    lg2 = jnp.where(lane == i1, -jnp.inf, lg)
    m2 = jnp.max(lg2, axis=1, keepdims=True)
    i2 = jnp.min(jnp.where(lg2 == m2, lane, LANES), axis=1, keepdims=True)
    e = jnp.exp(m2 - m1)
    first = lane == i1
    second = lane == i2
    w_ref[...] = jnp.where(first, 1.0 / (1.0 + e), 0.0) + jnp.where(second, e / (1.0 + e), 0.0)
    m1_ref[...] = jnp.where(first, 1.0, 0.0)
    m2_ref[...] = jnp.where(second, 1.0, 0.0)

    @pl.when(pl.program_id(0) == 0)
    def _():
        cnt_ref[...] = jnp.zeros_like(cnt_ref)

    tm = logits.shape[0]
    sel = jnp.where(first | second, 1.0, 0.0).astype(BF16)
    ri = lax.broadcasted_iota(I32, (tm, tm), 0)
    ci = lax.broadcasted_iota(I32, (tm, tm), 1)
    before = jnp.where(ri > ci, 1.0, 0.0).astype(BF16)
    run = cnt_ref[0:1, :]
    rank_ref[...] = run + _dot(before, sel)
    run = run + jnp.sum(sel.astype(F32), axis=0, keepdims=True)
    cnt_ref[...] = jnp.broadcast_to(run, cnt_ref.shape)
    tot_ref[...] = jnp.broadcast_to(run, tot_ref.shape)


def route_tokens(h, w_router_pad, tm=512):
    s, d = h.shape
    tm = min(tm, s)
    vm = 2 * (_nbytes((tm, d), BF16) + _nbytes((d, LANES), BF16) + 4 * _nbytes((tm, LANES), F32)) + 4 * _nbytes((tm, tm), F32)
    row = pl.BlockSpec((tm, LANES), lambda i: (i, 0))
    return pl.pallas_call(
        _router_body,
        grid=(s // tm,),
        in_specs=[pl.BlockSpec((tm, d), lambda i: (i, 0)),
                  pl.BlockSpec((d, LANES), lambda i: (0, 0))],
        out_specs=[row, row, row, row, pl.BlockSpec((SUBLANES, LANES), lambda i: (0, 0))],
        out_shape=[jax.ShapeDtypeStruct((s, LANES), F32)] * 4 + [jax.ShapeDtypeStruct((SUBLANES, LANES), F32)],
        scratch_shapes=[pltpu.VMEM((SUBLANES, LANES), F32)],
        compiler_params=_params(("arbitrary",), vm),
        name="moe_router",
    )(h, w_router_pad)


def _plan_body(w_ref, m1_ref, m2_ref, rank_ref, tot_ref, pw_ref, meta_ref, *, tile):
    lane8 = lax.broadcasted_iota(I32, (SUBLANES, LANES), 1)
    n = tot_ref[...]
    padded = jnp.floor((n + (tile - 1.0)) * (1.0 / tile)) * tile
    ends = padded
    d = 1
    while d < N_EXPERTS:
        ends = ends + jnp.where(lane8 >= d, pltpu.roll(ends, d, axis=1), 0.0)
        d *= 2
    start = (ends - padded)[0:1, :]
    posf = start + rank_ref[...]
    m1 = m1_ref[...]
    m2 = m2_ref[...]
    w = w_ref[...]
    lane = lax.broadcasted_iota(I32, w.shape, 1)
    cols = [jnp.sum(m1 * posf, axis=1, keepdims=True), jnp.sum(m2 * posf, axis=1, keepdims=True),
            jnp.sum(m1 * w, axis=1, keepdims=True), jnp.sum(m2 * w, axis=1, keepdims=True)]
    out = jnp.zeros_like(w)
    for j, col in enumerate(cols):
        out = jnp.where(lane == j, col, out)
    pw_ref[...] = out
    tile_start = lane8.astype(F32) * tile
    owner = jnp.zeros((SUBLANES, LANES), F32)
    for e in range(N_EXPERTS - 1):
        owner = owner + jnp.where(tile_start >= ends[:, e:e + 1], 1.0, 0.0)
    n_used = ends[:, N_EXPERTS - 1:N_EXPERTS] * (1.0 / tile)
    row8 = lax.broadcasted_iota(I32, (SUBLANES, LANES), 0)
    meta_ref[...] = jnp.where(row8 == 0, owner, jnp.broadcast_to(n_used, owner.shape))


def dispatch_plan(w, m1, m2, rank, totals, tile):
    s = w.shape[0]
    full = pl.BlockSpec((s, LANES), lambda: (0, 0))
    small = pl.BlockSpec((SUBLANES, LANES), lambda: (0, 0))
    vm = 2 * 5 * _nbytes((s, LANES), F32) + 8 * _nbytes((s, LANES), F32)
    return pl.pallas_call(
        functools.partial(_plan_body, tile=float(tile)),
        in_specs=[full, full, full, full, small],
        out_specs=[full, small],
        out_shape=[jax.ShapeDtypeStruct((s, LANES), F32), jax.ShapeDtypeStruct((SUBLANES, LANES), F32)],
        compiler_params=pltpu.CompilerParams(vmem_limit_bytes=int(min(vm, VMEM_CAP_BYTES))),
        name="moe_plan",
    )(w, m1, m2, rank, totals)


def _row_copy(src_ref, src_row, dst_ref, dst_row, sem):
    return pltpu.make_async_copy(src_ref.at[pl.ds(src_row, 1), :], dst_ref.at[pl.ds(dst_row, 1), :], sem)


def _dispatch_body(p1_ref, p2_ref, x_ref, xg_in_ref, xg_ref, sem, *, tt):
    del xg_in_ref
    base = pl.program_id(0) * tt

    def issue(r, carry):
        _row_copy(x_ref, r, xg_ref, p1_ref[base + r], sem).start()
        _row_copy(x_ref, r, xg_ref, p2_ref[base + r], sem).start()
        return carry

    lax.fori_loop(0, tt, issue, 0)

    def drain(r, carry):
        _row_copy(x_ref, 0, xg_ref, 0, sem).wait()
        _row_copy(x_ref, 0, xg_ref, 0, sem).wait()
        return carry

    lax.fori_loop(0, tt, drain, 0)


def dispatch_rows(x, pos1, pos2, n_rows, tt=256):
    s, d = x.shape
    tt = min(tt, s)
    grid_spec = pltpu.PrefetchScalarGridSpec(
        num_scalar_prefetch=2,
        grid=(s // tt,),
        in_specs=[pl.BlockSpec((tt, d), lambda i, p1, p2: (i, 0)),
                  pl.BlockSpec(memory_space=pl.ANY)],
        out_specs=pl.BlockSpec(memory_space=pl.ANY),
        scratch_shapes=[pltpu.SemaphoreType.DMA(())],
    )
    return pl.pallas_call(
        functools.partial(_dispatch_body, tt=tt),
        grid_spec=grid_spec,
        out_shape=jax.ShapeDtypeStruct((n_rows, d), x.dtype),
        input_output_aliases={3: 0},
        compiler_params=_params(("arbitrary",), 4 * _nbytes((tt, d), F32)),
        name="moe_dispatch",
    )(pos1, pos2, x, jnp.zeros((n_rows, d), x.dtype))


def _combine_body(p1_ref, p2_ref, x_ref, pw_ref, gn_ref, y_ref, o_ref, hn_ref, b1_ref, b2_ref, sem, *, tt):
    base = pl.program_id(0) * tt

    def issue(r, carry):
        _row_copy(y_ref, p1_ref[base + r], b1_ref, r, sem).start()
        _row_copy(y_ref, p2_ref[base + r], b2_ref, r, sem).start()
        return carry

    lax.fori_loop(0, tt, issue, 0)

    def drain(r, carry):
        _row_copy(y_ref, 0, b1_ref, 0, sem).wait()
        _row_copy(y_ref, 0, b2_ref, 0, sem).wait()
        return carry

    lax.fori_loop(0, tt, drain, 0)
    pw = pw_ref[...]
    y = x_ref[...] + pw[:, 2:3] * b1_ref[...] + pw[:, 3:4] * b2_ref[...]
    o_ref[...] = y
    hn_ref[...] = _rms(y, gn_ref[...]).astype(hn_ref.dtype)


def combine_rows(x, pw, y, pos1, pos2, next_gain, hn_dtype, tt=256):
    s, d = x.shape
    tt = min(tt, s)
    grid_spec = pltpu.PrefetchScalarGridSpec(
        num_scalar_prefetch=2,
        grid=(s // tt,),
        in_specs=[pl.BlockSpec((tt, d), lambda i, p1, p2: (i, 0)),
                  pl.BlockSpec((tt, LANES), lambda i, p1, p2: (i, 0)),
                  pl.BlockSpec((1, d), lambda i, p1, p2: (0, 0)),
                  pl.BlockSpec(memory_space=pl.ANY)],
        out_specs=[pl.BlockSpec((tt, d), lambda i, p1, p2: (i, 0)), pl.BlockSpec((tt, d), lambda i, p1, p2: (i, 0))],
        scratch_shapes=[pltpu.VMEM((tt, d), F32), pltpu.VMEM((tt, d), F32), pltpu.SemaphoreType.DMA(())],
    )
    return pl.pallas_call(
        functools.partial(_combine_body, tt=tt),
        grid_spec=grid_spec,
        out_shape=[jax.ShapeDtypeStruct((s, d), F32), jax.ShapeDtypeStruct((s, d), hn_dtype)],
        compiler_params=_params(("arbitrary",), 12 * _nbytes((tt, d), F32)),
        name="moe_combine",
    )(pos1, pos2, x, pw, next_gain.reshape(1, d), y)


def _moe_body(te_ref, nu_ref, xg_ref, gain_ref, wg_ref, wu_ref, wd_ref, o_ref, h_ref, acc_ref):
    i = pl.program_id(0)
    f = pl.program_id(1)

    @pl.when(i < nu_ref[0])
    def _():
        @pl.when(f == 0)
        def _():
            h_ref[...] = _rms(xg_ref[...], gain_ref[...]).astype(h_ref.dtype)
            acc_ref[...] = jnp.zeros_like(acc_ref)

        h = h_ref[...]
        g = _dot(h, wg_ref[0])
        u = _dot(h, wu_ref[0])
        acc_ref[...] += _dot((g * _sigmoid(g) * u).astype(BF16), wd_ref[0])

        @pl.when(f == pl.num_programs(1) - 1)
        def _():
            o_ref[...] = acc_ref[...]

    @pl.when((i >= nu_ref[0]) & (f == 0))
    def _():
        o_ref[...] = jnp.zeros_like(o_ref)


def moe_experts(xg, gain, tile_expert, n_used, w_gate, w_up, w_down, tm, tf=256):
    p, d = xg.shape
    ff = w_gate.shape[2]
    nf = ff // tf
    vm = (2 * (2 * _nbytes((tm, d), F32) + 3 * _nbytes((d, tf), BF16)) + _nbytes((tm, d), BF16)
          + _nbytes((tm, d), F32) + 6 * _nbytes((tm, tf), F32))

    def row(i, f, te, nu):
        return (jnp.minimum(i, nu[0] - 1), 0)

    def fcol(i, f, nu):
        return jnp.where(i < nu[0], f, nf - 1)

    grid_spec = pltpu.PrefetchScalarGridSpec(
        num_scalar_prefetch=2,
        grid=(p // tm, nf),
        in_specs=[pl.BlockSpec((tm, d), row),
                  pl.BlockSpec((1, d), lambda i, f, te, nu: (0, 0)),
                  pl.BlockSpec((1, d, tf), lambda i, f, te, nu: (te[i], 0, fcol(i, f, nu))),
                  pl.BlockSpec((1, d, tf), lambda i, f, te, nu: (te[i], 0, fcol(i, f, nu))),
                  pl.BlockSpec((1, tf, d), lambda i, f, te, nu: (te[i], fcol(i, f, nu), 0))],
        out_specs=pl.BlockSpec((tm, d), lambda i, f, te, nu: (i, 0)),
        scratch_shapes=[pltpu.VMEM((tm, d), BF16), pltpu.VMEM((tm, d), F32)],
    )
    return pl.pallas_call(
        _moe_body,
        grid_spec=grid_spec,
        out_shape=jax.ShapeDtypeStruct((p, d), F32),
        compiler_params=_params(("arbitrary", "arbitrary"), vm),
        name="moe_experts",
    )(tile_expert, n_used, xg, gain.reshape(1, d), w_gate, w_up, w_down)


def _rope_tables(positions, dim, reps):
    inv_freq = ROPE_THETA ** (-jnp.arange(0, dim, 2, dtype=F32) / dim)
    ang = positions.astype(F32)[:, None] * inv_freq
    cos, sin = jnp.cos(ang), jnp.sin(ang)
    return jnp.tile(cos, (1, 2 * reps)), jnp.tile(jnp.concatenate([-sin, sin], axis=1), (1, reps))


def _split_w_in(w):
    d = w.shape[0]
    o_a = 4 * DN_WIDTH
    o_qsa = o_a + 2 * DN_HEADS
    o_kix = o_qsa + 3 * SA_WIDTH + IDX_HEADS * IDX_HEAD_DIM
    o_wix = o_kix + IDX_HEAD_DIM
    o_g = o_wix + IDX_HEADS
    main = jnp.concatenate([w[:, :o_a], w[:, o_g:]], axis=1)
    small = jnp.concatenate([w[:, o_kix:o_wix], w[:, o_a:o_qsa], w[:, o_wix:o_g],
                             jnp.zeros((d, LANES - IDX_HEAD_DIM - 2 * DN_HEADS - IDX_HEADS), w.dtype)], axis=1)
    return main.astype(BF16), w[:, o_qsa:o_kix].astype(BF16), small.astype(BF16)


def _pad_lanes(v, offset):
    return jnp.zeros((1, LANES), F32).at[0, offset:offset + v.shape[0]].set(v.astype(F32))


def _mixer(x, h, cos_sa, sin_sa, cos_ix, sin_ix, ffn_gain, w_in, conv_w, a_log, dt_bias, dn_norm,
           w_dn_out, w_sa_out, w_o):
    s = x.shape[0]
    w_main, w_att, w_small = _split_w_in(w_in)
    proj = matmul(h, w_main, F32, tm=min(1024, s), tn=1024, name="in_proj")
    small = matmul(h, w_small, F32, tm=min(1024, s), tn=LANES, name="in_proj_small")

    qd, kd, w, u, att, gl = dn_chunk(proj, small, conv_w.astype(F32), _pad_lanes(a_log, SM_A), _pad_lanes(dt_bias, SM_A))
    o_dn = dn_scan(qd, kd, w, u, att, gl, proj, dn_norm.astype(F32))

    q_sa = proj_rope(h, w_att, BLK_QSA, cos_sa, sin_sa, SA_HEAD_DIM, scale=SA_HEAD_DIM ** -0.5 * LOG2_E)
    k_sa = proj_rope(h, w_att, BLK_KSA, cos_sa, sin_sa, SA_HEAD_DIM)
    v_sa = matmul(h, w_att[:, BLK_VSA * SA_WIDTH:(BLK_VSA + 1) * SA_WIDTH], BF16, tm=min(512, s), tn=SA_WIDTH, name="v_proj")
    q_ix = proj_rope(h, w_att, BLK_QIX, cos_ix, sin_ix, IDX_HEAD_DIM)
    k_lo, k_hi = rope_kix(small, cos_ix, sin_ix)
    t_attn = min(512, s)
    mask_t = index_mask(q_ix, k_lo, k_hi, small, min(TOPK_MAX, s // 4), tq=min(256, s), tk=t_attn, t_attn=t_attn)
    o_sa = masked_attention(q_sa, k_sa, v_sa, mask_t, t=t_attn)

    merged = merge_branches(o_dn, o_sa, w_dn_out.astype(BF16), w_sa_out.astype(BF16), proj)
    return out_proj_norm(merged, w_o.astype(BF16), x, ffn_gain.astype(F32))


def _moe_layer(x, h, norm_gain, w_router, w_gate, w_up, w_down, next_gain, hn_dtype):
    s, d = x.shape
    tile = min(512, s)
    n_tiles = (2 * s) // tile + N_EXPERTS
    wr = jnp.zeros((d, LANES), BF16).at[:, :N_EXPERTS].set(w_router.astype(BF16))
    w, m1, m2, rank, totals = route_tokens(h, wr)
    pw, meta = dispatch_plan(w, m1, m2, rank, totals, tile)
    pos1 = pw[:, 0].astype(I32)
    pos2 = pw[:, 1].astype(I32)
    tile_expert = meta[0, :n_tiles].astype(I32)
    n_used = meta[1, :1].astype(I32)
    xg = dispatch_rows(x, pos1, pos2, n_tiles * tile)
    y = moe_experts(xg, norm_gain.astype(F32), tile_expert, n_used, w_gate.astype(BF16), w_up.astype(BF16),
                    w_down.astype(BF16), tm=tile)
    return combine_rows(x, pw, y, pos1, pos2, next_gain.astype(F32), hn_dtype)


def kernel(x, positions, norm_mix, w_in, conv_w, a_log, dt_bias, dn_norm, w_dn_out, w_sa_out, w_o, norm_ffn, dense_w_gate, dense_w_up, dense_w_down, moe_router, moe_w_gate, moe_w_up, moe_w_down, final_norm):
    b, s, d = x.shape
    depth = norm_mix.shape[0]
    outs = []
    for bi in range(b):
        xb = x[bi]
        pos = positions[bi]
        cos_sa, sin_sa = _rope_tables(pos, SA_HEAD_DIM, 1)
        cos_ix, sin_ix = _rope_tables(pos, IDX_HEAD_DIM, 2)
        h = rmsnorm(xb, norm_mix[0], BF16)
        for layer in range(depth):
            xb, h = _mixer(xb, h, cos_sa, sin_sa, cos_ix, sin_ix, norm_ffn[layer], w_in[layer], conv_w[layer],
                           a_log[layer], dt_bias[layer], dn_norm[layer], w_dn_out[layer], w_sa_out[layer], w_o[layer])
            last = layer == depth - 1
            next_gain = (final_norm if last else norm_mix[layer + 1]).astype(F32)
            hn_dtype = x.dtype if last else BF16
            j = layer // 2
            if layer % 2 == 0:
                xb, h = ffn_dense(h, xb, dense_w_gate[j].astype(BF16), dense_w_up[j].astype(BF16),
                                  dense_w_down[j].astype(BF16), next_gain, hn_dtype)
            else:
                xb, h = _moe_layer(xb, h, norm_ffn[layer], moe_router[j], moe_w_gate[j], moe_w_up[j], moe_w_down[j],
                                   next_gain, hn_dtype)
        outs.append(h)
    return jnp.stack(outs, axis=0)
```

```python
import functools

import jax
import jax.numpy as jnp
from jax import lax
from jax.experimental import pallas as pl
from jax.experimental.pallas import tpu as pltpu

F32 = jnp.float32
BF16 = jnp.bfloat16
I32 = jnp.int32
I16 = jnp.int16

RMS_EPS = 1e-6
L2_EPS = 1e-6
DN_HEADS = 8
DN_HEAD_DIM = 128
DN_WIDTH = DN_HEADS * DN_HEAD_DIM
CONV_WIDTH = 4
DN_CHUNK = 64
SA_HEADS = 8
SA_HEAD_DIM = 128
SA_WIDTH = SA_HEADS * SA_HEAD_DIM
IDX_HEADS = 16
IDX_HEAD_DIM = 64
TOPK_MAX = 256
ROPE_THETA = 10000.0
N_EXPERTS = 8

LANES = 128
SUBLANES = 8
VMEM_CAP_BYTES = 56 * 2**20
NEG_BIG = -1e30
I16_MIN = -2**15
I16_ROWS = 16
LOG2_E = 1.4426950408889634

COL_QKV = 0
COL_Z = 3072
COL_GDN = 4096
COL_GSA = 6144
BLK_QSA, BLK_KSA, BLK_VSA, BLK_QIX = 0, 1, 2, 3
SM_A = 64
SM_B = 72
SM_WIX = 80


def _params(semantics, vmem_bytes):
    return pltpu.CompilerParams(dimension_semantics=semantics,
                                vmem_limit_bytes=int(min(max(vmem_bytes, 16 * 2**20), VMEM_CAP_BYTES)))


def _nbytes(shape, dtype):
    n = 1
    for s in shape:
        n *= s
    return n * jnp.dtype(dtype).itemsize


def _sigmoid(x):
    return 1.0 / (1.0 + jnp.exp(-x))


def _dot(a, b):
    return jnp.dot(a, b, preferred_element_type=F32)


def _dot_nt(a, b):
    return lax.dot_general(a, b, (((1,), (1,)), ((), ())), preferred_element_type=F32)


def _dot_tn(a, b):
    return lax.dot_general(a, b, (((0,), (0,)), ((), ())), preferred_element_type=F32)


def _split_bf16(x):
    hi = x.astype(BF16)
    return hi, (x - hi.astype(F32)).astype(BF16)


def _dot3(a, b):
    return _dot(a[0], b[0]) + _dot(a[1], b[0]) + _dot(a[0], b[1])


def _rms(x, gain):
    ms = jnp.mean(x * x, axis=-1, keepdims=True)
    return x * lax.rsqrt(ms + RMS_EPS) * gain


def _rmsnorm_body(x_ref, g_ref, o_ref):
    o_ref[...] = _rms(x_ref[...], g_ref[...]).astype(o_ref.dtype)


def rmsnorm(x, gain, out_dtype, tm=512):
    m, d = x.shape
    vm = 2 * (_nbytes((tm, d), F32) + _nbytes((tm, d), out_dtype)) + 4 * _nbytes((tm, d), F32)
    return pl.pallas_call(
        _rmsnorm_body,
        grid=(m // tm,),
        in_specs=[pl.BlockSpec((tm, d), lambda i: (i, 0)),
                  pl.BlockSpec((1, d), lambda i: (0, 0))],
        out_specs=pl.BlockSpec((tm, d), lambda i: (i, 0)),
        out_shape=jax.ShapeDtypeStruct((m, d), out_dtype),
        compiler_params=_params(("parallel",), vm),
        name="rmsnorm",
    )(x, gain.reshape(1, d))


def _mm_body(a_ref, b_ref, o_ref):
    o_ref[...] = _dot(a_ref[...], b_ref[...]).astype(o_ref.dtype)


def matmul(a, b, out_dtype, tm, tn, name):
    m, k = a.shape
    n = b.shape[1]
    vm = 2 * (_nbytes((tm, k), a.dtype) + _nbytes((k, tn), b.dtype) + _nbytes((tm, tn), out_dtype))
    vm += 2 * _nbytes((tm, tn), F32)
    return pl.pallas_call(
        _mm_body,
        grid=(m // tm, n // tn),
        in_specs=[pl.BlockSpec((tm, k), lambda i, j: (i, 0)),
                  pl.BlockSpec((k, tn), lambda i, j: (0, j))],
        out_specs=pl.BlockSpec((tm, tn), lambda i, j: (i, j)),
        out_shape=jax.ShapeDtypeStruct((m, n), out_dtype),
        compiler_params=_params(("parallel", "parallel"), vm),
        name=name,
    )(a, b)


def _dn_chunk_body(xc_ref, xp_ref, sm_ref, cw_ref, alog_ref, dtb_ref,
                   qd_ref, kd_ref, w_ref, u_ref, att_ref, gl_ref, ext_ref):
    c = pl.program_id(0)
    C = DN_CHUNK
    halo = SUBLANES
    ext_ref[0:halo, :] = jnp.where(c > 0, xp_ref[...], 0.0)
    ext_ref[halo:halo + C, :] = xc_ref[...]
    cw = cw_ref[...]
    y = cw[0:1, :] * ext_ref[pl.ds(halo - CONV_WIDTH + 1, C), :]
    for j in range(1, CONV_WIDTH):
        y = y + cw[j:j + 1, :] * ext_ref[pl.ds(halo - CONV_WIDTH + 1 + j, C), :]
    y = y * _sigmoid(y)

    sm = sm_ref[...]
    xa = sm + dtb_ref[...]
    softplus = jnp.maximum(xa, 0.0) + jnp.log1p(jnp.exp(-jnp.abs(xa)))
    g = -jnp.exp(alog_ref[...]) * softplus
    beta = _sigmoid(sm)

    row = lax.broadcasted_iota(I32, (C, LANES), 0)
    gc = g
    d = 1
    while d < C:
        gc = gc + jnp.where(row >= d, pltpu.roll(gc, d, axis=0), 0.0)
        d *= 2
    gct = jnp.concatenate([gc, jnp.zeros_like(gc)], axis=0).T
    ex = jnp.exp(gc)
    gc_last = gc[C - 1:C, :]
    exl = jnp.exp(gc_last - gc)
    gl_ref[...] = jnp.exp(jnp.broadcast_to(gct[SM_A:SM_A + DN_HEADS, C - 1:C], (DN_HEADS, LANES)))

    ri = lax.broadcasted_iota(I32, (C, C), 0)
    ci = lax.broadcasted_iota(I32, (C, C), 1)
    tril = ri >= ci
    strict = ri > ci
    eye = jnp.where(ri == ci, 1.0, 0.0).astype(F32)
    lvl_masks = []
    lb = 0
    while (1 << lb) < C:
        lvl_masks.append(((ri >> (lb + 1)) == (ci >> (lb + 1)))
                         & (((ri >> lb) & 1) == 1) & (((ci >> lb) & 1) == 0))
        lb += 1

    heads = range(DN_HEADS)
    a_mats, kbs, vbs = [], [], []
    for h in heads:
        sl = slice(h * DN_HEAD_DIM, (h + 1) * DN_HEAD_DIM)
        qh = y[:, h * DN_HEAD_DIM:(h + 1) * DN_HEAD_DIM]
        kh = y[:, DN_WIDTH + h * DN_HEAD_DIM:DN_WIDTH + (h + 1) * DN_HEAD_DIM]
        vh = y[:, 2 * DN_WIDTH + h * DN_HEAD_DIM:2 * DN_WIDTH + (h + 1) * DN_HEAD_DIM]
        qn = qh * lax.rsqrt(jnp.sum(qh * qh, axis=-1, keepdims=True) + L2_EPS) * (DN_HEAD_DIM ** -0.5)
        kn = kh * lax.rsqrt(jnp.sum(kh * kh, axis=-1, keepdims=True) + L2_EPS)
        bcol = beta[:, SM_B + h:SM_B + h + 1]
        gcol = gc[:, SM_A + h:SM_A + h + 1]
        grow = gct[SM_A + h:SM_A + h + 1, 0:C]
        dec = jnp.exp(jnp.where(tril, gcol - grow, -jnp.inf))
        kb = kn * bcol
        knb = kn.astype(BF16)
        excol = ex[:, SM_A + h:SM_A + h + 1]
        a_mats.append(jnp.where(strict, _dot_nt(kb.astype(BF16), knb) * dec, 0.0))
        kbs.append(kb * excol)
        vbs.append(vh * bcol)
        att = jnp.where(tril, _dot_nt(qn.astype(BF16), knb) * dec, 0.0)
        att_ref[:, h * C:(h + 1) * C] = att.astype(att_ref.dtype)
        qd_ref[:, sl] = (qn * excol).astype(qd_ref.dtype)
        kd_ref[:, sl] = (kn * exl[:, SM_A + h:SM_A + h + 1]).astype(kd_ref.dtype)

    x_inv = [eye - jnp.where(lvl_masks[0], a, 0.0) for a in a_mats]
    for lm in lvl_masks[1:]:
        xs = [_split_bf16(x) for x in x_inv]
        ts = [_dot3(xs[h], _split_bf16(jnp.where(lm, a_mats[h], 0.0))) for h in heads]
        x_inv = [x_inv[h] - _dot3(_split_bf16(ts[h]), xs[h]) for h in heads]
    xs = [_split_bf16(x) for x in x_inv]
    for h in heads:
        sl = slice(h * DN_HEAD_DIM, (h + 1) * DN_HEAD_DIM)
        w_ref[:, sl] = _dot3(xs[h], _split_bf16(kbs[h])).astype(w_ref.dtype)
        u_ref[:, sl] = _dot3(xs[h], _split_bf16(vbs[h]))


def dn_chunk(proj, small, conv_w, alog_pad, dtb_pad):
    s = proj.shape[0]
    C = DN_CHUNK
    n_chunks = s // C
    w3 = 3 * DN_WIDTH
    row_spec = lambda width: pl.BlockSpec((C, width), lambda c: (c, 0))
    vm = 2 * (_nbytes((C, w3), F32) + _nbytes((SUBLANES, w3), F32)) + 8 * _nbytes((C, w3), F32)
    return pl.pallas_call(
        _dn_chunk_body,
        grid=(n_chunks,),
        in_specs=[pl.BlockSpec((C, w3), lambda c: (c, COL_QKV // w3)),
                  pl.BlockSpec((SUBLANES, w3), lambda c: (jnp.maximum(c * (C // SUBLANES) - 1, 0), COL_QKV // w3)),
                  pl.BlockSpec((C, LANES), lambda c: (c, 0)),
                  pl.BlockSpec((CONV_WIDTH, w3), lambda c: (0, 0)),
                  pl.BlockSpec((1, LANES), lambda c: (0, 0)),
                  pl.BlockSpec((1, LANES), lambda c: (0, 0))],
        out_specs=[row_spec(DN_WIDTH), row_spec(DN_WIDTH), row_spec(DN_WIDTH),
                   row_spec(DN_WIDTH), row_spec(DN_HEADS * C),
                   pl.BlockSpec((DN_HEADS, LANES), lambda c: (c, 0))],
        out_shape=[jax.ShapeDtypeStruct((s, DN_WIDTH), BF16),
                   jax.ShapeDtypeStruct((s, DN_WIDTH), BF16),
                   jax.ShapeDtypeStruct((s, DN_WIDTH), BF16),
                   jax.ShapeDtypeStruct((s, DN_WIDTH), F32),
                   jax.ShapeDtypeStruct((s, DN_HEADS * C), BF16),
                   jax.ShapeDtypeStruct((n_chunks * DN_HEADS, LANES), F32)],
        scratch_shapes=[pltpu.VMEM((SUBLANES + C, w3), F32)],
        compiler_params=_params(("parallel",), vm),
        name="dn_chunk",
    )(proj, proj, small, conv_w, alog_pad, dtb_pad)


def _dn_scan_body(qd_ref, kd_ref, w_ref, u_ref, att_ref, gl_ref, z_ref, nrm_ref, o_ref, st_ref):
    c = pl.program_id(0)
    C = DN_CHUNK

    @pl.when(c == 0)
    def _():
        st_ref[...] = jnp.zeros_like(st_ref)

    heads = range(DN_HEADS)
    sls = [slice(h * DN_HEAD_DIM, (h + 1) * DN_HEAD_DIM) for h in heads]
    states = [st_ref[h] for h in heads]
    sbs = [s.astype(BF16) for s in states]
    w_s = [_dot(w_ref[:, sls[h]], sbs[h]) for h in heads]
    q_s = [_dot(qd_ref[:, sls[h]], sbs[h]) for h in heads]
    vbs = [(u_ref[:, sls[h]] - w_s[h]).astype(BF16) for h in heads]
    outs = [q_s[h] + _dot(att_ref[:, h * C:(h + 1) * C], vbs[h]) for h in heads]
    for h in heads:
        st_ref[h] = states[h] * gl_ref[h:h + 1, :] + _dot_tn(kd_ref[:, sls[h]], vbs[h])
    for h in heads:
        o = outs[h]
        ms = jnp.mean(o * o, axis=-1, keepdims=True)
        z = z_ref[:, sls[h]]
        o_ref[:, sls[h]] = (o * lax.rsqrt(ms + RMS_EPS) * nrm_ref[...] * (z * _sigmoid(z))).astype(o_ref.dtype)


def dn_scan(qd, kd, w, u, att, gl, proj, dn_norm):
    s = qd.shape[0]
    C = DN_CHUNK
    row = lambda width: pl.BlockSpec((C, width), lambda c: (c, 0))
    vm = 2 * 6 * _nbytes((C, DN_WIDTH), F32) + 2 * _nbytes((DN_HEADS, DN_HEAD_DIM, DN_HEAD_DIM), F32)
    return pl.pallas_call(
        _dn_scan_body,
        grid=(s // C,),
        in_specs=[row(DN_WIDTH), row(DN_WIDTH), row(DN_WIDTH), row(DN_WIDTH), row(DN_HEADS * C),
                  pl.BlockSpec((DN_HEADS, LANES), lambda c: (c, 0)),
                  pl.BlockSpec((C, DN_WIDTH), lambda c: (c, COL_Z // DN_WIDTH)),
                  pl.BlockSpec((1, DN_HEAD_DIM), lambda c: (0, 0))],
        out_specs=row(DN_WIDTH),
        out_shape=jax.ShapeDtypeStruct((s, DN_WIDTH), BF16),
        scratch_shapes=[pltpu.VMEM((DN_HEADS, DN_HEAD_DIM, DN_HEAD_DIM), F32)],
        compiler_params=_params(("arbitrary",), vm),
        name="dn_scan",
    )(qd, kd, w, u, att, gl, proj, dn_norm.reshape(1, DN_HEAD_DIM))


def _rotate_half(x, cs, sn, head_dim, first):
    half = head_dim // 2
    if head_dim == LANES:
        swapped = pltpu.roll(x, half, axis=1)
    else:
        swapped = jnp.where(first, pltpu.roll(x, LANES - half, axis=1), pltpu.roll(x, half, axis=1))
    return x * cs + swapped * sn


def _proj_rope_body(a_ref, b_ref, c_ref, s_ref, o_ref, *, head_dim, scale):
    acc = _dot(a_ref[...], b_ref[...])
    cs = c_ref[...]
    sn = s_ref[...]
    lane = lax.broadcasted_iota(I32, cs.shape, 1)
    first = (lane & (head_dim - 1)) < head_dim // 2
    for j in range(acc.shape[1] // LANES):
        sl = slice(j * LANES, (j + 1) * LANES)
        o_ref[:, sl] = (_rotate_half(acc[:, sl], cs, sn, head_dim, first) * scale).astype(o_ref.dtype)


def proj_rope(h, w_att, col_block, cos_t, sin_t, head_dim, scale=1.0, tm=1024):
    s, d = h.shape
    tm = min(tm, s)
    n = SA_WIDTH
    vm = 2 * (_nbytes((tm, d), BF16) + _nbytes((d, n), BF16) + _nbytes((tm, n), BF16)) + 4 * _nbytes((tm, n), F32)
    tab = pl.BlockSpec((tm, LANES), lambda i: (i, 0))
    return pl.pallas_call(
        functools.partial(_proj_rope_body, head_dim=head_dim, scale=scale),
        grid=(s // tm,),
        in_specs=[pl.BlockSpec((tm, d), lambda i: (i, 0)),
                  pl.BlockSpec((d, n), lambda i: (0, col_block)), tab, tab],
        out_specs=pl.BlockSpec((tm, n), lambda i: (i, 0)),
        out_shape=jax.ShapeDtypeStruct((s, n), BF16),
        compiler_params=_params(("parallel",), vm),
        name="proj_rope",
    )(h, w_att, cos_t, sin_t)


def _rope_kix_body(sm_ref, c_ref, s_ref, klo_ref, khi_ref):
    cs = c_ref[...]
    lane = lax.broadcasted_iota(I32, cs.shape, 1)
    first = (lane & (IDX_HEAD_DIM - 1)) < IDX_HEAD_DIM // 2
    k_rot = _rotate_half(sm_ref[...], cs, s_ref[...], IDX_HEAD_DIM, first)
    k_lo = jnp.where(lane < IDX_HEAD_DIM, k_rot, 0.0)
    klo_ref[...] = k_lo.astype(klo_ref.dtype)
    khi_ref[...] = pltpu.roll(k_lo, IDX_HEAD_DIM, axis=1).astype(khi_ref.dtype)


def rope_kix(small, cos_t, sin_t, tm=512):
    s = small.shape[0]
    tm = min(tm, s)
    tab = pl.BlockSpec((tm, LANES), lambda i: (i, 0))
    return pl.pallas_call(
        _rope_kix_body,
        grid=(s // tm,),
        in_specs=[tab, tab, tab],
        out_specs=[tab, tab],
        out_shape=[jax.ShapeDtypeStruct((s, LANES), BF16)] * 2,
        compiler_params=_params(("parallel",), 16 * _nbytes((tm, LANES), F32)),
        name="rope_kix",
    )(small, cos_t, sin_t)


def _index_body(q_ref, klo_ref, khi_ref, sm_ref, mask_ref, keys_ref, hi_ref, lo_ref, *, tq, tk, topk):
    i = pl.program_id(0)
    nkt = keys_ref.shape[0]
    nk = ((i + 1) * tq + tk - 1) // tk
    w_t = (sm_ref[...] * (IDX_HEADS ** -0.5 * IDX_HEAD_DIM ** -0.5)).T
    key_l = lax.broadcasted_iota(I32, (tk, tq), 0)
    qry_g = i * tq + lax.broadcasted_iota(I32, (tk, tq), 1)

    def score_tile(kt, carry):
        off = pl.multiple_of(kt * tk, tk)
        k_lo = klo_ref[pl.ds(off, tk), :]
        k_hi = khi_ref[pl.ds(off, tk), :]
        acc = jnp.zeros((tk, tq), F32)
        for j in range(IDX_HEADS // 2):
            qp = q_ref[:, j * LANES:(j + 1) * LANES]
            acc = acc + w_t[SM_WIX + 2 * j:SM_WIX + 2 * j + 1, :] * jnp.maximum(_dot_nt(k_lo, qp), 0.0)
            acc = acc + w_t[SM_WIX + 2 * j + 1:SM_WIX + 2 * j + 2, :] * jnp.maximum(_dot_nt(k_hi, qp), 0.0)
        sc = jnp.where(kt * tk + key_l <= qry_g, acc, -jnp.inf)
        bits = pltpu.bitcast(sc, I32)
        key = bits ^ ((bits >> 31) & 0x7FFFFFFF)
        keys_ref[kt] = key
        hi_ref[kt] = (key >> 16).astype(I16)
        return carry

    lax.fori_loop(0, nk, score_tile, 0)

    cnt_rows = 2 * I16_ROWS

    def count16(ref, cand, strict):
        def body(kt, cnt):
            t = ref[kt]
            hit = jnp.where((t > cand) if strict else (t >= cand), jnp.int16(1), jnp.int16(0))
            for j in range(tk // cnt_rows):
                cnt = cnt + hit[j * cnt_rows:(j + 1) * cnt_rows]
            return cnt

        cnt = lax.fori_loop(0, nk, body, jnp.zeros((cnt_rows, tq), I16))
        return jnp.sum(cnt.astype(F32), axis=0, keepdims=True)

    def kth_largest16(ref, kth):
        zero = jnp.zeros((1, tq), I32)
        ans = jnp.where(count16(ref, zero.astype(I16), False) >= kth, zero, I16_MIN)

        def bit_body(b, ans):
            cand = ans + lax.shift_left(jnp.int32(1), 14 - b)
            return jnp.where(count16(ref, cand.astype(I16), False) >= kth, cand, ans)

        return lax.fori_loop(0, 15, bit_body, ans)

    kf = jnp.full((1, tq), float(topk), F32)
    t_hi = kth_largest16(hi_ref, kf)
    above = count16(hi_ref, t_hi.astype(I16), True)

    def low_tile(kt, carry):
        key = keys_ref[kt]
        lo = (key & 0xFFFF) + I16_MIN
        lo_ref[kt] = jnp.where((key >> 16) == t_hi, lo, I16_MIN).astype(I16)
        return carry

    lax.fori_loop(0, nk, low_tile, 0)
    t_lo = kth_largest16(lo_ref, kf - above)
    ans = t_hi * 65536 + (t_lo - I16_MIN)

    def put(kt, sel):
        mask_ref[0, pl.ds(pl.multiple_of(kt * tk, tk), tk), :] = jnp.where(sel, 0.0, NEG_BIG).astype(mask_ref.dtype)

    def write(kt, cnt):
        sel = (keys_ref[kt] >= ans) & (kt * tk + key_l <= qry_g)
        put(kt, sel)
        return cnt + jnp.sum(jnp.where(sel, 1.0, 0.0).reshape(tk // cnt_rows, cnt_rows, tq), axis=0)

    kept = jnp.sum(lax.fori_loop(0, nk, write, jnp.zeros((cnt_rows, tq), F32)), axis=0, keepdims=True)

    @pl.when(jnp.max(kept) > float(topk))
    def _():
        def count_gt(kt, cnt):
            gt = (keys_ref[kt] > ans) & (kt * tk + key_l <= qry_g)
            return cnt + jnp.sum(jnp.where(gt, 1.0, 0.0), axis=0, keepdims=True)

        need = kf - lax.fori_loop(0, nk, count_gt, jnp.zeros((1, tq), F32))
        ri = lax.broadcasted_iota(I32, (tk, tk), 0)
        ci = lax.broadcasted_iota(I32, (tk, tk), 1)
        upto = jnp.where(ri >= ci, 1.0, 0.0).astype(BF16)

        def rewrite(kt, seen):
            key = keys_ref[kt]
            causal = kt * tk + key_l <= qry_g
            tie = (key == ans) & causal
            rank = seen + _dot(upto, jnp.where(tie, 1.0, 0.0).astype(BF16))
            put(kt, ((key > ans) & causal) | (tie & (rank <= need)))
            return rank[tk - 1:tk, :]

        lax.fori_loop(0, nk, rewrite, jnp.zeros((1, tq), F32))

    def clear(kt, carry):
        mask_ref[0, pl.ds(pl.multiple_of(kt * tk, tk), tk), :] = jnp.full((tk, tq), NEG_BIG, mask_ref.dtype)
        return carry

    lax.fori_loop(nk, nkt, clear, 0)


def index_mask(q_ix, k_lo, k_hi, small, topk, tq, tk, t_attn):
    s = q_ix.shape[0]
    nkt = s // tk
    per = t_attn // tq
    wq = IDX_HEADS * IDX_HEAD_DIM
    vm = (2 * (_nbytes((tq, wq), BF16) + 2 * _nbytes((s, LANES), BF16) + _nbytes((tq, LANES), F32)
               + _nbytes((s, tq), BF16)) + 2 * _nbytes((s, tq), I32) + 8 * _nbytes((tk, tq), F32))
    return pl.pallas_call(
        functools.partial(_index_body, tq=tq, tk=tk, topk=topk),
        grid=(s // tq,),
        in_specs=[pl.BlockSpec((tq, wq), lambda i: (i, 0)),
                  pl.BlockSpec((s, LANES), lambda i: (0, 0)),
                  pl.BlockSpec((s, LANES), lambda i: (0, 0)),
                  pl.BlockSpec((tq, LANES), lambda i: (i, 0))],
        out_specs=pl.BlockSpec((1, s, tq), lambda i: (i // per, 0, i % per)),
        out_shape=jax.ShapeDtypeStruct((s // t_attn, s, t_attn), BF16),
        scratch_shapes=[pltpu.VMEM((nkt, tk, tq), I32), pltpu.VMEM((nkt, tk, tq), I16), pltpu.VMEM((nkt, tk, tq), I16)],
        compiler_params=_params(("parallel",), vm),
        name="index_mask",
    )(q_ix, k_lo, k_hi, small)


def _attn_body(qi_ref, ki_ref, q_ref, k_ref, v_ref, mk_ref, o_ref, m_ref, l_ref, al_ref, acc_ref, bias_ref, s_ref):
    p = pl.program_id(0)
    qi = qi_ref[p]
    ki = ki_ref[p]

    @pl.when(ki == 0)
    def _():
        m_ref[...] = jnp.full_like(m_ref, NEG_BIG)
        l_ref[...] = jnp.zeros_like(l_ref)
        acc_ref[...] = jnp.zeros_like(acc_ref)

    bias_ref[...] = mk_ref[0].astype(F32)
    tk, tq = bias_ref.shape
    part = 4 * SUBLANES

    def fold(x, op):
        return op(op(x.reshape(tk // part, part, tq), axis=0), axis=0, keepdims=True)

    heads = range(SA_HEADS)
    sls = [slice(h * SA_HEAD_DIM, (h + 1) * SA_HEAD_DIM) for h in heads]
    for h in heads:
        s = _dot_nt(k_ref[:, sls[h]], q_ref[:, sls[h]]) + bias_ref[...]
        s_ref[h] = s
        m_old = m_ref[h:h + 1, :]
        m_new = jnp.maximum(m_old, fold(s, jnp.max))
        al_ref[h:h + 1, :] = jnp.exp2(m_old - m_new)
        m_ref[h:h + 1, :] = m_new
    for h in heads:
        pr = jnp.exp2(s_ref[h] - m_ref[h:h + 1, :])
        alpha = al_ref[h:h + 1, :]
        l_ref[h:h + 1, :] = alpha * l_ref[h:h + 1, :] + fold(pr, jnp.sum)
        acc_ref[h] = alpha * acc_ref[h] + _dot_tn(v_ref[:, sls[h]], pr.astype(BF16))

    @pl.when(ki == qi)
    def _():
        for h in range(SA_HEADS):
            sl = slice(h * SA_HEAD_DIM, (h + 1) * SA_HEAD_DIM)
            o_ref[:, sl] = (acc_ref[h] / l_ref[h:h + 1, :]).T.astype(o_ref.dtype)


def masked_attention(q, k, v, mask_t, t):
    s = q.shape[0]
    nb = s // t
    pairs = [(a, b) for a in range(nb) for b in range(a + 1)]
    qi = jnp.asarray([a for a, _ in pairs], I32)
    ki = jnp.asarray([b for _, b in pairs], I32)
    vm = (2 * (4 * _nbytes((t, SA_WIDTH), BF16) + _nbytes((t, t), BF16)) + _nbytes((t, SA_WIDTH), F32)
          + 2 * _nbytes((SA_HEADS, t), F32) + 8 * _nbytes((t, t), F32))
    grid_spec = pltpu.PrefetchScalarGridSpec(
        num_scalar_prefetch=2,
        grid=(len(pairs),),
        in_specs=[pl.BlockSpec((t, SA_WIDTH), lambda p, qi, ki: (qi[p], 0)),
                  pl.BlockSpec((t, SA_WIDTH), lambda p, qi, ki: (ki[p], 0)),
                  pl.BlockSpec((t, SA_WIDTH), lambda p, qi, ki: (ki[p], 0)),
                  pl.BlockSpec((1, t, t), lambda p, qi, ki: (qi[p], ki[p], 0))],
        out_specs=pl.BlockSpec((t, SA_WIDTH), lambda p, qi, ki: (qi[p], 0)),
        scratch_shapes=[pltpu.VMEM((SA_HEADS, t), F32),
                        pltpu.VMEM((SA_HEADS, t), F32),
                        pltpu.VMEM((SA_HEADS, t), F32),
                        pltpu.VMEM((SA_HEADS, SA_HEAD_DIM, t), F32),
                        pltpu.VMEM((t, t), F32),
                        pltpu.VMEM((SA_HEADS, t, t), F32)],
    )
    return pl.pallas_call(
        _attn_body,
        grid_spec=grid_spec,
        out_shape=jax.ShapeDtypeStruct((s, SA_WIDTH), BF16),
        compiler_params=_params(("arbitrary",), vm),
        name="masked_attention",
    )(qi, ki, q, k, v, mask_t)


def _merge_body(odn_ref, osa_ref, wdn_ref, wsa_ref, gdn_ref, gsa_ref, o_ref):
    y_dn = _dot(odn_ref[...], wdn_ref[...])
    y_sa = _dot(osa_ref[...], wsa_ref[...])
    o_ref[...] = (_sigmoid(gdn_ref[...]) * y_dn + _sigmoid(gsa_ref[...]) * y_sa).astype(o_ref.dtype)


def merge_branches(o_dn, o_sa, w_dn, w_sa, proj, tm=512, tn=1024):
    s, kd = o_dn.shape
    d = w_dn.shape[1]
    vm = 2 * (2 * _nbytes((tm, kd), BF16) + 2 * _nbytes((kd, tn), BF16) + 2 * _nbytes((tm, tn), F32)
              + _nbytes((tm, tn), BF16)) + 6 * _nbytes((tm, tn), F32)
    return pl.pallas_call(
        _merge_body,
        grid=(s // tm, d // tn),
        in_specs=[pl.BlockSpec((tm, kd), lambda i, j: (i, 0)),
                  pl.BlockSpec((tm, kd), lambda i, j: (i, 0)),
                  pl.BlockSpec((kd, tn), lambda i, j: (0, j)),
                  pl.BlockSpec((kd, tn), lambda i, j: (0, j)),
                  pl.BlockSpec((tm, tn), lambda i, j: (i, COL_GDN // tn + j)),
                  pl.BlockSpec((tm, tn), lambda i, j: (i, COL_GSA // tn + j))],
        out_specs=pl.BlockSpec((tm, tn), lambda i, j: (i, j)),
        out_shape=jax.ShapeDtypeStruct((s, d), BF16),
        compiler_params=_params(("parallel", "parallel"), vm),
        name="merge_branches",
    )(o_dn, o_sa, w_dn, w_sa, proj, proj)


def _out_proj_body(m_ref, w_ref, x_ref, gn_ref, o_ref, hn_ref):
    y = x_ref[...] + _dot(m_ref[...], w_ref[...])
    o_ref[...] = y
    hn_ref[...] = _rms(y, gn_ref[...]).astype(hn_ref.dtype)


def out_proj_norm(merged, w_o, x, next_gain, tm=512):
    s, k = merged.shape
    d = w_o.shape[1]
    tm = min(tm, s)
    vm = 2 * (_nbytes((tm, k), BF16) + _nbytes((k, d), BF16) + 2 * _nbytes((tm, d), F32) + _nbytes((tm, d), BF16)) + 3 * _nbytes((tm, d), F32)
    row = lambda width: pl.BlockSpec((tm, width), lambda i: (i, 0))
    return pl.pallas_call(
        _out_proj_body,
        grid=(s // tm,),
        in_specs=[row(k), pl.BlockSpec((k, d), lambda i: (0, 0)), row(d), pl.BlockSpec((1, d), lambda i: (0, 0))],
        out_specs=[row(d), row(d)],
        out_shape=[jax.ShapeDtypeStruct((s, d), F32), jax.ShapeDtypeStruct((s, d), BF16)],
        compiler_params=_params(("parallel",), vm),
        name="out_proj",
    )(merged, w_o, x, next_gain.reshape(1, d))


def _ffn_body(h_ref, x_ref, wg_ref, wu_ref, wd_ref, gn_ref, o_ref, hn_ref):
    f = pl.program_id(1)

    @pl.when(f == 0)
    def _():
        o_ref[...] = jnp.zeros_like(o_ref)

    h = h_ref[...]
    g = _dot(h, wg_ref[...])
    u = _dot(h, wu_ref[...])
    o_ref[...] += _dot((g * _sigmoid(g) * u).astype(BF16), wd_ref[...])

    @pl.when(f == pl.num_programs(1) - 1)
    def _():
        y = x_ref[...] + o_ref[...]
        o_ref[...] = y
        hn_ref[...] = _rms(y, gn_ref[...]).astype(hn_ref.dtype)


def ffn_dense(h, x, w_gate, w_up, w_down, next_gain, hn_dtype, tm=512, tf=512):
    s, d = h.shape
    ff = w_gate.shape[1]
    vm = (2 * (_nbytes((tm, d), BF16) + _nbytes((tm, d), hn_dtype) + 2 * _nbytes((tm, d), F32)
               + 3 * _nbytes((d, tf), BF16)) + 3 * _nbytes((tm, d), F32) + 6 * _nbytes((tm, tf), F32))
    row = pl.BlockSpec((tm, d), lambda i, f: (i, 0))
    return pl.pallas_call(
        _ffn_body,
        grid=(s // tm, ff // tf),
        in_specs=[row, row,
                  pl.BlockSpec((d, tf), lambda i, f: (0, f)),
                  pl.BlockSpec((d, tf), lambda i, f: (0, f)),
                  pl.BlockSpec((tf, d), lambda i, f: (f, 0)),
                  pl.BlockSpec((1, d), lambda i, f: (0, 0))],
        out_specs=[row, row],
        out_shape=[jax.ShapeDtypeStruct((s, d), F32), jax.ShapeDtypeStruct((s, d), hn_dtype)],
        compiler_params=_params(("parallel", "arbitrary"), vm),
        name="ffn_dense",
    )(h, x, w_gate, w_up, w_down, next_gain.reshape(1, d))


def _router_body(h_ref, wr_ref, w_ref, m1_ref, m2_ref, rank_ref, tot_ref, cnt_ref):
    logits = _dot(h_ref[...], wr_ref[...])
    lane = lax.broadcasted_iota(I32, logits.shape, 1)
    lg = jnp.where(lane < N_EXPERTS, logits, -jnp.inf)
    m1 = jnp.max(lg, axis=1, keepdims=True)
    i1 = jnp.min(jnp.where(lg == m1, lane, LANES), axis=1, keepdims=True)
    lg2 = jnp.where(lane == i1, -jnp.inf, lg)
    m2 = jnp.max(lg2, axis=1, keepdims=True)
    i2 = jnp.min(jnp.where(lg2 == m2, lane, LANES), axis=1, keepdims=True)
    e = jnp.exp(m2 - m1)
    first = lane == i1
    second = lane == i2
    w_ref[...] = jnp.where(first, 1.0 / (1.0 + e), 0.0) + jnp.where(second, e / (1.0 + e), 0.0)
    m1_ref[...] = jnp.where(first, 1.0, 0.0)
    m2_ref[...] = jnp.where(second, 1.0, 0.0)

    @pl.when(pl.program_id(0) == 0)
    def _():
        cnt_ref[...] = jnp.zeros_like(cnt_ref)

    tm = logits.shape[0]
    sel = jnp.where(first | second, 1.0, 0.0).astype(BF16)
    ri = lax.broadcasted_iota(I32, (tm, tm), 0)
    ci = lax.broadcasted_iota(I32, (tm, tm), 1)
    before = jnp.where(ri > ci, 1.0, 0.0).astype(BF16)
    run = cnt_ref[0:1, :]
    rank_ref[...] = run + _dot(before, sel)
    run = run + jnp.sum(sel.astype(F32), axis=0, keepdims=True)
    cnt_ref[...] = jnp.broadcast_to(run, cnt_ref.shape)
    tot_ref[...] = jnp.broadcast_to(run, tot_ref.shape)


def route_tokens(h, w_router_pad, tm=512):
    s, d = h.shape
    tm = min(tm, s)
    vm = 2 * (_nbytes((tm, d), BF16) + _nbytes((d, LANES), BF16) + 4 * _nbytes((tm, LANES), F32)) + 4 * _nbytes((tm, tm), F32)
    row = pl.BlockSpec((tm, LANES), lambda i: (i, 0))
    return pl.pallas_call(
        _router_body,
        grid=(s // tm,),
        in_specs=[pl.BlockSpec((tm, d), lambda i: (i, 0)),
                  pl.BlockSpec((d, LANES), lambda i: (0, 0))],
        out_specs=[row, row, row, row, pl.BlockSpec((SUBLANES, LANES), lambda i: (0, 0))],
        out_shape=[jax.ShapeDtypeStruct((s, LANES), F32)] * 4 + [jax.ShapeDtypeStruct((SUBLANES, LANES), F32)],
        scratch_shapes=[pltpu.VMEM((SUBLANES, LANES), F32)],
        compiler_params=_params(("arbitrary",), vm),
        name="moe_router",
    )(h, w_router_pad)


def _plan_body(w_ref, m1_ref, m2_ref, rank_ref, tot_ref, pw_ref, meta_ref, *, tile):
    lane8 = lax.broadcasted_iota(I32, (SUBLANES, LANES), 1)
    n = tot_ref[...]
    padded = jnp.floor((n + (tile - 1.0)) * (1.0 / tile)) * tile
    ends = padded
    d = 1
    while d < N_EXPERTS:
        ends = ends + jnp.where(lane8 >= d, pltpu.roll(ends, d, axis=1), 0.0)
        d *= 2
    start = (ends - padded)[0:1, :]
    posf = start + rank_ref[...]
    m1 = m1_ref[...]
    m2 = m2_ref[...]
    w = w_ref[...]
    lane = lax.broadcasted_iota(I32, w.shape, 1)
    cols = [jnp.sum(m1 * posf, axis=1, keepdims=True), jnp.sum(m2 * posf, axis=1, keepdims=True),
            jnp.sum(m1 * w, axis=1, keepdims=True), jnp.sum(m2 * w, axis=1, keepdims=True)]
    out = jnp.zeros_like(w)
    for j, col in enumerate(cols):
        out = jnp.where(lane == j, col, out)
    pw_ref[...] = out
    tile_start = lane8.astype(F32) * tile
    owner = jnp.zeros((SUBLANES, LANES), F32)
    for e in range(N_EXPERTS - 1):
        owner = owner + jnp.where(tile_start >= ends[:, e:e + 1], 1.0, 0.0)
    n_used = ends[:, N_EXPERTS - 1:N_EXPERTS] * (1.0 / tile)
    row8 = lax.broadcasted_iota(I32, (SUBLANES, LANES), 0)
    meta_ref[...] = jnp.where(row8 == 0, owner, jnp.broadcast_to(n_used, owner.shape))


def dispatch_plan(w, m1, m2, rank, totals, tile):
    s = w.shape[0]
    full = pl.BlockSpec((s, LANES), lambda: (0, 0))
    small = pl.BlockSpec((SUBLANES, LANES), lambda: (0, 0))
    vm = 2 * 5 * _nbytes((s, LANES), F32) + 8 * _nbytes((s, LANES), F32)
    return pl.pallas_call(
        functools.partial(_plan_body, tile=float(tile)),
        in_specs=[full, full, full, full, small],
        out_specs=[full, small],
        out_shape=[jax.ShapeDtypeStruct((s, LANES), F32), jax.ShapeDtypeStruct((SUBLANES, LANES), F32)],
        compiler_params=pltpu.CompilerParams(vmem_limit_bytes=int(min(vm, VMEM_CAP_BYTES))),
        name="moe_plan",
    )(w, m1, m2, rank, totals)


def _row_copy(src_ref, src_row, dst_ref, dst_row, sem):
    return pltpu.make_async_copy(src_ref.at[pl.ds(src_row, 1), :], dst_ref.at[pl.ds(dst_row, 1), :], sem)


def _dispatch_body(p1_ref, p2_ref, x_ref, xg_in_ref, xg_ref, sem, *, tt):
    del xg_in_ref
    base = pl.program_id(0) * tt

    def issue(r, carry):
        _row_copy(x_ref, r, xg_ref, p1_ref[base + r], sem).start()
        _row_copy(x_ref, r, xg_ref, p2_ref[base + r], sem).start()
        return carry

    lax.fori_loop(0, tt, issue, 0)

    def drain(r, carry):
        _row_copy(x_ref, 0, xg_ref, 0, sem).wait()
        _row_copy(x_ref, 0, xg_ref, 0, sem).wait()
        return carry

    lax.fori_loop(0, tt, drain, 0)


def dispatch_rows(x, pos1, pos2, n_rows, tt=512):
    s, d = x.shape
    tt = min(tt, s)
    grid_spec = pltpu.PrefetchScalarGridSpec(
        num_scalar_prefetch=2,
        grid=(s // tt,),
        in_specs=[pl.BlockSpec((tt, d), lambda i, p1, p2: (i, 0)),
                  pl.BlockSpec(memory_space=pl.ANY)],
        out_specs=pl.BlockSpec(memory_space=pl.ANY),
        scratch_shapes=[pltpu.SemaphoreType.DMA(())],
    )
    return pl.pallas_call(
        functools.partial(_dispatch_body, tt=tt),
        grid_spec=grid_spec,
        out_shape=jax.ShapeDtypeStruct((n_rows, d), x.dtype),
        input_output_aliases={3: 0},
        compiler_params=_params(("arbitrary",), 4 * _nbytes((tt, d), F32)),
        name="moe_dispatch",
    )(pos1, pos2, x, jnp.zeros((n_rows, d), x.dtype))


def _combine_body(p1_ref, p2_ref, x_ref, pw_ref, gn_ref, y_ref, o_ref, hn_ref, b1_ref, b2_ref, sem, *, tt):
    base = pl.program_id(0) * tt

    def issue(r, carry):
        _row_copy(y_ref, p1_ref[base + r], b1_ref, r, sem).start()
        _row_copy(y_ref, p2_ref[base + r], b2_ref, r, sem).start()
        return carry

    lax.fori_loop(0, tt, issue, 0)

    def drain(r, carry):
        _row_copy(y_ref, 0, b1_ref, 0, sem).wait()
        _row_copy(y_ref, 0, b2_ref, 0, sem).wait()
        return carry

    lax.fori_loop(0, tt, drain, 0)
    pw = pw_ref[...]
    y = x_ref[...] + pw[:, 2:3] * b1_ref[...] + pw[:, 3:4] * b2_ref[...]
    o_ref[...] = y
    hn_ref[...] = _rms(y, gn_ref[...]).astype(hn_ref.dtype)


def combine_rows(x, pw, y, pos1, pos2, next_gain, hn_dtype, tt=512):
    s, d = x.shape
    tt = min(tt, s)
    grid_spec = pltpu.PrefetchScalarGridSpec(
        num_scalar_prefetch=2,
        grid=(s // tt,),
        in_specs=[pl.BlockSpec((tt, d), lambda i, p1, p2: (i, 0)),
                  pl.BlockSpec((tt, LANES), lambda i, p1, p2: (i, 0)),
                  pl.BlockSpec((1, d), lambda i, p1, p2: (0, 0)),
                  pl.BlockSpec(memory_space=pl.ANY)],
        out_specs=[pl.BlockSpec((tt, d), lambda i, p1, p2: (i, 0)), pl.BlockSpec((tt, d), lambda i, p1, p2: (i, 0))],
        scratch_shapes=[pltpu.VMEM((tt, d), F32), pltpu.VMEM((tt, d), F32), pltpu.SemaphoreType.DMA(())],
    )
    return pl.pallas_call(
        functools.partial(_combine_body, tt=tt),
        grid_spec=grid_spec,
        out_shape=[jax.ShapeDtypeStruct((s, d), F32), jax.ShapeDtypeStruct((s, d), hn_dtype)],
        compiler_params=_params(("arbitrary",), 12 * _nbytes((tt, d), F32)),
        name="moe_combine",
    )(pos1, pos2, x, pw, next_gain.reshape(1, d), y)


def _moe_body(te_ref, nu_ref, xg_ref, gain_ref, wg_ref, wu_ref, wd_ref, o_ref, h_ref, acc_ref):
    i = pl.program_id(0)
    f = pl.program_id(1)

    @pl.when(i < nu_ref[0])
    def _():
        @pl.when(f == 0)
        def _():
            h_ref[...] = _rms(xg_ref[...], gain_ref[...]).astype(h_ref.dtype)
            acc_ref[...] = jnp.zeros_like(acc_ref)

        h = h_ref[...]
        g = _dot(h, wg_ref[0])
        u = _dot(h, wu_ref[0])
        acc_ref[...] += _dot((g * _sigmoid(g) * u).astype(BF16), wd_ref[0])

        @pl.when(f == pl.num_programs(1) - 1)
        def _():
            o_ref[...] = acc_ref[...]

    @pl.when((i >= nu_ref[0]) & (f == 0))
    def _():
        o_ref[...] = jnp.zeros_like(o_ref)


def moe_experts(xg, gain, tile_expert, n_used, w_gate, w_up, w_down, tm, tf=256):
    p, d = xg.shape
    ff = w_gate.shape[2]
    nf = ff // tf
    vm = (2 * (2 * _nbytes((tm, d), F32) + 3 * _nbytes((d, tf), BF16)) + _nbytes((tm, d), BF16)
          + _nbytes((tm, d), F32) + 6 * _nbytes((tm, tf), F32))

    def row(i, f, te, nu):
        return (jnp.minimum(i, nu[0] - 1), 0)

    def fcol(i, f, nu):
        return jnp.where(i < nu[0], f, nf - 1)

    grid_spec = pltpu.PrefetchScalarGridSpec(
        num_scalar_prefetch=2,
        grid=(p // tm, nf),
        in_specs=[pl.BlockSpec((tm, d), row),
                  pl.BlockSpec((1, d), lambda i, f, te, nu: (0, 0)),
                  pl.BlockSpec((1, d, tf), lambda i, f, te, nu: (te[i], 0, fcol(i, f, nu))),
                  pl.BlockSpec((1, d, tf), lambda i, f, te, nu: (te[i], 0, fcol(i, f, nu))),
                  pl.BlockSpec((1, tf, d), lambda i, f, te, nu: (te[i], fcol(i, f, nu), 0))],
        out_specs=pl.BlockSpec((tm, d), lambda i, f, te, nu: (i, 0)),
        scratch_shapes=[pltpu.VMEM((tm, d), BF16), pltpu.VMEM((tm, d), F32)],
    )
    return pl.pallas_call(
        _moe_body,
        grid_spec=grid_spec,
        out_shape=jax.ShapeDtypeStruct((p, d), F32),
        compiler_params=_params(("arbitrary", "arbitrary"), vm),
        name="moe_experts",
    )(tile_expert, n_used, xg, gain.reshape(1, d), w_gate, w_up, w_down)


def _rope_tables(positions, dim, reps):
    inv_freq = ROPE_THETA ** (-jnp.arange(0, dim, 2, dtype=F32) / dim)
    ang = positions.astype(F32)[:, None] * inv_freq
    cos, sin = jnp.cos(ang), jnp.sin(ang)
    return jnp.tile(cos, (1, 2 * reps)), jnp.tile(jnp.concatenate([-sin, sin], axis=1), (1, reps))


def _split_w_in(w):
    d = w.shape[0]
    o_a = 4 * DN_WIDTH
    o_qsa = o_a + 2 * DN_HEADS
    o_kix = o_qsa + 3 * SA_WIDTH + IDX_HEADS * IDX_HEAD_DIM
    o_wix = o_kix + IDX_HEAD_DIM
    o_g = o_wix + IDX_HEADS
    main = jnp.concatenate([w[:, :o_a], w[:, o_g:]], axis=1)
    small = jnp.concatenate([w[:, o_kix:o_wix], w[:, o_a:o_qsa], w[:, o_wix:o_g],
                             jnp.zeros((d, LANES - IDX_HEAD_DIM - 2 * DN_HEADS - IDX_HEADS), w.dtype)], axis=1)
    return main.astype(BF16), w[:, o_qsa:o_kix].astype(BF16), small.astype(BF16)


def _pad_lanes(v, offset):
    return jnp.zeros((1, LANES), F32).at[0, offset:offset + v.shape[0]].set(v.astype(F32))


def _mixer(x, h, cos_sa, sin_sa, cos_ix, sin_ix, ffn_gain, w_in, conv_w, a_log, dt_bias, dn_norm,
           w_dn_out, w_sa_out, w_o):
    s = x.shape[0]
    w_main, w_att, w_small = _split_w_in(w_in)
    proj = matmul(h, w_main, F32, tm=min(1024, s), tn=1024, name="in_proj")
    small = matmul(h, w_small, F32, tm=min(1024, s), tn=LANES, name="in_proj_small")

    qd, kd, w, u, att, gl = dn_chunk(proj, small, conv_w.astype(F32), _pad_lanes(a_log, SM_A), _pad_lanes(dt_bias, SM_A))
    o_dn = dn_scan(qd, kd, w, u, att, gl, proj, dn_norm.astype(F32))

    q_sa = proj_rope(h, w_att, BLK_QSA, cos_sa, sin_sa, SA_HEAD_DIM, scale=SA_HEAD_DIM ** -0.5 * LOG2_E)
    k_sa = proj_rope(h, w_att, BLK_KSA, cos_sa, sin_sa, SA_HEAD_DIM)
    v_sa = matmul(h, w_att[:, BLK_VSA * SA_WIDTH:(BLK_VSA + 1) * SA_WIDTH], BF16, tm=min(512, s), tn=SA_WIDTH, name="v_proj")
    q_ix = proj_rope(h, w_att, BLK_QIX, cos_ix, sin_ix, IDX_HEAD_DIM)
    k_lo, k_hi = rope_kix(small, cos_ix, sin_ix)
    t_attn = min(512, s)
    mask_t = index_mask(q_ix, k_lo, k_hi, small, min(TOPK_MAX, s // 4), tq=min(256, s), tk=t_attn, t_attn=t_attn)
    o_sa = masked_attention(q_sa, k_sa, v_sa, mask_t, t=t_attn)

    merged = merge_branches(o_dn, o_sa, w_dn_out.astype(BF16), w_sa_out.astype(BF16), proj)
    return out_proj_norm(merged, w_o.astype(BF16), x, ffn_gain.astype(F32))


def _moe_layer(x, h, norm_gain, w_router, w_gate, w_up, w_down, next_gain, hn_dtype):
    s, d = x.shape
    tile = min(512, s)
    n_tiles = (2 * s) // tile + N_EXPERTS
    wr = jnp.zeros((d, LANES), BF16).at[:, :N_EXPERTS].set(w_router.astype(BF16))
    w, m1, m2, rank, totals = route_tokens(h, wr)
    pw, meta = dispatch_plan(w, m1, m2, rank, totals, tile)
    pos1 = pw[:, 0].astype(I32)
    pos2 = pw[:, 1].astype(I32)
    tile_expert = meta[0, :n_tiles].astype(I32)
    n_used = meta[1, :1].astype(I32)
    xg = dispatch_rows(x, pos1, pos2, n_tiles * tile)
    y = moe_experts(xg, norm_gain.astype(F32), tile_expert, n_used, w_gate.astype(BF16), w_up.astype(BF16),
                    w_down.astype(BF16), tm=tile)
    return combine_rows(x, pw, y, pos1, pos2, next_gain.astype(F32), hn_dtype)


def kernel(x, positions, norm_mix, w_in, conv_w, a_log, dt_bias, dn_norm, w_dn_out, w_sa_out, w_o, norm_ffn, dense_w_gate, dense_w_up, dense_w_down, moe_router, moe_w_gate, moe_w_up, moe_w_down, final_norm):
    b, s, d = x.shape
    depth = norm_mix.shape[0]
    outs = []
    for bi in range(b):
        xb = x[bi]
        pos = positions[bi]
        cos_sa, sin_sa = _rope_tables(pos, SA_HEAD_DIM, 1)
        cos_ix, sin_ix = _rope_tables(pos, IDX_HEAD_DIM, 2)
        h = rmsnorm(xb, norm_mix[0], BF16)
        for layer in range(depth):
            xb, h = _mixer(xb, h, cos_sa, sin_sa, cos_ix, sin_ix, norm_ffn[layer], w_in[layer], conv_w[layer],
                           a_log[layer], dt_bias[layer], dn_norm[layer], w_dn_out[layer], w_sa_out[layer], w_o[layer])
            last = layer == depth - 1
            next_gain = (final_norm if last else norm_mix[layer + 1]).astype(F32)
            hn_dtype = x.dtype if last else BF16
            j = layer // 2
            if layer % 2 == 0:
                xb, h = ffn_dense(h, xb, dense_w_gate[j].astype(BF16), dense_w_up[j].astype(BF16),
                                  dense_w_down[j].astype(BF16), next_gain, hn_dtype)
            else:
                xb, h = _moe_layer(xb, h, norm_ffn[layer], moe_router[j], moe_w_gate[j], moe_w_up[j], moe_w_down[j],
                                   next_gain, hn_dtype)
        outs.append(h)
    return jnp.stack(outs, axis=0)
```

```python
import functools

import jax
import jax.numpy as jnp
from jax import lax
from jax.experimental import pallas as pl
from jax.experimental.pallas import tpu as pltpu

F32 = jnp.float32
BF16 = jnp.bfloat16
I32 = jnp.int32
I16 = jnp.int16

RMS_EPS = 1e-6
L2_EPS = 1e-6
DN_HEADS = 8
DN_HEAD_DIM = 128
DN_WIDTH = DN_HEADS * DN_HEAD_DIM
CONV_WIDTH = 4
DN_CHUNK = 64
SA_HEADS = 8
SA_HEAD_DIM = 128
SA_WIDTH = SA_HEADS * SA_HEAD_DIM
IDX_HEADS = 16
IDX_HEAD_DIM = 64
TOPK_MAX = 256
ROPE_THETA = 10000.0
N_EXPERTS = 8

LANES = 128
SUBLANES = 8
VMEM_CAP_BYTES = 56 * 2**20
NEG_BIG = -1e30
I16_MIN = -2**15
I16_ROWS = 16
LOG2_E = 1.4426950408889634

COL_QKV = 0
COL_Z = 3072
COL_GDN = 4096
COL_GSA = 6144
BLK_QSA, BLK_KSA, BLK_VSA, BLK_QIX = 0, 1, 2, 3
SM_A = 64
SM_B = 72
SM_WIX = 80


def _params(semantics, vmem_bytes):
    return pltpu.CompilerParams(dimension_semantics=semantics,
                                vmem_limit_bytes=int(min(max(vmem_bytes, 16 * 2**20), VMEM_CAP_BYTES)))


def _nbytes(shape, dtype):
    n = 1
    for s in shape:
        n *= s
    return n * jnp.dtype(dtype).itemsize


def _sigmoid(x):
    return 1.0 / (1.0 + jnp.exp(-x))


def _dot(a, b):
    return jnp.dot(a, b, preferred_element_type=F32)


def _dot_nt(a, b):
    return lax.dot_general(a, b, (((1,), (1,)), ((), ())), preferred_element_type=F32)


def _dot_tn(a, b):
    return lax.dot_general(a, b, (((0,), (0,)), ((), ())), preferred_element_type=F32)


def _split_bf16(x):
    hi = x.astype(BF16)
    return hi, (x - hi.astype(F32)).astype(BF16)


def _dot3(a, b):
    return _dot(a[0], b[0]) + _dot(a[1], b[0]) + _dot(a[0], b[1])


def _rms(x, gain):
    ms = jnp.mean(x * x, axis=-1, keepdims=True)
    return x * lax.rsqrt(ms + RMS_EPS) * gain


def _rmsnorm_body(x_ref, g_ref, o_ref):
    o_ref[...] = _rms(x_ref[...], g_ref[...]).astype(o_ref.dtype)


def rmsnorm(x, gain, out_dtype, tm=512):
    m, d = x.shape
    vm = 2 * (_nbytes((tm, d), F32) + _nbytes((tm, d), out_dtype)) + 4 * _nbytes((tm, d), F32)
    return pl.pallas_call(
        _rmsnorm_body,
        grid=(m // tm,),
        in_specs=[pl.BlockSpec((tm, d), lambda i: (i, 0)),
                  pl.BlockSpec((1, d), lambda i: (0, 0))],
        out_specs=pl.BlockSpec((tm, d), lambda i: (i, 0)),
        out_shape=jax.ShapeDtypeStruct((m, d), out_dtype),
        compiler_params=_params(("parallel",), vm),
        name="rmsnorm",
    )(x, gain.reshape(1, d))


def _mm_body(a_ref, b_ref, o_ref):
    o_ref[...] = _dot(a_ref[...], b_ref[...]).astype(o_ref.dtype)


def matmul(a, b, out_dtype, tm, tn, name):
    m, k = a.shape
    n = b.shape[1]
    vm = 2 * (_nbytes((tm, k), a.dtype) + _nbytes((k, tn), b.dtype) + _nbytes((tm, tn), out_dtype))
    vm += 2 * _nbytes((tm, tn), F32)
    return pl.pallas_call(
        _mm_body,
        grid=(m // tm, n // tn),
        in_specs=[pl.BlockSpec((tm, k), lambda i, j: (i, 0)),
                  pl.BlockSpec((k, tn), lambda i, j: (0, j))],
        out_specs=pl.BlockSpec((tm, tn), lambda i, j: (i, j)),
        out_shape=jax.ShapeDtypeStruct((m, n), out_dtype),
        compiler_params=_params(("parallel", "parallel"), vm),
        name=name,
    )(a, b)


def _dn_chunk_body(xc_ref, xp_ref, sm_ref, cw_ref, alog_ref, dtb_ref,
                   qd_ref, kd_ref, w_ref, u_ref, att_ref, gl_ref, ext_ref):
    c = pl.program_id(0)
    C = DN_CHUNK
    halo = SUBLANES
    ext_ref[0:halo, :] = jnp.where(c > 0, xp_ref[...], 0.0)
    ext_ref[halo:halo + C, :] = xc_ref[...]
    cw = cw_ref[...]
    y = cw[0:1, :] * ext_ref[pl.ds(halo - CONV_WIDTH + 1, C), :]
    for j in range(1, CONV_WIDTH):
        y = y + cw[j:j + 1, :] * ext_ref[pl.ds(halo - CONV_WIDTH + 1 + j, C), :]
    y = y * _sigmoid(y)

    sm = sm_ref[...]
    xa = sm + dtb_ref[...]
    softplus = jnp.maximum(xa, 0.0) + jnp.log1p(jnp.exp(-jnp.abs(xa)))
    g = -jnp.exp(alog_ref[...]) * softplus
    beta = _sigmoid(sm)

    row = lax.broadcasted_iota(I32, (C, LANES), 0)
    gc = g
    d = 1
    while d < C:
        gc = gc + jnp.where(row >= d, pltpu.roll(gc, d, axis=0), 0.0)
        d *= 2
    gct = jnp.concatenate([gc, jnp.zeros_like(gc)], axis=0).T
    ex = jnp.exp(gc)
    gc_last = gc[C - 1:C, :]
    exl = jnp.exp(gc_last - gc)
    gl_ref[...] = jnp.exp(jnp.broadcast_to(gct[SM_A:SM_A + DN_HEADS, C - 1:C], (DN_HEADS, LANES)))

    ri = lax.broadcasted_iota(I32, (C, C), 0)
    ci = lax.broadcasted_iota(I32, (C, C), 1)
    tril = ri >= ci
    strict = ri > ci
    eye = jnp.where(ri == ci, 1.0, 0.0).astype(F32)
    lvl_masks = []
    lb = 0
    while (1 << lb) < C:
        lvl_masks.append(((ri >> (lb + 1)) == (ci >> (lb + 1)))
                         & (((ri >> lb) & 1) == 1) & (((ci >> lb) & 1) == 0))
        lb += 1

    heads = range(DN_HEADS)
    a_mats, kbs, vbs = [], [], []
    for h in heads:
        sl = slice(h * DN_HEAD_DIM, (h + 1) * DN_HEAD_DIM)
        qh = y[:, h * DN_HEAD_DIM:(h + 1) * DN_HEAD_DIM]
        kh = y[:, DN_WIDTH + h * DN_HEAD_DIM:DN_WIDTH + (h + 1) * DN_HEAD_DIM]
        vh = y[:, 2 * DN_WIDTH + h * DN_HEAD_DIM:2 * DN_WIDTH + (h + 1) * DN_HEAD_DIM]
        qn = qh * lax.rsqrt(jnp.sum(qh * qh, axis=-1, keepdims=True) + L2_EPS) * (DN_HEAD_DIM ** -0.5)
        kn = kh * lax.rsqrt(jnp.sum(kh * kh, axis=-1, keepdims=True) + L2_EPS)
        bcol = beta[:, SM_B + h:SM_B + h + 1]
        gcol = gc[:, SM_A + h:SM_A + h + 1]
        grow = gct[SM_A + h:SM_A + h + 1, 0:C]
        dec = jnp.exp(jnp.where(tril, gcol - grow, -jnp.inf))
        kb = kn * bcol
        knb = kn.astype(BF16)
        excol = ex[:, SM_A + h:SM_A + h + 1]
        a_mats.append(jnp.where(strict, _dot_nt(kb.astype(BF16), knb) * dec, 0.0))
        kbs.append(kb * excol)
        vbs.append(vh * bcol)
        att = jnp.where(tril, _dot_nt(qn.astype(BF16), knb) * dec, 0.0)
        att_ref[:, h * C:(h + 1) * C] = att.astype(att_ref.dtype)
        qd_ref[:, sl] = (qn * excol).astype(qd_ref.dtype)
        kd_ref[:, sl] = (kn * exl[:, SM_A + h:SM_A + h + 1]).astype(kd_ref.dtype)

    x_inv = [eye - jnp.where(lvl_masks[0], a, 0.0) for a in a_mats]
    for lm in lvl_masks[1:]:
        xs = [_split_bf16(x) for x in x_inv]
        ts = [_dot3(xs[h], _split_bf16(jnp.where(lm, a_mats[h], 0.0))) for h in heads]
        x_inv = [x_inv[h] - _dot3(_split_bf16(ts[h]), xs[h]) for h in heads]
    xs = [_split_bf16(x) for x in x_inv]
    for h in heads:
        sl = slice(h * DN_HEAD_DIM, (h + 1) * DN_HEAD_DIM)
        w_ref[:, sl] = _dot3(xs[h], _split_bf16(kbs[h])).astype(w_ref.dtype)
        u_ref[:, sl] = _dot3(xs[h], _split_bf16(vbs[h]))


def dn_chunk(proj, small, conv_w, alog_pad, dtb_pad):
    s = proj.shape[0]
    C = DN_CHUNK
    n_chunks = s // C
    w3 = 3 * DN_WIDTH
    row_spec = lambda width: pl.BlockSpec((C, width), lambda c: (c, 0))
    vm = 2 * (_nbytes((C, w3), F32) + _nbytes((SUBLANES, w3), F32)) + 8 * _nbytes((C, w3), F32)
    return pl.pallas_call(
        _dn_chunk_body,
        grid=(n_chunks,),
        in_specs=[pl.BlockSpec((C, w3), lambda c: (c, COL_QKV // w3)),
                  pl.BlockSpec((SUBLANES, w3), lambda c: (jnp.maximum(c * (C // SUBLANES) - 1, 0), COL_QKV // w3)),
                  pl.BlockSpec((C, LANES), lambda c: (c, 0)),
                  pl.BlockSpec((CONV_WIDTH, w3), lambda c: (0, 0)),
                  pl.BlockSpec((1, LANES), lambda c: (0, 0)),
                  pl.BlockSpec((1, LANES), lambda c: (0, 0))],
        out_specs=[row_spec(DN_WIDTH), row_spec(DN_WIDTH), row_spec(DN_WIDTH),
                   row_spec(DN_WIDTH), row_spec(DN_HEADS * C),
                   pl.BlockSpec((DN_HEADS, LANES), lambda c: (c, 0))],
        out_shape=[jax.ShapeDtypeStruct((s, DN_WIDTH), BF16),
                   jax.ShapeDtypeStruct((s, DN_WIDTH), BF16),
                   jax.ShapeDtypeStruct((s, DN_WIDTH), BF16),
                   jax.ShapeDtypeStruct((s, DN_WIDTH), F32),
                   jax.ShapeDtypeStruct((s, DN_HEADS * C), BF16),
                   jax.ShapeDtypeStruct((n_chunks * DN_HEADS, LANES), F32)],
        scratch_shapes=[pltpu.VMEM((SUBLANES + C, w3), F32)],
        compiler_params=_params(("parallel",), vm),
        name="dn_chunk",
    )(proj, proj, small, conv_w, alog_pad, dtb_pad)


def _dn_scan_body(qd_ref, kd_ref, w_ref, u_ref, att_ref, gl_ref, z_ref, nrm_ref, o_ref, st_ref):
    c = pl.program_id(0)
    C = DN_CHUNK

    @pl.when(c == 0)
    def _():
        st_ref[...] = jnp.zeros_like(st_ref)

    heads = range(DN_HEADS)
    sls = [slice(h * DN_HEAD_DIM, (h + 1) * DN_HEAD_DIM) for h in heads]
    states = [st_ref[h] for h in heads]
    sbs = [s.astype(BF16) for s in states]
    w_s = [_dot(w_ref[:, sls[h]], sbs[h]) for h in heads]
    q_s = [_dot(qd_ref[:, sls[h]], sbs[h]) for h in heads]
    vbs = [(u_ref[:, sls[h]] - w_s[h]).astype(BF16) for h in heads]
    outs = [q_s[h] + _dot(att_ref[:, h * C:(h + 1) * C], vbs[h]) for h in heads]
    for h in heads:
        st_ref[h] = states[h] * gl_ref[h:h + 1, :] + _dot_tn(kd_ref[:, sls[h]], vbs[h])
    for h in heads:
        o = outs[h]
        ms = jnp.mean(o * o, axis=-1, keepdims=True)
        z = z_ref[:, sls[h]]
        o_ref[:, sls[h]] = (o * lax.rsqrt(ms + RMS_EPS) * nrm_ref[...] * (z * _sigmoid(z))).astype(o_ref.dtype)


def dn_scan(qd, kd, w, u, att, gl, proj, dn_norm):
    s = qd.shape[0]
    C = DN_CHUNK
    row = lambda width: pl.BlockSpec((C, width), lambda c: (c, 0))
    vm = 2 * 6 * _nbytes((C, DN_WIDTH), F32) + 2 * _nbytes((DN_HEADS, DN_HEAD_DIM, DN_HEAD_DIM), F32)
    return pl.pallas_call(
        _dn_scan_body,
        grid=(s // C,),
        in_specs=[row(DN_WIDTH), row(DN_WIDTH), row(DN_WIDTH), row(DN_WIDTH), row(DN_HEADS * C),
                  pl.BlockSpec((DN_HEADS, LANES), lambda c: (c, 0)),
                  pl.BlockSpec((C, DN_WIDTH), lambda c: (c, COL_Z // DN_WIDTH)),
                  pl.BlockSpec((1, DN_HEAD_DIM), lambda c: (0, 0))],
        out_specs=row(DN_WIDTH),
        out_shape=jax.ShapeDtypeStruct((s, DN_WIDTH), BF16),
        scratch_shapes=[pltpu.VMEM((DN_HEADS, DN_HEAD_DIM, DN_HEAD_DIM), F32)],
        compiler_params=_params(("arbitrary",), vm),
        name="dn_scan",
    )(qd, kd, w, u, att, gl, proj, dn_norm.reshape(1, DN_HEAD_DIM))


def _rotate_half(x, cs, sn, head_dim, first):
    half = head_dim // 2
    if head_dim == LANES:
        swapped = pltpu.roll(x, half, axis=1)
    else:
        swapped = jnp.where(first, pltpu.roll(x, LANES - half, axis=1), pltpu.roll(x, half, axis=1))
    return x * cs + swapped * sn


def _proj_rope_body(a_ref, b_ref, c_ref, s_ref, o_ref, *, head_dim, scale):
    acc = _dot(a_ref[...], b_ref[...])
    cs = c_ref[...]
    sn = s_ref[...]
    lane = lax.broadcasted_iota(I32, cs.shape, 1)
    first = (lane & (head_dim - 1)) < head_dim // 2
    for j in range(acc.shape[1] // LANES):
        sl = slice(j * LANES, (j + 1) * LANES)
        o_ref[:, sl] = (_rotate_half(acc[:, sl], cs, sn, head_dim, first) * scale).astype(o_ref.dtype)


def proj_rope(h, w_att, col_block, cos_t, sin_t, head_dim, scale=1.0, tm=1024):
    s, d = h.shape
    tm = min(tm, s)
    n = SA_WIDTH
    vm = 2 * (_nbytes((tm, d), BF16) + _nbytes((d, n), BF16) + _nbytes((tm, n), BF16)) + 4 * _nbytes((tm, n), F32)
    tab = pl.BlockSpec((tm, LANES), lambda i: (i, 0))
    return pl.pallas_call(
        functools.partial(_proj_rope_body, head_dim=head_dim, scale=scale),
        grid=(s // tm,),
        in_specs=[pl.BlockSpec((tm, d), lambda i: (i, 0)),
                  pl.BlockSpec((d, n), lambda i: (0, col_block)), tab, tab],
        out_specs=pl.BlockSpec((tm, n), lambda i: (i, 0)),
        out_shape=jax.ShapeDtypeStruct((s, n), BF16),
        compiler_params=_params(("parallel",), vm),
        name="proj_rope",
    )(h, w_att, cos_t, sin_t)


def _rope_kix_body(sm_ref, c_ref, s_ref, klo_ref, khi_ref):
    cs = c_ref[...]
    lane = lax.broadcasted_iota(I32, cs.shape, 1)
    first = (lane & (IDX_HEAD_DIM - 1)) < IDX_HEAD_DIM // 2
    k_rot = _rotate_half(sm_ref[...], cs, s_ref[...], IDX_HEAD_DIM, first)
    k_lo = jnp.where(lane < IDX_HEAD_DIM, k_rot, 0.0)
    klo_ref[...] = k_lo.astype(klo_ref.dtype)
    khi_ref[...] = pltpu.roll(k_lo, IDX_HEAD_DIM, axis=1).astype(khi_ref.dtype)


def rope_kix(small, cos_t, sin_t, tm=512):
    s = small.shape[0]
    tm = min(tm, s)
    tab = pl.BlockSpec((tm, LANES), lambda i: (i, 0))
    return pl.pallas_call(
        _rope_kix_body,
        grid=(s // tm,),
        in_specs=[tab, tab, tab],
        out_specs=[tab, tab],
        out_shape=[jax.ShapeDtypeStruct((s, LANES), BF16)] * 2,
        compiler_params=_params(("parallel",), 16 * _nbytes((tm, LANES), F32)),
        name="rope_kix",
    )(small, cos_t, sin_t)


def _index_body(q_ref, klo_ref, khi_ref, sm_ref, mask_ref, keys_ref, hi_ref, lo_ref, *, tq, tk, topk):
    i = pl.program_id(0)
    nkt = keys_ref.shape[0]
    nk = ((i + 1) * tq + tk - 1) // tk
    w_t = (sm_ref[...] * (IDX_HEADS ** -0.5 * IDX_HEAD_DIM ** -0.5)).T
    key_l = lax.broadcasted_iota(I32, (tk, tq), 0)
    qry_g = i * tq + lax.broadcasted_iota(I32, (tk, tq), 1)

    def score_tile(kt, carry):
        off = pl.multiple_of(kt * tk, tk)
        k_lo = klo_ref[pl.ds(off, tk), :]
        k_hi = khi_ref[pl.ds(off, tk), :]
        acc = jnp.zeros((tk, tq), F32)
        for j in range(IDX_HEADS // 2):
            qp = q_ref[:, j * LANES:(j + 1) * LANES]
            acc = acc + w_t[SM_WIX + 2 * j:SM_WIX + 2 * j + 1, :] * jnp.maximum(_dot_nt(k_lo, qp), 0.0)
            acc = acc + w_t[SM_WIX + 2 * j + 1:SM_WIX + 2 * j + 2, :] * jnp.maximum(_dot_nt(k_hi, qp), 0.0)
        sc = jnp.where(kt * tk + key_l <= qry_g, acc, -jnp.inf)
        bits = pltpu.bitcast(sc, I32)
        key = bits ^ ((bits >> 31) & 0x7FFFFFFF)
        keys_ref[kt] = key
        hi_ref[kt] = (key >> 16).astype(I16)
        return carry

    lax.fori_loop(0, nk, score_tile, 0)

    cnt_rows = 2 * I16_ROWS

    def count16(ref, cand, strict):
        def body(kt, cnt):
            t = ref[kt]
            hit = jnp.where((t > cand) if strict else (t >= cand), jnp.int16(1), jnp.int16(0))
            for j in range(tk // cnt_rows):
                cnt = cnt + hit[j * cnt_rows:(j + 1) * cnt_rows]
            return cnt

        cnt = lax.fori_loop(0, nk, body, jnp.zeros((cnt_rows, tq), I16))
        return jnp.sum(cnt.astype(F32), axis=0, keepdims=True)

    def kth_largest16(ref, kth):
        zero = jnp.zeros((1, tq), I32)
        ans = jnp.where(count16(ref, zero.astype(I16), False) >= kth, zero, I16_MIN)

        def bit_body(b, ans):
            cand = ans + lax.shift_left(jnp.int32(1), 14 - b)
            return jnp.where(count16(ref, cand.astype(I16), False) >= kth, cand, ans)

        return lax.fori_loop(0, 15, bit_body, ans)

    kf = jnp.full((1, tq), float(topk), F32)
    t_hi = kth_largest16(hi_ref, kf)
    above = count16(hi_ref, t_hi.astype(I16), True)

    def low_tile(kt, carry):
        key = keys_ref[kt]
        lo = (key & 0xFFFF) + I16_MIN
        lo_ref[kt] = jnp.where((key >> 16) == t_hi, lo, I16_MIN).astype(I16)
        return carry

    lax.fori_loop(0, nk, low_tile, 0)
    t_lo = kth_largest16(lo_ref, kf - above)
    ans = t_hi * 65536 + (t_lo - I16_MIN)

    def put(kt, sel):
        mask_ref[0, pl.ds(pl.multiple_of(kt * tk, tk), tk), :] = jnp.where(sel, 0.0, NEG_BIG).astype(mask_ref.dtype)

    def write(kt, cnt):
        sel = (keys_ref[kt] >= ans) & (kt * tk + key_l <= qry_g)
        put(kt, sel)
        return cnt + jnp.sum(jnp.where(sel, 1.0, 0.0).reshape(tk // cnt_rows, cnt_rows, tq), axis=0)

    kept = jnp.sum(lax.fori_loop(0, nk, write, jnp.zeros((cnt_rows, tq), F32)), axis=0, keepdims=True)

    @pl.when(jnp.max(kept) > float(topk))
    def _():
        def count_gt(kt, cnt):
            gt = (keys_ref[kt] > ans) & (kt * tk + key_l <= qry_g)
            return cnt + jnp.sum(jnp.where(gt, 1.0, 0.0), axis=0, keepdims=True)

        need = kf - lax.fori_loop(0, nk, count_gt, jnp.zeros((1, tq), F32))
        ri = lax.broadcasted_iota(I32, (tk, tk), 0)
        ci = lax.broadcasted_iota(I32, (tk, tk), 1)
        upto = jnp.where(ri >= ci, 1.0, 0.0).astype(BF16)

        def rewrite(kt, seen):
            key = keys_ref[kt]
            causal = kt * tk + key_l <= qry_g
            tie = (key == ans) & causal
            rank = seen + _dot(upto, jnp.where(tie, 1.0, 0.0).astype(BF16))
            put(kt, ((key > ans) & causal) | (tie & (rank <= need)))
            return rank[tk - 1:tk, :]

        lax.fori_loop(0, nk, rewrite, jnp.zeros((1, tq), F32))

    def clear(kt, carry):
        mask_ref[0, pl.ds(pl.multiple_of(kt * tk, tk), tk), :] = jnp.full((tk, tq), NEG_BIG, mask_ref.dtype)
        return carry

    lax.fori_loop(nk, nkt, clear, 0)


def index_mask(q_ix, k_lo, k_hi, small, topk, tq, tk, t_attn):
    s = q_ix.shape[0]
    nkt = s // tk
    per = t_attn // tq
    wq = IDX_HEADS * IDX_HEAD_DIM
    vm = (2 * (_nbytes((tq, wq), BF16) + 2 * _nbytes((s, LANES), BF16) + _nbytes((tq, LANES), F32)
               + _nbytes((s, tq), BF16)) + 2 * _nbytes((s, tq), I32) + 8 * _nbytes((tk, tq), F32))
    return pl.pallas_call(
        functools.partial(_index_body, tq=tq, tk=tk, topk=topk),
        grid=(s // tq,),
        in_specs=[pl.BlockSpec((tq, wq), lambda i: (i, 0)),
                  pl.BlockSpec((s, LANES), lambda i: (0, 0)),
                  pl.BlockSpec((s, LANES), lambda i: (0, 0)),
                  pl.BlockSpec((tq, LANES), lambda i: (i, 0))],
        out_specs=pl.BlockSpec((1, s, tq), lambda i: (i // per, 0, i % per)),
        out_shape=jax.ShapeDtypeStruct((s // t_attn, s, t_attn), BF16),
        scratch_shapes=[pltpu.VMEM((nkt, tk, tq), I32), pltpu.VMEM((nkt, tk, tq), I16), pltpu.VMEM((nkt, tk, tq), I16)],
        compiler_params=_params(("parallel",), vm),
        name="index_mask",
    )(q_ix, k_lo, k_hi, small)


def _attn_body(qi_ref, ki_ref, q_ref, k_ref, v_ref, mk_ref, o_ref, m_ref, l_ref, al_ref, acc_ref, bias_ref, s_ref):
    p = pl.program_id(0)
    qi = qi_ref[p]
    ki = ki_ref[p]

    @pl.when(ki == 0)
    def _():
        m_ref[...] = jnp.full_like(m_ref, NEG_BIG)
        l_ref[...] = jnp.zeros_like(l_ref)
        acc_ref[...] = jnp.zeros_like(acc_ref)

    bias_ref[...] = mk_ref[0].astype(F32)
    tk, tq = bias_ref.shape
    part = 4 * SUBLANES

    def fold(x, op):
        return op(op(x.reshape(tk // part, part, tq), axis=0), axis=0, keepdims=True)

    heads = range(SA_HEADS)
    sls = [slice(h * SA_HEAD_DIM, (h + 1) * SA_HEAD_DIM) for h in heads]
    for h in heads:
        s = _dot_nt(k_ref[:, sls[h]], q_ref[:, sls[h]]) + bias_ref[...]
        s_ref[h] = s
        m_old = m_ref[h:h + 1, :]
        m_new = jnp.maximum(m_old, fold(s, jnp.max))
        al_ref[h:h + 1, :] = jnp.exp2(m_old - m_new)
        m_ref[h:h + 1, :] = m_new
    for h in heads:
        pr = jnp.exp2(s_ref[h] - m_ref[h:h + 1, :])
        alpha = al_ref[h:h + 1, :]
        l_ref[h:h + 1, :] = alpha * l_ref[h:h + 1, :] + fold(pr, jnp.sum)
        acc_ref[h] = alpha * acc_ref[h] + _dot_tn(v_ref[:, sls[h]], pr.astype(BF16))

    @pl.when(ki == qi)
    def _():
        for h in range(SA_HEADS):
            sl = slice(h * SA_HEAD_DIM, (h + 1) * SA_HEAD_DIM)
            o_ref[:, sl] = (acc_ref[h] / l_ref[h:h + 1, :]).T.astype(o_ref.dtype)


def masked_attention(q, k, v, mask_t, t):
    s = q.shape[0]
    nb = s // t
    pairs = [(a, b) for a in range(nb) for b in range(a + 1)]
    qi = jnp.asarray([a for a, _ in pairs], I32)
    ki = jnp.asarray([b for _, b in pairs], I32)
    vm = (2 * (4 * _nbytes((t, SA_WIDTH), BF16) + _nbytes((t, t), BF16)) + _nbytes((t, SA_WIDTH), F32)
          + 2 * _nbytes((SA_HEADS, t), F32) + 8 * _nbytes((t, t), F32))
    grid_spec = pltpu.PrefetchScalarGridSpec(
        num_scalar_prefetch=2,
        grid=(len(pairs),),
        in_specs=[pl.BlockSpec((t, SA_WIDTH), lambda p, qi, ki: (qi[p], 0)),
                  pl.BlockSpec((t, SA_WIDTH), lambda p, qi, ki: (ki[p], 0)),
                  pl.BlockSpec((t, SA_WIDTH), lambda p, qi, ki: (ki[p], 0)),
                  pl.BlockSpec((1, t, t), lambda p, qi, ki: (qi[p], ki[p], 0))],
        out_specs=pl.BlockSpec((t, SA_WIDTH), lambda p, qi, ki: (qi[p], 0)),
        scratch_shapes=[pltpu.VMEM((SA_HEADS, t), F32),
                        pltpu.VMEM((SA_HEADS, t), F32),
                        pltpu.VMEM((SA_HEADS, t), F32),
                        pltpu.VMEM((SA_HEADS, SA_HEAD_DIM, t), F32),
                        pltpu.VMEM((t, t), F32),
                        pltpu.VMEM((SA_HEADS, t, t), F32)],
    )
    return pl.pallas_call(
        _attn_body,
        grid_spec=grid_spec,
        out_shape=jax.ShapeDtypeStruct((s, SA_WIDTH), BF16),
        compiler_params=_params(("arbitrary",), vm),
        name="masked_attention",
    )(qi, ki, q, k, v, mask_t)


def _merge_body(odn_ref, osa_ref, wdn_ref, wsa_ref, gdn_ref, gsa_ref, o_ref):
    y_dn = _dot(odn_ref[...], wdn_ref[...])
    y_sa = _dot(osa_ref[...], wsa_ref[...])
    o_ref[...] = (_sigmoid(gdn_ref[...]) * y_dn + _sigmoid(gsa_ref[...]) * y_sa).astype(o_ref.dtype)


def merge_branches(o_dn, o_sa, w_dn, w_sa, proj, tm=512, tn=1024):
    s, kd = o_dn.shape
    d = w_dn.shape[1]
    vm = 2 * (2 * _nbytes((tm, kd), BF16) + 2 * _nbytes((kd, tn), BF16) + 2 * _nbytes((tm, tn), F32)
              + _nbytes((tm, tn), BF16)) + 6 * _nbytes((tm, tn), F32)
    return pl.pallas_call(
        _merge_body,
        grid=(s // tm, d // tn),
        in_specs=[pl.BlockSpec((tm, kd), lambda i, j: (i, 0)),
                  pl.BlockSpec((tm, kd), lambda i, j: (i, 0)),
                  pl.BlockSpec((kd, tn), lambda i, j: (0, j)),
                  pl.BlockSpec((kd, tn), lambda i, j: (0, j)),
                  pl.BlockSpec((tm, tn), lambda i, j: (i, COL_GDN // tn + j)),
                  pl.BlockSpec((tm, tn), lambda i, j: (i, COL_GSA // tn + j))],
        out_specs=pl.BlockSpec((tm, tn), lambda i, j: (i, j)),
        out_shape=jax.ShapeDtypeStruct((s, d), BF16),
        compiler_params=_params(("parallel", "parallel"), vm),
        name="merge_branches",
    )(o_dn, o_sa, w_dn, w_sa, proj, proj)


def _out_proj_body(m_ref, w_ref, x_ref, gn_ref, o_ref, hn_ref):
    y = x_ref[...] + _dot(m_ref[...], w_ref[...])
    o_ref[...] = y
    hn_ref[...] = _rms(y, gn_ref[...]).astype(hn_ref.dtype)


def out_proj_norm(merged, w_o, x, next_gain, tm=512):
    s, k = merged.shape
    d = w_o.shape[1]
    tm = min(tm, s)
    vm = 2 * (_nbytes((tm, k), BF16) + _nbytes((k, d), BF16) + 2 * _nbytes((tm, d), F32) + _nbytes((tm, d), BF16)) + 3 * _nbytes((tm, d), F32)
    row = lambda width: pl.BlockSpec((tm, width), lambda i: (i, 0))
    return pl.pallas_call(
        _out_proj_body,
        grid=(s // tm,),
        in_specs=[row(k), pl.BlockSpec((k, d), lambda i: (0, 0)), row(d), pl.BlockSpec((1, d), lambda i: (0, 0))],
        out_specs=[row(d), row(d)],
        out_shape=[jax.ShapeDtypeStruct((s, d), F32), jax.ShapeDtypeStruct((s, d), BF16)],
        compiler_params=_params(("parallel",), vm),
        name="out_proj",
    )(merged, w_o, x, next_gain.reshape(1, d))


def _merge_out_body(odn_ref, osa_ref, wdn_ref, wsa_ref, gdn_ref, gsa_ref, wo_ref, x_ref, gn_ref, o_ref, hn_ref):
    y_dn = _dot(odn_ref[...], wdn_ref[...])
    y_sa = _dot(osa_ref[...], wsa_ref[...])
    merged = (_sigmoid(gdn_ref[...]) * y_dn + _sigmoid(gsa_ref[...]) * y_sa).astype(BF16)
    y = x_ref[...] + _dot(merged, wo_ref[...])
    o_ref[...] = y
    hn_ref[...] = _rms(y, gn_ref[...]).astype(hn_ref.dtype)


def merge_out(o_dn, o_sa, w_dn, w_sa, proj, w_o, x, next_gain, tm=256):
    s, kd = o_dn.shape
    d = w_o.shape[1]
    tm = min(tm, s)
    row = lambda width, cb=0: pl.BlockSpec((tm, width), lambda i: (i, cb))
    const = lambda shape: pl.BlockSpec(shape, lambda i: (0, 0))
    vm = (2 * (2 * _nbytes((kd, d), BF16) + _nbytes((d, d), BF16) + 2 * _nbytes((tm, kd), BF16)
               + 4 * _nbytes((tm, d), F32) + _nbytes((tm, d), BF16)) + 6 * _nbytes((tm, d), F32))
    return pl.pallas_call(
        _merge_out_body,
        grid=(s // tm,),
        in_specs=[row(kd), row(kd), const((kd, d)), const((kd, d)), row(d, COL_GDN // d), row(d, COL_GSA // d),
                  const((d, d)), row(d), const((1, d))],
        out_specs=[row(d), row(d)],
        out_shape=[jax.ShapeDtypeStruct((s, d), F32), jax.ShapeDtypeStruct((s, d), BF16)],
        compiler_params=_params(("parallel",), vm),
        name="merge_out",
    )(o_dn, o_sa, w_dn, w_sa, proj, proj, w_o, x, next_gain.reshape(1, d))


def _ffn_body(h_ref, x_ref, wg_ref, wu_ref, wd_ref, gn_ref, o_ref, hn_ref):
    f = pl.program_id(1)

    @pl.when(f == 0)
    def _():
        o_ref[...] = jnp.zeros_like(o_ref)

    h = h_ref[...]
    g = _dot(h, wg_ref[...])
    u = _dot(h, wu_ref[...])
    o_ref[...] += _dot((g * _sigmoid(g) * u).astype(BF16), wd_ref[...])

    @pl.when(f == pl.num_programs(1) - 1)
    def _():
        y = x_ref[...] + o_ref[...]
        o_ref[...] = y
        hn_ref[...] = _rms(y, gn_ref[...]).astype(hn_ref.dtype)


def ffn_dense(h, x, w_gate, w_up, w_down, next_gain, hn_dtype, tm=512, tf=512):
    s, d = h.shape
    ff = w_gate.shape[1]
    vm = (2 * (_nbytes((tm, d), BF16) + _nbytes((tm, d), hn_dtype) + 2 * _nbytes((tm, d), F32)
               + 3 * _nbytes((d, tf), BF16)) + 3 * _nbytes((tm, d), F32) + 6 * _nbytes((tm, tf), F32))
    row = pl.BlockSpec((tm, d), lambda i, f: (i, 0))
    return pl.pallas_call(
        _ffn_body,
        grid=(s // tm, ff // tf),
        in_specs=[row, row,
                  pl.BlockSpec((d, tf), lambda i, f: (0, f)),
                  pl.BlockSpec((d, tf), lambda i, f: (0, f)),
                  pl.BlockSpec((tf, d), lambda i, f: (f, 0)),
                  pl.BlockSpec((1, d), lambda i, f: (0, 0))],
        out_specs=[row, row],
        out_shape=[jax.ShapeDtypeStruct((s, d), F32), jax.ShapeDtypeStruct((s, d), hn_dtype)],
        compiler_params=_params(("parallel", "arbitrary"), vm),
        name="ffn_dense",
    )(h, x, w_gate, w_up, w_down, next_gain.reshape(1, d))


def _router_body(h_ref, wr_ref, w_ref, m1_ref, m2_ref, rank_ref, tot_ref, cnt_ref):
    logits = _dot(h_ref[...], wr_ref[...])
    lane = lax.broadcasted_iota(I32, logits.shape, 1)
    lg = jnp.where(lane < N_EXPERTS, logits, -jnp.inf)
    m1 = jnp.max(lg, axis=1, keepdims=True)
    i1 = jnp.min(jnp.where(lg == m1, lane, LANES), axis=1, keepdims=True)
    lg2 = jnp.where(lane == i1, -jnp.inf, lg)
    m2 = jnp.max(lg2, axis=1, keepdims=True)
    i2 = jnp.min(jnp.where(lg2 == m2, lane, LANES), axis=1, keepdims=True)
    e = jnp.exp(m2 - m1)
    first = lane == i1
    second = lane == i2
    w_ref[...] = jnp.where(first, 1.0 / (1.0 + e), 0.0) + jnp.where(second, e / (1.0 + e), 0.0)
    m1_ref[...] = jnp.where(first, 1.0, 0.0)
    m2_ref[...] = jnp.where(second, 1.0, 0.0)

    @pl.when(pl.program_id(0) == 0)
    def _():
        cnt_ref[...] = jnp.zeros_like(cnt_ref)

    tm = logits.shape[0]
    sel = jnp.where(first | second, 1.0, 0.0).astype(BF16)
    ri = lax.broadcasted_iota(I32, (tm, tm), 0)
    ci = lax.broadcasted_iota(I32, (tm, tm), 1)
    before = jnp.where(ri > ci, 1.0, 0.0).astype(BF16)
    run = cnt_ref[0:1, :]
    rank_ref[...] = run + _dot(before, sel)
    run = run + jnp.sum(sel.astype(F32), axis=0, keepdims=True)
    cnt_ref[...] = jnp.broadcast_to(run, cnt_ref.shape)
    tot_ref[...] = jnp.broadcast_to(run, tot_ref.shape)


def route_tokens(h, w_router_pad, tm=512):
    s, d = h.shape
    tm = min(tm, s)
    vm = 2 * (_nbytes((tm, d), BF16) + _nbytes((d, LANES), BF16) + 4 * _nbytes((tm, LANES), F32)) + 4 * _nbytes((tm, tm), F32)
    row = pl.BlockSpec((tm, LANES), lambda i: (i, 0))
    return pl.pallas_call(
        _router_body,
        grid=(s // tm,),
        in_specs=[pl.BlockSpec((tm, d), lambda i: (i, 0)),
                  pl.BlockSpec((d, LANES), lambda i: (0, 0))],
        out_specs=[row, row, row, row, pl.BlockSpec((SUBLANES, LANES), lambda i: (0, 0))],
        out_shape=[jax.ShapeDtypeStruct((s, LANES), F32)] * 4 + [jax.ShapeDtypeStruct((SUBLANES, LANES), F32)],
        scratch_shapes=[pltpu.VMEM((SUBLANES, LANES), F32)],
        compiler_params=_params(("arbitrary",), vm),
        name="moe_router",
    )(h, w_router_pad)


def _plan_body(w_ref, m1_ref, m2_ref, rank_ref, tot_ref, pw_ref, meta_ref, *, tile):
    lane8 = lax.broadcasted_iota(I32, (SUBLANES, LANES), 1)
    n = tot_ref[...]
    padded = jnp.floor((n + (tile - 1.0)) * (1.0 / tile)) * tile
    ends = padded
    d = 1
    while d < N_EXPERTS:
        ends = ends + jnp.where(lane8 >= d, pltpu.roll(ends, d, axis=1), 0.0)
        d *= 2
    start = (ends - padded)[0:1, :]
    posf = start + rank_ref[...]
    m1 = m1_ref[...]
    m2 = m2_ref[...]
    w = w_ref[...]
    lane = lax.broadcasted_iota(I32, w.shape, 1)
    cols = [jnp.sum(m1 * posf, axis=1, keepdims=True), jnp.sum(m2 * posf, axis=1, keepdims=True),
            jnp.sum(m1 * w, axis=1, keepdims=True), jnp.sum(m2 * w, axis=1, keepdims=True)]
    out = jnp.zeros_like(w)
    for j, col in enumerate(cols):
        out = jnp.where(lane == j, col, out)
    pw_ref[...] = out
    tile_start = lane8.astype(F32) * tile
    owner = jnp.zeros((SUBLANES, LANES), F32)
    for e in range(N_EXPERTS - 1):
        owner = owner + jnp.where(tile_start >= ends[:, e:e + 1], 1.0, 0.0)
    n_used = ends[:, N_EXPERTS - 1:N_EXPERTS] * (1.0 / tile)
    row8 = lax.broadcasted_iota(I32, (SUBLANES, LANES), 0)
    meta_ref[...] = jnp.where(row8 == 0, owner, jnp.broadcast_to(n_used, owner.shape))


def dispatch_plan(w, m1, m2, rank, totals, tile):
    s = w.shape[0]
    full = pl.BlockSpec((s, LANES), lambda: (0, 0))
    small = pl.BlockSpec((SUBLANES, LANES), lambda: (0, 0))
    vm = 2 * 5 * _nbytes((s, LANES), F32) + 8 * _nbytes((s, LANES), F32)
    return pl.pallas_call(
        functools.partial(_plan_body, tile=float(tile)),
        in_specs=[full, full, full, full, small],
        out_specs=[full, small],
        out_shape=[jax.ShapeDtypeStruct((s, LANES), F32), jax.ShapeDtypeStruct((SUBLANES, LANES), F32)],
        compiler_params=pltpu.CompilerParams(vmem_limit_bytes=int(min(vm, VMEM_CAP_BYTES))),
        name="moe_plan",
    )(w, m1, m2, rank, totals)


def _row_copy(src_ref, src_row, dst_ref, dst_row, sem):
    return pltpu.make_async_copy(src_ref.at[pl.ds(src_row, 1), :], dst_ref.at[pl.ds(dst_row, 1), :], sem)


def _dispatch_body(p1_ref, p2_ref, x_ref, xg_in_ref, xg_ref, sem, *, tt):
    del xg_in_ref
    base = pl.program_id(0) * tt

    def issue(r, carry):
        _row_copy(x_ref, r, xg_ref, p1_ref[base + r], sem).start()
        _row_copy(x_ref, r, xg_ref, p2_ref[base + r], sem).start()
        return carry

    lax.fori_loop(0, tt, issue, 0)

    def drain(r, carry):
        _row_copy(x_ref, 0, xg_ref, 0, sem).wait()
        _row_copy(x_ref, 0, xg_ref, 0, sem).wait()
        return carry

    lax.fori_loop(0, tt, drain, 0)


def dispatch_rows(x, pos1, pos2, n_rows, tt=512):
    s, d = x.shape
    tt = min(tt, s)
    grid_spec = pltpu.PrefetchScalarGridSpec(
        num_scalar_prefetch=2,
        grid=(s // tt,),
        in_specs=[pl.BlockSpec((tt, d), lambda i, p1, p2: (i, 0)),
                  pl.BlockSpec(memory_space=pl.ANY)],
        out_specs=pl.BlockSpec(memory_space=pl.ANY),
        scratch_shapes=[pltpu.SemaphoreType.DMA(())],
    )
    return pl.pallas_call(
        functools.partial(_dispatch_body, tt=tt),
        grid_spec=grid_spec,
        out_shape=jax.ShapeDtypeStruct((n_rows, d), x.dtype),
        input_output_aliases={3: 0},
        compiler_params=_params(("arbitrary",), 4 * _nbytes((tt, d), F32)),
        name="moe_dispatch",
    )(pos1, pos2, x, jnp.zeros((n_rows, d), x.dtype))


def _combine_body(p1_ref, p2_ref, x_ref, pw_ref, gn_ref, y_ref, o_ref, hn_ref, b1_ref, b2_ref, sem, *, tt):
    base = pl.program_id(0) * tt

    def issue(r, carry):
        _row_copy(y_ref, p1_ref[base + r], b1_ref, r, sem).start()
        _row_copy(y_ref, p2_ref[base + r], b2_ref, r, sem).start()
        return carry

    lax.fori_loop(0, tt, issue, 0)

    def drain(r, carry):
        _row_copy(y_ref, 0, b1_ref, 0, sem).wait()
        _row_copy(y_ref, 0, b2_ref, 0, sem).wait()
        return carry

    lax.fori_loop(0, tt, drain, 0)
    pw = pw_ref[...]
    y = x_ref[...] + pw[:, 2:3] * b1_ref[...] + pw[:, 3:4] * b2_ref[...]
    o_ref[...] = y
    hn_ref[...] = _rms(y, gn_ref[...]).astype(hn_ref.dtype)


def combine_rows(x, pw, y, pos1, pos2, next_gain, hn_dtype, tt=512):
    s, d = x.shape
    tt = min(tt, s)
    grid_spec = pltpu.PrefetchScalarGridSpec(
        num_scalar_prefetch=2,
        grid=(s // tt,),
        in_specs=[pl.BlockSpec((tt, d), lambda i, p1, p2: (i, 0)),
                  pl.BlockSpec((tt, LANES), lambda i, p1, p2: (i, 0)),
                  pl.BlockSpec((1, d), lambda i, p1, p2: (0, 0)),
                  pl.BlockSpec(memory_space=pl.ANY)],
        out_specs=[pl.BlockSpec((tt, d), lambda i, p1, p2: (i, 0)), pl.BlockSpec((tt, d), lambda i, p1, p2: (i, 0))],
        scratch_shapes=[pltpu.VMEM((tt, d), F32), pltpu.VMEM((tt, d), F32), pltpu.SemaphoreType.DMA(())],
    )
    return pl.pallas_call(
        functools.partial(_combine_body, tt=tt),
        grid_spec=grid_spec,
        out_shape=[jax.ShapeDtypeStruct((s, d), F32), jax.ShapeDtypeStruct((s, d), hn_dtype)],
        compiler_params=_params(("arbitrary",), 12 * _nbytes((tt, d), F32)),
        name="moe_combine",
    )(pos1, pos2, x, pw, next_gain.reshape(1, d), y)


def _moe_body(te_ref, nu_ref, xg_ref, gain_ref, wg_ref, wu_ref, wd_ref, o_ref, h_ref, acc_ref):
    i = pl.program_id(0)
    f = pl.program_id(1)

    @pl.when(i < nu_ref[0])
    def _():
        @pl.when(f == 0)
        def _():
            h_ref[...] = _rms(xg_ref[...], gain_ref[...]).astype(h_ref.dtype)
            acc_ref[...] = jnp.zeros_like(acc_ref)

        h = h_ref[...]
        g = _dot(h, wg_ref[0])
        u = _dot(h, wu_ref[0])
        acc_ref[...] += _dot((g * _sigmoid(g) * u).astype(BF16), wd_ref[0])

        @pl.when(f == pl.num_programs(1) - 1)
        def _():
            o_ref[...] = acc_ref[...]

    @pl.when((i >= nu_ref[0]) & (f == 0))
    def _():
        o_ref[...] = jnp.zeros_like(o_ref)


def moe_experts(xg, gain, tile_expert, n_used, w_gate, w_up, w_down, tm, tf=256):
    p, d = xg.shape
    ff = w_gate.shape[2]
    nf = ff // tf
    vm = (2 * (2 * _nbytes((tm, d), F32) + 3 * _nbytes((d, tf), BF16)) + _nbytes((tm, d), BF16)
          + _nbytes((tm, d), F32) + 6 * _nbytes((tm, tf), F32))

    def row(i, f, te, nu):
        return (jnp.minimum(i, nu[0] - 1), 0)

    def fcol(i, f, nu):
        return jnp.where(i < nu[0], f, nf - 1)

    grid_spec = pltpu.PrefetchScalarGridSpec(
        num_scalar_prefetch=2,
        grid=(p // tm, nf),
        in_specs=[pl.BlockSpec((tm, d), row),
                  pl.BlockSpec((1, d), lambda i, f, te, nu: (0, 0)),
                  pl.BlockSpec((1, d, tf), lambda i, f, te, nu: (te[i], 0, fcol(i, f, nu))),
                  pl.BlockSpec((1, d, tf), lambda i, f, te, nu: (te[i], 0, fcol(i, f, nu))),
                  pl.BlockSpec((1, tf, d), lambda i, f, te, nu: (te[i], fcol(i, f, nu), 0))],
        out_specs=pl.BlockSpec((tm, d), lambda i, f, te, nu: (i, 0)),
        scratch_shapes=[pltpu.VMEM((tm, d), BF16), pltpu.VMEM((tm, d), F32)],
    )
    return pl.pallas_call(
        _moe_body,
        grid_spec=grid_spec,
        out_shape=jax.ShapeDtypeStruct((p, d), F32),
        compiler_params=_params(("arbitrary", "arbitrary"), vm),
        name="moe_experts",
    )(tile_expert, n_used, xg, gain.reshape(1, d), w_gate, w_up, w_down)


def _rope_tables(positions, dim, reps):
    inv_freq = ROPE_THETA ** (-jnp.arange(0, dim, 2, dtype=F32) / dim)
    ang = positions.astype(F32)[:, None] * inv_freq
    cos, sin = jnp.cos(ang), jnp.sin(ang)
    return jnp.tile(cos, (1, 2 * reps)), jnp.tile(jnp.concatenate([-sin, sin], axis=1), (1, reps))


def _split_w_in(w):
    d = w.shape[0]
    o_a = 4 * DN_WIDTH
    o_qsa = o_a + 2 * DN_HEADS
    o_kix = o_qsa + 3 * SA_WIDTH + IDX_HEADS * IDX_HEAD_DIM
    o_wix = o_kix + IDX_HEAD_DIM
    o_g = o_wix + IDX_HEADS
    main = jnp.concatenate([w[:, :o_a], w[:, o_g:]], axis=1)
    small = jnp.concatenate([w[:, o_kix:o_wix], w[:, o_a:o_qsa], w[:, o_wix:o_g],
                             jnp.zeros((d, LANES - IDX_HEAD_DIM - 2 * DN_HEADS - IDX_HEADS), w.dtype)], axis=1)
    return main.astype(BF16), w[:, o_qsa:o_kix].astype(BF16), small.astype(BF16)


def _pad_lanes(v, offset):
    return jnp.zeros((1, LANES), F32).at[0, offset:offset + v.shape[0]].set(v.astype(F32))


def _mixer(x, h, cos_sa, sin_sa, cos_ix, sin_ix, ffn_gain, w_in, conv_w, a_log, dt_bias, dn_norm,
           w_dn_out, w_sa_out, w_o):
    s = x.shape[0]
    w_main, w_att, w_small = _split_w_in(w_in)
    proj = matmul(h, w_main, F32, tm=min(1024, s), tn=1024, name="in_proj")
    small = matmul(h, w_small, F32, tm=min(1024, s), tn=LANES, name="in_proj_small")

    qd, kd, w, u, att, gl = dn_chunk(proj, small, conv_w.astype(F32), _pad_lanes(a_log, SM_A), _pad_lanes(dt_bias, SM_A))
    o_dn = dn_scan(qd, kd, w, u, att, gl, proj, dn_norm.astype(F32))

    q_sa = proj_rope(h, w_att, BLK_QSA, cos_sa, sin_sa, SA_HEAD_DIM, scale=SA_HEAD_DIM ** -0.5 * LOG2_E)
    k_sa = proj_rope(h, w_att, BLK_KSA, cos_sa, sin_sa, SA_HEAD_DIM)
    v_sa = matmul(h, w_att[:, BLK_VSA * SA_WIDTH:(BLK_VSA + 1) * SA_WIDTH], BF16, tm=min(512, s), tn=SA_WIDTH, name="v_proj")
    q_ix = proj_rope(h, w_att, BLK_QIX, cos_ix, sin_ix, IDX_HEAD_DIM)
    k_lo, k_hi = rope_kix(small, cos_ix, sin_ix)
    t_attn = min(512, s)
    mask_t = index_mask(q_ix, k_lo, k_hi, small, min(TOPK_MAX, s // 4), tq=min(256, s), tk=t_attn, t_attn=t_attn)
    o_sa = masked_attention(q_sa, k_sa, v_sa, mask_t, t=t_attn)

    return merge_out(o_dn, o_sa, w_dn_out.astype(BF16), w_sa_out.astype(BF16), proj, w_o.astype(BF16), x,
                     ffn_gain.astype(F32))


def _moe_layer(x, h, norm_gain, w_router, w_gate, w_up, w_down, next_gain, hn_dtype):
    s, d = x.shape
    tile = min(512, s)
    n_tiles = (2 * s) // tile + N_EXPERTS
    wr = jnp.zeros((d, LANES), BF16).at[:, :N_EXPERTS].set(w_router.astype(BF16))
    w, m1, m2, rank, totals = route_tokens(h, wr)
    pw, meta = dispatch_plan(w, m1, m2, rank, totals, tile)
    pos1 = pw[:, 0].astype(I32)
    pos2 = pw[:, 1].astype(I32)
    tile_expert = meta[0, :n_tiles].astype(I32)
    n_used = meta[1, :1].astype(I32)
    xg = dispatch_rows(x, pos1, pos2, n_tiles * tile)
    y = moe_experts(xg, norm_gain.astype(F32), tile_expert, n_used, w_gate.astype(BF16), w_up.astype(BF16),
                    w_down.astype(BF16), tm=tile)
    return combine_rows(x, pw, y, pos1, pos2, next_gain.astype(F32), hn_dtype)


def kernel(x, positions, norm_mix, w_in, conv_w, a_log, dt_bias, dn_norm, w_dn_out, w_sa_out, w_o, norm_ffn, dense_w_gate, dense_w_up, dense_w_down, moe_router, moe_w_gate, moe_w_up, moe_w_down, final_norm):
    b, s, d = x.shape
    depth = norm_mix.shape[0]
    outs = []
    for bi in range(b):
        xb = x[bi]
        pos = positions[bi]
        cos_sa, sin_sa = _rope_tables(pos, SA_HEAD_DIM, 1)
        cos_ix, sin_ix = _rope_tables(pos, IDX_HEAD_DIM, 2)
        h = rmsnorm(xb, norm_mix[0], BF16)
        for layer in range(depth):
            xb, h = _mixer(xb, h, cos_sa, sin_sa, cos_ix, sin_ix, norm_ffn[layer], w_in[layer], conv_w[layer],
                           a_log[layer], dt_bias[layer], dn_norm[layer], w_dn_out[layer], w_sa_out[layer], w_o[layer])
            last = layer == depth - 1
            next_gain = (final_norm if last else norm_mix[layer + 1]).astype(F32)
            hn_dtype = x.dtype if last else BF16
            j = layer // 2
            if layer % 2 == 0:
                xb, h = ffn_dense(h, xb, dense_w_gate[j].astype(BF16), dense_w_up[j].astype(BF16),
                                  dense_w_down[j].astype(BF16), next_gain, hn_dtype)
            else:
                xb, h = _moe_layer(xb, h, norm_ffn[layer], moe_router[j], moe_w_gate[j], moe_w_up[j], moe_w_down[j],
                                   next_gain, hn_dtype)
        outs.append(h)
    return jnp.stack(outs, axis=0)
```
